```python
import jax, jax.numpy as jnp
from jax import lax
import numpy as np

D_MODEL = 2048
BATCH = 2
SEQ = 4096
DEPTH = 4

CTX_LEN = 256
GRID_W = 64
MIX_WIDTH = D_MODEL
M_WIDTH = MIX_WIDTH // 2
R_WIDTH = MIX_WIDTH - M_WIDTH
M_HEADS = 4
M_DV = M_WIDTH // M_HEADS
M_DQK = M_DV // 2
M_QK = M_HEADS * M_DQK
M_CHUNK = 64
R_BLOCKS = 16
R_BLOCK = R_WIDTH // R_BLOCKS
CONV_W = 4
CONV_PAD = (1, 2)
LRU_C = 8.0
EPS = 1e-6
IN_SIZES = (M_QK, M_QK, M_WIDTH, M_WIDTH, M_WIDTH, 4 * M_HEADS, R_WIDTH, R_WIDTH)
IN_WIDTH = sum(IN_SIZES)
IN_SPLITS = [int(s) for s in np.cumsum(IN_SIZES)[:-1]]

kernel_name = 'hymba_mlstm_rglru_prefix_dit'


def rmsnorm(t, g):
    tf = t.astype(jnp.float32)
    y = tf * lax.rsqrt(jnp.mean(tf * tf, axis=-1, keepdims=True) + EPS)
    return (y * g.astype(jnp.float32)).astype(t.dtype)


def dwconv(t, w, b):
    y = lax.conv_general_dilated(t, w[:, None, :].astype(t.dtype), (1,), [CONV_PAD],
                                 dimension_numbers=('NWC', 'WIO', 'NWC'),
                                 feature_group_count=t.shape[-1])
    return y + b.astype(t.dtype)


def to_col_major(t):
    b, n, ch = t.shape
    rows = n // GRID_W
    return t.reshape(b, rows, GRID_W, ch).transpose(0, 2, 1, 3).reshape(b, n, ch)


def to_row_major(t):
    b, n, ch = t.shape
    rows = n // GRID_W
    return t.reshape(b, GRID_W, rows, ch).transpose(0, 2, 1, 3).reshape(b, n, ch)


def mlstm_chunkwise(q, k, v, ig, lf, state):
    bsz, nh, t, _ = q.shape
    nc = t // M_CHUNK

    def chunks(a):
        a = a.reshape(a.shape[:2] + (nc, M_CHUNK) + a.shape[3:])
        return jnp.moveaxis(a, 2, 0)

    causal = jnp.tril(jnp.ones((M_CHUNK, M_CHUNK), dtype=bool))

    def step(carry, inp):
        c_st, n_st, m_st = carry
        qc, kc, vc, ic, fc = inp
        b = jnp.cumsum(fc, axis=-1)
        dmat = jnp.where(causal, b[..., :, None] - b[..., None, :] + ic[..., None, :], -jnp.inf)
        g = b + m_st[..., None]
        m_t = jnp.maximum(g, jnp.max(dmat, axis=-1))
        inter = jnp.exp(g - m_t)
        s = jnp.einsum('bhtd,bhsd->bhts', qc, kc) * jnp.exp(dmat - m_t[..., None])
        num = inter[..., None] * jnp.einsum('bhvd,bhtd->bhtv', c_st, qc) + jnp.einsum('bhts,bhsv->bhtv', s, vc)
        nq = inter * jnp.einsum('bhd,bhtd->bht', n_st, qc) + jnp.sum(s, axis=-1)
        h = num / jnp.maximum(jnp.abs(nq), jnp.exp(-m_t))[..., None]
        m_new = m_t[..., -1]
        w_s = jnp.exp(b[..., -1:] - b + ic - m_new[..., None])
        decay = jnp.exp(b[..., -1] + m_st - m_new)
        c_new = decay[..., None, None] * c_st + jnp.einsum('bhs,bhsv,bhsd->bhvd', w_s, vc, kc)
        n_new = decay[..., None] * n_st + jnp.einsum('bhs,bhsd->bhd', w_s, kc)
        return (c_new, n_new, m_new), h

    state, hs = lax.scan(step, state, (chunks(q), chunks(k), chunks(v), chunks(ig), chunks(lf)))
    hs = jnp.moveaxis(hs, 0, 2).reshape(bsz, nh, t, v.shape[-1])
    return hs, state


def mlstm_zero_state(bsz):
    f32 = jnp.float32
    return (jnp.zeros((bsz, M_HEADS, M_DV, M_DQK), f32), jnp.zeros((bsz, M_HEADS, M_DQK), f32),
            jnp.zeros((bsz, M_HEADS), f32))


def mlstm_inputs(qk_raw, v, gates, conv_w, conv_b, b_gate):
    f32 = jnp.float32
    qk = jax.nn.silu(dwconv(qk_raw, conv_w, conv_b)).astype(f32)
    q, k = jnp.split(qk, 2, axis=-1)

    def heads(a, dh):
        return a.reshape(a.shape[0], a.shape[1], M_HEADS, dh).transpose(0, 2, 1, 3)

    q = heads(q, M_DQK)
    k = heads(k, M_DQK) * (M_DQK ** -0.5)
    vh = heads(v.astype(f32), M_DV)
    gt = (gates.astype(f32) + b_gate.astype(f32)).transpose(0, 2, 1)
    i_f, f_f, i_b, f_b = jnp.split(gt, 4, axis=1)
    return q, k, vh, (i_f, jax.nn.log_sigmoid(f_f)), (i_b, jax.nn.log_sigmoid(f_b))


def mlstm_bidir(q, k, v, gf, gb, state_f, state_b):
    h_f, s_f = mlstm_chunkwise(q, k, v, gf[0], gf[1], state_f)
    fl = lambda a: jnp.flip(a, axis=2)
    h_b, s_b = mlstm_chunkwise(fl(q), fl(k), fl(v), fl(gb[0]), fl(gb[1]), state_b)
    return h_f + fl(h_b), s_f, s_b


def heads_to_seq(h):
    b, nh, t, dv = h.shape
    return h.transpose(0, 2, 1, 3).reshape(b, t, nh * dv)


def mlstm_out(hseq, o, z, g):
    f32 = jnp.float32
    b, t, _ = hseq.shape
    hh = hseq.reshape(b, t, M_HEADS, M_DV)
    hh = hh * lax.rsqrt(jnp.mean(hh * hh, axis=-1, keepdims=True) + EPS)
    hn = hh.reshape(b, t, M_WIDTH) * g.astype(f32)
    return hn * jax.nn.sigmoid(o.astype(f32)) * jax.nn.silu(z.astype(f32))


def _lin_comb(left, right):
    a_l, u_l = left
    a_r, u_r = right
    return a_l * a_r, a_r * u_l + u_r


def rglru(xc, w_r, b_r, w_i, b_i, lam, h0):
    bsz, t, ch = xc.shape
    xb = xc.reshape(bsz, t, R_BLOCKS, R_BLOCK)
    r = jax.nn.sigmoid(jnp.einsum('btgi,gij->btgj', xb, w_r).reshape(bsz, t, ch) + b_r)
    ii = jax.nn.sigmoid(jnp.einsum('btgi,gij->btgj', xb, w_i).reshape(bsz, t, ch) + b_i)
    log_a = -LRU_C * r * jax.nn.softplus(-lam)
    a = jnp.exp(log_a)
    u = jnp.sqrt(-jnp.expm1(2.0 * log_a)) * (ii * xc)
    a_cum, h = lax.associative_scan(_lin_comb, (a, u), axis=1)
    h = h + a_cum * h0[:, None, :]
    return h, h[:, -1]


def rglru_branch(xr_lat, xr_ctx, conv_w, conv_b, w_rg, b_rg, lam):
    f32 = jnp.float32
    xl = dwconv(xr_lat, conv_w, conv_b).astype(f32)
    xc = dwconv(xr_ctx, conv_w, conv_b).astype(f32)
    w_rg = w_rg.astype(f32)
    b_rg = b_rg.astype(f32)
    lam = lam.astype(f32)
    fwd = (w_rg[0, 0], b_rg[0, 0], w_rg[0, 1], b_rg[0, 1], lam[0])
    bwd = (w_rg[1, 0], b_rg[1, 0], w_rg[1, 1], b_rg[1, 1], lam[1])
    zeros = jnp.zeros((xc.shape[0], R_WIDTH), f32)
    h_cf, s_f = rglru(xc, *fwd, zeros)
    h_cb, s_b = rglru(jnp.flip(xc, 1), *bwd, zeros)
    h_lf, _ = rglru(xl, *fwd, s_f)
    h_lb, _ = rglru(jnp.flip(xl, 1), *bwd, s_b)
    return h_lf + jnp.flip(h_lb, 1), h_cf + jnp.flip(h_cb, 1)


def setup_inputs(seed: int = 0) -> dict:
    key = jax.random.key(seed)
    ks = jax.random.split(key, 24)
    f32 = jnp.float32
    L, D, H = DEPTH, D_MODEL, M_HEADS

    def nrm(k, shape, scale=1.0):
        return jax.random.normal(k, shape, f32) * scale

    f_bias = jnp.linspace(3.0, 6.0, H, dtype=f32)
    i_noise = nrm(ks[8], (L, 2, H), 0.01)
    f_noise = nrm(ks[9], (L, 2, H), 0.01)
    b_gate = jnp.concatenate([i_noise[:, 0], f_bias + f_noise[:, 0], i_noise[:, 1], f_bias + f_noise[:, 1]], axis=-1)
    u = jax.random.uniform(ks[15], (L, 2, R_WIDTH), f32, 0.9, 0.999)
    s = u ** (1.0 / LRU_C)
    lru_lambda = jnp.log(s) - jnp.log1p(-s)
    return {
        'x': nrm(ks[0], (BATCH, SEQ, D)),
        'c': nrm(ks[1], (BATCH, D)),
        'ctx': nrm(ks[2], (BATCH, CTX_LEN, D)),
        'c_ctx': nrm(ks[3], (D,)),
        'w_mod': nrm(ks[4], (L, D, 3 * D), D ** -0.5),
        'b_mod': nrm(ks[5], (L, 3 * D), 0.01),
        'norm_g': 1.0 + nrm(ks[6], (L, D), 0.01),
        'w_in': nrm(ks[7], (L, D, IN_WIDTH), D ** -0.5),
        'b_gate': b_gate,
        'conv_qk_w': nrm(ks[10], (L, CONV_W, 2 * M_QK), CONV_W ** -0.5),
        'conv_qk_b': nrm(ks[11], (L, 2 * M_QK), 0.01),
        'm_norm_g': 1.0 + nrm(ks[12], (L, M_WIDTH), 0.01),
        'conv_r_w': nrm(ks[13], (L, CONV_W, R_WIDTH), CONV_W ** -0.5),
        'conv_r_b': nrm(ks[14], (L, R_WIDTH), 0.01),
        'w_rg': nrm(ks[16], (L, 2, 2, R_BLOCKS, R_BLOCK, R_BLOCK), R_BLOCK ** -0.5),
        'b_rg': nrm(ks[17], (L, 2, 2, R_WIDTH), 0.01),
        'lru_lambda': lru_lambda,
        'w_out': nrm(ks[18], (L, MIX_WIDTH, D), MIX_WIDTH ** -0.5),
        'final_g': 1.0 + nrm(ks[19], (D,), 0.01),
    }


def reference(x, c, ctx, c_ctx, w_mod, b_mod, norm_g, w_in, b_gate, conv_qk_w, conv_qk_b, m_norm_g,
              conv_r_w, conv_r_b, w_rg, b_rg, lru_lambda, w_out, final_g):
    bsz = x.shape[0]
    for l in range(DEPTH):
        last = l == DEPTH - 1
        mod_x = jax.nn.silu(c) @ w_mod[l] + b_mod[l]
        mod_c = jax.nn.silu(c_ctx) @ w_mod[l] + b_mod[l]
        sh_x, sc_x, gt_x = jnp.split(mod_x[:, None, :], 3, axis=-1)
        sh_c, sc_c, gt_c = jnp.split(mod_c, 3, axis=-1)
        hx = rmsnorm(x, norm_g[l]) * (1.0 + sc_x) + sh_x
        hc = rmsnorm(ctx, norm_g[l]) * (1.0 + sc_c) + sh_c
        px = jnp.split(hx @ w_in[l], IN_SPLITS, axis=-1)
        pc = jnp.split(hc @ w_in[l], IN_SPLITS, axis=-1)

        qc_, kc_, vc_, gfc, gbc = mlstm_inputs(jnp.concatenate(pc[0:2], axis=-1), pc[2], pc[5],
                                               conv_qk_w[l], conv_qk_b[l], b_gate[l])
        h_c, st_f, st_b = mlstm_bidir(qc_, kc_, vc_, gfc, gbc, mlstm_zero_state(bsz), mlstm_zero_state(bsz))
        ql, kl, vl, gfl, gbl = mlstm_inputs(to_col_major(jnp.concatenate(px[0:2], axis=-1)),
                                            to_col_major(px[2]), to_col_major(px[5]),
                                            conv_qk_w[l], conv_qk_b[l], b_gate[l])
        h_l, _, _ = mlstm_bidir(ql, kl, vl, gfl, gbl, st_f, st_b)
        ym_x = mlstm_out(to_row_major(heads_to_seq(h_l)), px[3], px[4], m_norm_g[l])

        yr_x, yr_c = rglru_branch(px[6], pc[6], conv_r_w[l], conv_r_b[l], w_rg[l], b_rg[l], lru_lambda[l])
        yr_x = yr_x * jax.nn.silu(px[7].astype(jnp.float32))

        yx = jnp.concatenate([ym_x, yr_x], axis=-1).astype(x.dtype)
        x = x + gt_x * (yx @ w_out[l])
        if not last:
            ym_c = mlstm_out(heads_to_seq(h_c), pc[3], pc[4], m_norm_g[l])
            yr_c = yr_c * jax.nn.silu(pc[7].astype(jnp.float32))
            yc = jnp.concatenate([ym_c, yr_c], axis=-1).astype(ctx.dtype)
            ctx = ctx + gt_c * (yc @ w_out[l])
    return rmsnorm(x, final_g)
```

```python
import functools

import jax
import jax.numpy as jnp
from jax import lax
from jax.experimental import pallas as pl
from jax.experimental.pallas import tpu as pltpu

D_MODEL = 2048
DEPTH = 4
CTX_LEN = 256
GRID_W = 64
M_WIDTH = 1024
R_WIDTH = 1024
M_HEADS = 4
M_DV = 256
M_DQK = 128
M_QK = 512
R_BLOCKS = 16
R_BLOCK = 64
CONV_W = 4
LRU_C = 8.0
EPS = 1e-6

LANES = 128
SUBLANES = 8
SEC = 1024
N_SEC = 6
SEC_QK, SEC_V, SEC_O, SEC_ZM, SEC_XL, SEC_ZL = range(N_SEC)
GATE_W = 2 * LANES
CHUNK = 256
CHUNK_COLS = CHUNK // GRID_W
LRU_BLK = 256
LRU_TILE = 256
VMEM_LIMIT = 56 * 1024 * 1024

F32 = jnp.float32
BF16 = jnp.bfloat16


def _cparams(sem):
    return pltpu.CompilerParams(dimension_semantics=sem, vmem_limit_bytes=VMEM_LIMIT)


def _mod_kernel(c_ref, w_ref, b_ref, o_ref):
    c = c_ref[...]
    s = (c * jax.nn.sigmoid(c)).astype(BF16)
    o_ref[...] = jnp.dot(s, w_ref[...].astype(BF16), preferred_element_type=F32) + b_ref[...]


def _modulation(cvec, w_mod, b_mod):
    depth, d, n = w_mod.shape
    tn = 1024
    return pl.pallas_call(
        _mod_kernel,
        grid=(depth, n // tn),
        in_specs=[
            pl.BlockSpec((SUBLANES, d), lambda l, j: (0, 0)),
            pl.BlockSpec((None, d, tn), lambda l, j: (l, 0, j)),
            pl.BlockSpec((None, 1, tn), lambda l, j: (l, 0, j)),
        ],
        out_specs=pl.BlockSpec((None, SUBLANES, tn), lambda l, j: (l, 0, j)),
        out_shape=jax.ShapeDtypeStruct((depth, SUBLANES, n), F32),
        compiler_params=_cparams(("arbitrary", "arbitrary")),
        name="mod",
    )(cvec, w_mod, b_mod.reshape(depth, 1, n))


def _in_proj_kernel(x_ref, sh_ref, sc_ref, g_ref, w_ref, wg_ref, p_ref, gate_ref, h_scr):
    @pl.when(pl.program_id(1) == 0)
    def _():
        x = x_ref[...]
        ms = jnp.mean(x * x, axis=-1, keepdims=True)
        y = x * lax.rsqrt(ms + EPS) * g_ref[...]
        h = (y * (1.0 + sc_ref[...]) + sh_ref[...]).astype(BF16)
        h_scr[...] = h
        gate_ref[...] = jnp.dot(h, wg_ref[...], preferred_element_type=F32)

    p_ref[...] = jnp.dot(h_scr[...], w_ref[...], preferred_element_type=F32)


def _in_proj(x2d, mod_l, row_of_tile, norm_g, w_main, w_gate, tm):
    m, d = x2d.shape
    return pl.pallas_call(
        _in_proj_kernel,
        grid=(m // tm, N_SEC),
        in_specs=[
            pl.BlockSpec((tm, d), lambda i, n: (i, 0)),
            pl.BlockSpec((None, None, 1, d), lambda i, n: (row_of_tile(i), 0, 0, 0)),
            pl.BlockSpec((None, None, 1, d), lambda i, n: (row_of_tile(i), 1, 0, 0)),
            pl.BlockSpec((1, d), lambda i, n: (0, 0)),
            pl.BlockSpec((d, SEC), lambda i, n: (0, n)),
            pl.BlockSpec((d, GATE_W), lambda i, n: (0, 0)),
        ],
        out_specs=[
            pl.BlockSpec((None, tm, SEC), lambda i, n: (n, i, 0)),
            pl.BlockSpec((tm, GATE_W), lambda i, n: (i, 0)),
        ],
        out_shape=[
            jax.ShapeDtypeStruct((N_SEC, m, SEC), F32),
            jax.ShapeDtypeStruct((m, GATE_W), F32),
        ],
        scratch_shapes=[pltpu.VMEM((tm, d), BF16)],
        compiler_params=_cparams(("arbitrary", "arbitrary")),
        name="in_proj",
    )(x2d, mod_l, mod_l, norm_g, w_main, w_gate)


def _scan_rows(x, op, ident, reverse):
    n = x.shape[0]
    row = lax.broadcasted_iota(jnp.int32, x.shape, 0)
    k = 1
    while k < n:
        if reverse:
            shifted = jnp.where(row < n - k, pltpu.roll(x, n - k, axis=0), ident)
        else:
            shifted = jnp.where(row >= k, pltpu.roll(x, k, axis=0), ident)
        x = op(x, shifted)
        k *= 2
    return x


def _conv_rows(xs_ref, n, w_ref, b_ref):
    acc = b_ref[...] + w_ref[0:1, :] * xs_ref[SUBLANES - 1:SUBLANES - 1 + n, :]
    for j in range(1, CONV_W):
        acc = acc + w_ref[j:j + 1, :] * xs_ref[SUBLANES - 1 + j:SUBLANES - 1 + j + n, :]
    return acc


def _silu(x):
    return x * jax.nn.sigmoid(x)


def _mlstm_dir(d, qk, v, gates, bg_ref, ct_ref, m_ref, write_h):
    n = qk.shape[0]
    rev = d == 1
    gi = gates[:, :LANES] + bg_ref[0:1, :]
    lf = jax.nn.log_sigmoid(gates[:, LANES:] + bg_ref[1:2, :])
    bc = _scan_rows(lf, jnp.add, 0.0, rev)
    a = gi - bc
    m_prev = m_ref[d, 0:1, :]
    mm = jnp.maximum(_scan_rows(a, jnp.maximum, -jnp.inf, rev), m_prev)
    inter = jnp.exp(m_prev - mm)
    em = jnp.exp(-(bc + mm))
    last = 0 if rev else n - 1
    mm_last = mm[last:last + 1, :]
    m_new = bc[last:last + 1, :] + mm_last
    decay = jnp.exp(m_prev - mm_last)
    wcol = jnp.exp(a - mm_last)
    a_t = a.T
    row = lax.broadcasted_iota(jnp.int32, (n, n), 0)
    col = lax.broadcasted_iota(jnp.int32, (n, n), 1)
    mask = (row <= col) if rev else (row >= col)
    ones = jnp.ones((n, LANES), BF16)
    for h in range(M_HEADS):
        e = d * M_HEADS + h
        q = qk[:, h * M_DQK:(h + 1) * M_DQK].astype(BF16)
        kf = qk[:, M_QK + h * M_DQK:M_QK + (h + 1) * M_DQK] * (M_DQK ** -0.5)
        vaug = jnp.concatenate([v[:, h * M_DV:(h + 1) * M_DV].astype(BF16), ones], axis=1)
        dmat = jnp.where(mask, jnp.exp(a_t[e:e + 1, :] - mm[:, e:e + 1]), 0.0)
        s = lax.dot_general(q, kf.astype(BF16), (((1,), (1,)), ((), ())), preferred_element_type=F32)
        sw = (s * dmat).astype(BF16)
        ct = ct_ref[e]
        num = inter[:, e:e + 1] * jnp.dot(q, ct.astype(BF16), preferred_element_type=F32)
        num = num + jnp.dot(sw, vaug, preferred_element_type=F32)
        den = jnp.maximum(jnp.abs(num[:, M_DV:]), em[:, e:e + 1])
        write_h(h, num[:, :M_DV] / jnp.concatenate([den, den], axis=1))
        kw = (kf * wcol[:, e:e + 1]).astype(BF16)
        upd = lax.dot_general(kw, vaug, (((0,), (0,)), ((), ())), preferred_element_type=F32)
        ct_ref[e] = decay[:, e:e + 1] * ct + upd
    m_ref[d, 0:1, :] = m_new


def _mlstm_kernel(cqk_ref, cv_ref, cg_ref,
                  qf_ref, pf_ref, nf_ref, vf_ref, gf_ref,
                  qb_ref, pb_ref, nb_ref, vb_ref, gb_ref,
                  cw_ref, cb_ref, bg_ref,
                  hcf_ref, hcb_ref, hf_ref, hb_ref,
                  xs_ref, ct_ref, m_ref, *, n_lat):
    s = pl.program_id(1)
    zero_rows = jnp.zeros((SUBLANES, 2 * M_QK), F32)

    @pl.when(s == 0)
    def _():
        ct_ref[...] = jnp.zeros_like(ct_ref)
        m_ref[...] = jnp.zeros_like(m_ref)
        xs_ref[0:SUBLANES, :] = zero_rows
        xs_ref[SUBLANES + CHUNK:2 * SUBLANES + CHUNK, :] = zero_rows
        xs_ref[SUBLANES:SUBLANES + CHUNK, :] = cqk_ref[...]
        qk = _silu(_conv_rows(xs_ref, CHUNK, cw_ref, cb_ref))
        v = cv_ref[...]
        g = cg_ref[...]
        for d, out_ref in ((0, hcf_ref), (1, hcb_ref)):
            def write_h(h, val, out_ref=out_ref):
                out_ref[:, h * M_DV:(h + 1) * M_DV] = val
            _mlstm_dir(d, qk, v, g, bg_ref, ct_ref, m_ref, write_h)

    @pl.when(s > 0)
    def _():
        for d, q_ref, p_ref, n_ref, v_ref, g_ref, out_ref in (
                (0, qf_ref, pf_ref, nf_ref, vf_ref, gf_ref, hf_ref),
                (1, qb_ref, pb_ref, nb_ref, vb_ref, gb_ref, hb_ref)):
            j = (s - 1) if d == 0 else (n_lat - s)
            has_prev = (j > 0).astype(F32)
            has_next = (j < n_lat - 1).astype(F32)
            xs_ref[0:SUBLANES, :] = p_ref[...] * has_prev
            xs_ref[SUBLANES + CHUNK:2 * SUBLANES + CHUNK, :] = n_ref[...] * has_next
            for c in range(CHUNK_COLS):
                xs_ref[SUBLANES + c * GRID_W:SUBLANES + (c + 1) * GRID_W, :] = q_ref[:, c * SEC:(c + 1) * SEC]
            qk = _silu(_conv_rows(xs_ref, CHUNK, cw_ref, cb_ref))
            v = jnp.concatenate([v_ref[:, c * SEC:(c + 1) * SEC] for c in range(CHUNK_COLS)], axis=0)
            g = jnp.concatenate([g_ref[:, c * GATE_W:(c + 1) * GATE_W] for c in range(CHUNK_COLS)], axis=0)

            def write_h(h, val, out_ref=out_ref):
                for c in range(CHUNK_COLS):
                    out_ref[:, c * SEC + h * M_DV:c * SEC + (h + 1) * M_DV] = val[c * GRID_W:(c + 1) * GRID_W, :]
            _mlstm_dir(d, qk, v, g, bg_ref, ct_ref, m_ref, write_h)


def _mlstm(pc, gc, px, gx, conv_w, conv_b, bg2):
    _, bsz, t, _ = px.shape
    rows = t // GRID_W
    n_lat = t // CHUNK
    pxv = px.reshape(N_SEC, bsz, rows, GRID_W * SEC)
    gxv = gx.reshape(bsz, rows, GRID_W * GATE_W)
    last_col = GRID_W - 1

    def jf(s):
        return jnp.maximum(s - 1, 0)

    def jb(s):
        return jnp.minimum(n_lat - s, n_lat - 1)

    def lat_specs(jfun):
        return [
            pl.BlockSpec((None, None, rows, CHUNK * SEC // GRID_W), lambda b, s: (SEC_QK, b, 0, jfun(s))),
            pl.BlockSpec((None, None, SUBLANES, SEC),
                         lambda b, s: (SEC_QK, b, rows // SUBLANES - 1, jnp.maximum(jfun(s) * CHUNK_COLS - 1, 0))),
            pl.BlockSpec((None, None, SUBLANES, SEC),
                         lambda b, s: (SEC_QK, b, 0, jnp.minimum(jfun(s) * CHUNK_COLS + CHUNK_COLS, last_col))),
            pl.BlockSpec((None, None, rows, CHUNK_COLS * SEC), lambda b, s: (SEC_V, b, 0, jfun(s))),
            pl.BlockSpec((None, rows, CHUNK_COLS * GATE_W), lambda b, s: (b, 0, jfun(s))),
        ]

    in_specs = [
        pl.BlockSpec((None, None, CTX_LEN, SEC), lambda b, s: (SEC_QK, b, 0, 0)),
        pl.BlockSpec((None, None, CTX_LEN, SEC), lambda b, s: (SEC_V, b, 0, 0)),
        pl.BlockSpec((None, CTX_LEN, GATE_W), lambda b, s: (b, 0, 0)),
    ] + lat_specs(jf) + lat_specs(jb) + [
        pl.BlockSpec((CONV_W, SEC), lambda b, s: (0, 0)),
        pl.BlockSpec((1, SEC), lambda b, s: (0, 0)),
        pl.BlockSpec((SUBLANES, LANES), lambda b, s: (0, 0)),
    ]
    out_specs = [
        pl.BlockSpec((None, CTX_LEN, M_WIDTH), lambda b, s: (b, 0, 0)),
        pl.BlockSpec((None, CTX_LEN, M_WIDTH), lambda b, s: (b, 0, 0)),
        pl.BlockSpec((None, rows, CHUNK_COLS * M_WIDTH), lambda b, s: (b, 0, jf(s))),
        pl.BlockSpec((None, rows, CHUNK_COLS * M_WIDTH), lambda b, s: (b, 0, jb(s))),
    ]
    out_shape = [
        jax.ShapeDtypeStruct((bsz, CTX_LEN, M_WIDTH), F32),
        jax.ShapeDtypeStruct((bsz, CTX_LEN, M_WIDTH), F32),
        jax.ShapeDtypeStruct((bsz, rows, GRID_W * M_WIDTH), F32),
        jax.ShapeDtypeStruct((bsz, rows, GRID_W * M_WIDTH), F32),
    ]
    hcf, hcb, hf, hb = pl.pallas_call(
        functools.partial(_mlstm_kernel, n_lat=n_lat),
        grid=(bsz, n_lat + 1),
        in_specs=in_specs,
        out_specs=out_specs,
        out_shape=out_shape,
        scratch_shapes=[
            pltpu.VMEM((CHUNK + 2 * SUBLANES, 2 * M_QK), F32),
            pltpu.VMEM((2 * M_HEADS, M_DQK, M_DV + LANES), F32),
            pltpu.VMEM((2, SUBLANES, LANES), F32),
        ],
        compiler_params=_cparams(("arbitrary", "arbitrary")),
        name="mlstm",
    )(pc, pc, gc, pxv, pxv, pxv, pxv, gxv, pxv, pxv, pxv, pxv, gxv, conv_w, conv_b, bg2)
    return hcf, hcb, hf.reshape(bsz, t, M_WIDTH), hb.reshape(bsz, t, M_WIDTH)


def _lru_gates(d, xc, wd_ref, br_ref, sp_ref, a_ref, u_ref):
    xb = xc.astype(BF16)
    for j in range(R_WIDTH // LRU_TILE):
        sl = slice(j * LRU_TILE, (j + 1) * LRU_TILE)
        xj = xb[:, sl]
        r = jax.nn.sigmoid(jnp.dot(xj, wd_ref[d, 0, j], preferred_element_type=F32) + br_ref[d, 0:1, sl])
        ii = jax.nn.sigmoid(jnp.dot(xj, wd_ref[d, 1, j], preferred_element_type=F32) + br_ref[d, 1:2, sl])
        log_a = (-LRU_C) * r * sp_ref[d:d + 1, sl]
        a = jnp.exp(log_a)
        a_ref[:, sl] = a
        u_ref[:, sl] = jnp.sqrt(-jnp.tanh(log_a) * (a * a + 1.0)) * (ii * xc[:, sl])


def _lru_scan(d, a_ref, u_ref, h_ref, out_ref, n):
    rev = d == 1
    groups = n // SUBLANES
    row = lax.broadcasted_iota(jnp.int32, (SUBLANES, R_WIDTH), 0)

    def body(g, carry):
        gg = (groups - 1 - g) if rev else g
        r0 = pl.multiple_of(gg * SUBLANES, SUBLANES)
        a = a_ref[pl.ds(r0, SUBLANES), :]
        u = u_ref[pl.ds(r0, SUBLANES), :]
        k = 1
        while k < SUBLANES:
            if rev:
                ok = row < SUBLANES - k
                a_s = pltpu.roll(a, SUBLANES - k, axis=0)
                u_s = pltpu.roll(u, SUBLANES - k, axis=0)
            else:
                ok = row >= k
                a_s = pltpu.roll(a, k, axis=0)
                u_s = pltpu.roll(u, k, axis=0)
            u = u + a * jnp.where(ok, u_s, 0.0)
            a = a * jnp.where(ok, a_s, 1.0)
            k *= 2
        hh = u + a * carry
        out_ref[pl.ds(r0, SUBLANES), :] = hh
        return hh[0:1, :] if rev else hh[SUBLANES - 1:SUBLANES, :]

    h_ref[d, 0:1, :] = lax.fori_loop(0, groups, body, h_ref[d, 0:1, :])


def _rglru_kernel(cx_ref, xf_ref, pf_ref, nf_ref, xb_ref, pb_ref, nb_ref,
                  cw_ref, cb_ref, wd_ref, br_ref, lam_ref,
                  ycf_ref, ycb_ref, yf_ref, yb_ref,
                  xs_ref, a_ref, u_ref, h_ref, sp_ref, *, n_lat):
    s = pl.program_id(1)
    zero_rows = jnp.zeros((SUBLANES, R_WIDTH), F32)

    @pl.when(s == 0)
    def _():
        h_ref[...] = jnp.zeros_like(h_ref)
        sp_ref[...] = jax.nn.softplus(-lam_ref[...])
        xs_ref[0:SUBLANES, :] = zero_rows
        xs_ref[SUBLANES + LRU_BLK:2 * SUBLANES + LRU_BLK, :] = zero_rows
        xs_ref[SUBLANES:SUBLANES + LRU_BLK, :] = cx_ref[...]
        xc = _conv_rows(xs_ref, LRU_BLK, cw_ref, cb_ref)
        for d, out_ref in ((0, ycf_ref), (1, ycb_ref)):
            _lru_gates(d, xc, wd_ref, br_ref, sp_ref, a_ref, u_ref)
            _lru_scan(d, a_ref, u_ref, h_ref, out_ref, LRU_BLK)

    @pl.when(s > 0)
    def _():
        for d, x_ref, p_ref, n_ref, out_ref in ((0, xf_ref, pf_ref, nf_ref, yf_ref),
                                                (1, xb_ref, pb_ref, nb_ref, yb_ref)):
            j = (s - 1) if d == 0 else (n_lat - s)
            has_prev = (j > 0).astype(F32)
            has_next = (j < n_lat - 1).astype(F32)
            xs_ref[0:SUBLANES, :] = p_ref[...] * has_prev
            xs_ref[SUBLANES + LRU_BLK:2 * SUBLANES + LRU_BLK, :] = n_ref[...] * has_next
            xs_ref[SUBLANES:SUBLANES + LRU_BLK, :] = x_ref[...]
            xc = _conv_rows(xs_ref, LRU_BLK, cw_ref, cb_ref)
            _lru_gates(d, xc, wd_ref, br_ref, sp_ref, a_ref, u_ref)
            _lru_scan(d, a_ref, u_ref, h_ref, out_ref, LRU_BLK)


def _rglru(pc, px, conv_w, conv_b, wd, b_rg, lam):
    _, bsz, t, _ = px.shape
    n_lat = t // LRU_BLK
    per_blk = LRU_BLK // SUBLANES
    n_rows8 = t // SUBLANES

    def jf(s):
        return jnp.maximum(s - 1, 0)

    def jb(s):
        return jnp.minimum(n_lat - s, n_lat - 1)

    def lat_specs(jfun):
        return [
            pl.BlockSpec((None, None, LRU_BLK, SEC), lambda b, s: (SEC_XL, b, jfun(s), 0)),
            pl.BlockSpec((None, None, SUBLANES, SEC),
                         lambda b, s: (SEC_XL, b, jnp.maximum(jfun(s) * per_blk - 1, 0), 0)),
            pl.BlockSpec((None, None, SUBLANES, SEC),
                         lambda b, s: (SEC_XL, b, jnp.minimum((jfun(s) + 1) * per_blk, n_rows8 - 1), 0)),
        ]

    in_specs = [pl.BlockSpec((None, None, CTX_LEN, SEC), lambda b, s: (SEC_XL, b, 0, 0))]
    in_specs += lat_specs(jf) + lat_specs(jb) + [
        pl.BlockSpec((CONV_W, SEC), lambda b, s: (0, 0)),
        pl.BlockSpec((1, SEC), lambda b, s: (0, 0)),
        pl.BlockSpec(wd.shape, lambda b, s: (0, 0, 0, 0, 0)),
        pl.BlockSpec(b_rg.shape, lambda b, s: (0, 0, 0)),
        pl.BlockSpec(lam.shape, lambda b, s: (0, 0)),
    ]
    out_specs = [
        pl.BlockSpec((None, CTX_LEN, R_WIDTH), lambda b, s: (b, 0, 0)),
        pl.BlockSpec((None, CTX_LEN, R_WIDTH), lambda b, s: (b, 0, 0)),
        pl.BlockSpec((None, LRU_BLK, R_WIDTH), lambda b, s: (b, jf(s), 0)),
        pl.BlockSpec((None, LRU_BLK, R_WIDTH), lambda b, s: (b, jb(s), 0)),
    ]
    out_shape = [
        jax.ShapeDtypeStruct((bsz, CTX_LEN, R_WIDTH), F32),
        jax.ShapeDtypeStruct((bsz, CTX_LEN, R_WIDTH), F32),
        jax.ShapeDtypeStruct((bsz, t, R_WIDTH), F32),
        jax.ShapeDtypeStruct((bsz, t, R_WIDTH), F32),
    ]
    return pl.pallas_call(
        functools.partial(_rglru_kernel, n_lat=n_lat),
        grid=(bsz, n_lat + 1),
        in_specs=in_specs,
        out_specs=out_specs,
        out_shape=out_shape,
        scratch_shapes=[
            pltpu.VMEM((LRU_BLK + 2 * SUBLANES, R_WIDTH), F32),
            pltpu.VMEM((LRU_BLK, R_WIDTH), F32),
            pltpu.VMEM((LRU_BLK, R_WIDTH), F32),
            pltpu.VMEM((2, SUBLANES, R_WIDTH), F32),
            pltpu.VMEM((2, R_WIDTH), F32),
        ],
        compiler_params=_cparams(("arbitrary", "arbitrary")),
        name="rglru",
    )(pc, px, px, px, px, px, px, conv_w, conv_b, wd, b_rg, lam)


def _out_proj_kernel(hf_ref, hb_ref, yf_ref, yb_ref, o_ref, zm_ref, zl_ref, x_ref, gt_ref, mg_ref, w_ref, fg_ref,
                     out_ref, *, final):
    hm = hf_ref[...] + hb_ref[...]
    parts = []
    for h in range(M_HEADS):
        hh = hm[:, h * M_DV:(h + 1) * M_DV]
        parts.append(hh * lax.rsqrt(jnp.mean(hh * hh, axis=-1, keepdims=True) + EPS))
    hn = jnp.concatenate(parts, axis=1) * mg_ref[...]
    ym = hn * jax.nn.sigmoid(o_ref[...]) * _silu(zm_ref[...])
    yr = (yf_ref[...] + yb_ref[...]) * _silu(zl_ref[...])
    y = jnp.concatenate([ym, yr], axis=1).astype(BF16)
    xn = x_ref[...] + gt_ref[...] * jnp.dot(y, w_ref[...], preferred_element_type=F32)
    if final:
        xn = xn * lax.rsqrt(jnp.mean(xn * xn, axis=-1, keepdims=True) + EPS) * fg_ref[...]
    out_ref[...] = xn


def _out_proj(hf, hb, yf, yb, p, x2d, mod_l, row_of_tile, m_norm_g, w_out, final_g, tm, final):
    m, d = x2d.shape

    def tok(width):
        return pl.BlockSpec((tm, width), lambda i: (i, 0))

    def sec(k):
        return pl.BlockSpec((None, tm, SEC), lambda i: (k, i, 0))

    return pl.pallas_call(
        functools.partial(_out_proj_kernel, final=final),
        grid=(m // tm,),
        in_specs=[
            tok(M_WIDTH), tok(M_WIDTH), tok(R_WIDTH), tok(R_WIDTH),
            sec(SEC_O), sec(SEC_ZM), sec(SEC_ZL),
            tok(d),
            pl.BlockSpec((None, None, 1, d), lambda i: (row_of_tile(i), 2, 0, 0)),
            pl.BlockSpec((1, M_WIDTH), lambda i: (0, 0)),
            pl.BlockSpec((d, d), lambda i: (0, 0)),
            pl.BlockSpec((1, d), lambda i: (0, 0)),
        ],
        out_specs=tok(d),
        out_shape=jax.ShapeDtypeStruct((m, d), F32),
        compiler_params=_cparams(("arbitrary",)),
        name="out_proj",
    )(hf, hb, yf, yb, p, p, p, x2d, mod_l, m_norm_g, w_out, final_g)


def _dense_gate_tiles(w_rg):
    depth = w_rg.shape[0]
    per = LRU_TILE // R_BLOCK
    w = w_rg.reshape(depth, 2, 2, R_WIDTH // LRU_TILE, per, R_BLOCK, R_BLOCK)
    eye = jnp.eye(per, dtype=w.dtype)
    dense = jnp.einsum('ldgtpij,pq->ldgtpiqj', w, eye)
    return dense.reshape(depth, 2, 2, R_WIDTH // LRU_TILE, LRU_TILE, LRU_TILE).astype(BF16)


def kernel(x, c, ctx, c_ctx, w_mod, b_mod, norm_g, w_in, b_gate, conv_qk_w, conv_qk_b, m_norm_g, conv_r_w, conv_r_b,
           w_rg, b_rg, lru_lambda, w_out, final_g):
    bsz, t, d = x.shape
    depth = w_mod.shape[0]
    nh = M_HEADS

    o_q, o_k, o_v, o_o, o_zm, o_g, o_xl = 0, M_QK, 2 * M_QK, 2 * M_QK + M_WIDTH, 2 * M_QK + 2 * M_WIDTH, \
        2 * M_QK + 3 * M_WIDTH, 2 * M_QK + 3 * M_WIDTH + 4 * nh
    w_main = jnp.concatenate([w_in[:, :, o_q:o_g], w_in[:, :, o_xl:]], axis=-1).astype(BF16)
    wg = w_in[:, :, o_g:o_xl]
    zpad = jnp.zeros((depth, d, LANES - 2 * nh), w_in.dtype)
    w_gate = jnp.concatenate([wg[:, :, 0:nh], wg[:, :, 2 * nh:3 * nh], zpad,
                              wg[:, :, nh:2 * nh], wg[:, :, 3 * nh:4 * nh], zpad], axis=-1).astype(BF16)
    bpad = jnp.zeros((depth, LANES - 2 * nh), F32)
    bg_i = jnp.concatenate([b_gate[:, 0:nh], b_gate[:, 2 * nh:3 * nh], bpad], axis=-1)
    bg_f = jnp.concatenate([b_gate[:, nh:2 * nh], b_gate[:, 3 * nh:4 * nh], bpad], axis=-1)
    bg2 = jnp.concatenate([bg_i[:, None, :], bg_f[:, None, :], jnp.zeros((depth, SUBLANES - 2, LANES), F32)], axis=1)
    w_out_b = w_out.astype(BF16)
    wd = _dense_gate_tiles(w_rg)

    cvec = jnp.concatenate([c, c_ctx[None, :], jnp.zeros((SUBLANES - bsz - 1, d), F32)], axis=0)
    mods = _modulation(cvec, w_mod, b_mod).reshape(depth, SUBLANES, 3, 1, d)

    tm_in = 512
    tm_out = 256
    x2d = x.reshape(bsz * t, d)
    c2d = ctx.reshape(bsz * CTX_LEN, d)
    lat_row_in = lambda i: i // (t // tm_in)
    lat_row_out = lambda i: i // (t // tm_out)
    ctx_row = lambda i: bsz

    for l in range(depth):
        last = l == depth - 1
        mod_l = mods[l]
        ng = norm_g[l][None, :]
        px, gx = _in_proj(x2d, mod_l, lat_row_in, ng, w_main[l], w_gate[l], tm_in)
        pc, gc = _in_proj(c2d, mod_l, ctx_row, ng, w_main[l], w_gate[l], tm_in)
        px4 = px.reshape(N_SEC, bsz, t, SEC)
        pc4 = pc.reshape(N_SEC, bsz, CTX_LEN, SEC)
        hcf, hcb, hf, hb = _mlstm(pc4, gc.reshape(bsz, CTX_LEN, GATE_W), px4, gx.reshape(bsz, t, GATE_W),
                                  conv_qk_w[l], conv_qk_b[l][None, :], bg2[l])
        ycf, ycb, yf, yb = _rglru(pc4, px4, conv_r_w[l], conv_r_b[l][None, :], wd[l],
                                  b_rg[l].reshape(2, 2, R_WIDTH), lru_lambda[l])
        mg = m_norm_g[l][None, :]
        fg = final_g[None, :]
        x2d = _out_proj(hf.reshape(bsz * t, M_WIDTH), hb.reshape(bsz * t, M_WIDTH),
                        yf.reshape(bsz * t, R_WIDTH), yb.reshape(bsz * t, R_WIDTH),
                        px, x2d, mod_l, lat_row_out, mg, w_out_b[l], fg, tm_out, last)
        if not last:
            c2d = _out_proj(hcf.reshape(bsz * CTX_LEN, M_WIDTH), hcb.reshape(bsz * CTX_LEN, M_WIDTH),
                            ycf.reshape(bsz * CTX_LEN, R_WIDTH), ycb.reshape(bsz * CTX_LEN, R_WIDTH),
                            pc, c2d, mod_l, ctx_row, mg, w_out_b[l], fg, tm_out, False)
    return x2d.reshape(bsz, t, d)
```

```python
import functools

import jax
import jax.numpy as jnp
from jax import lax
from jax.experimental import pallas as pl
from jax.experimental.pallas import tpu as pltpu

D_MODEL = 2048
DEPTH = 4
CTX_LEN = 256
GRID_W = 64
M_WIDTH = 1024
R_WIDTH = 1024
M_HEADS = 4
M_DV = 256
M_DQK = 128
M_QK = 512
R_BLOCKS = 16
R_BLOCK = 64
CONV_W = 4
LRU_C = 8.0
EPS = 1e-6

LANES = 128
SUBLANES = 8
BF16_ROWS = 16
SEC = 1024
N_SEC = 6
N_SEC32 = 2
S32_QK, S32_V = 0, 1
S16_O, S16_ZM, S16_XL, S16_ZL = 0, 1, 2, 3
WA_SECS = 4
GATE_W = 2 * LANES
CTX_CHUNK = CTX_LEN
COL_GROUP = SUBLANES
LAT_CHUNK = COL_GROUP * GRID_W
LRU_BLK = 256
LRU_TILE = 256
VMEM_LIMIT = 56 * 1024 * 1024

F32 = jnp.float32
BF16 = jnp.bfloat16


def _cparams(sem):
    return pltpu.CompilerParams(dimension_semantics=sem, vmem_limit_bytes=VMEM_LIMIT)


def _silu(x):
    return x * jax.nn.sigmoid(x)


def _sigmoid_t(x):
    return 0.5 * jnp.tanh(0.5 * x) + 0.5


def _mod_kernel(c_ref, w_ref, b_ref, o_ref):
    c = c_ref[...]
    s = (c * jax.nn.sigmoid(c)).astype(BF16)
    o_ref[...] = jnp.dot(s, w_ref[...].astype(BF16), preferred_element_type=F32) + b_ref[...]


def _modulation(cvec, w_mod, b_mod):
    depth, d, n = w_mod.shape
    tn = 1024
    return pl.pallas_call(
        _mod_kernel,
        grid=(depth, n // tn),
        in_specs=[
            pl.BlockSpec((SUBLANES, d), lambda l, j: (0, 0)),
            pl.BlockSpec((None, d, tn), lambda l, j: (l, 0, j)),
            pl.BlockSpec((None, 1, tn), lambda l, j: (l, 0, j)),
        ],
        out_specs=pl.BlockSpec((None, SUBLANES, tn), lambda l, j: (l, 0, j)),
        out_shape=jax.ShapeDtypeStruct((depth, SUBLANES, n), F32),
        compiler_params=_cparams(("arbitrary", "arbitrary")),
        name="mod",
    )(cvec, w_mod, b_mod.reshape(depth, 1, n))


def _in_proj_kernel(x_ref, sh_ref, sc_ref, g_ref, wa_ref, wb_ref, wg_ref, p32_ref, p16_ref, gate_ref, h_scr):
    n = pl.program_id(1)

    @pl.when(n == 0)
    def _():
        x = x_ref[...]
        ms = jnp.mean(x * x, axis=-1, keepdims=True)
        y = x * lax.rsqrt(ms + EPS) * g_ref[...]
        h = (y * (1.0 + sc_ref[...]) + sh_ref[...]).astype(BF16)
        h_scr[...] = h
        gate_ref[...] = jnp.dot(h, wg_ref[...], preferred_element_type=F32)

    @pl.when(n < N_SEC32)
    def _():
        p32_ref[...] = jnp.dot(h_scr[...], wa_ref[...], preferred_element_type=F32)

    @pl.when(jnp.logical_and(n >= N_SEC32, n < WA_SECS))
    def _():
        p16_ref[...] = jnp.dot(h_scr[...], wa_ref[...], preferred_element_type=F32).astype(BF16)

    @pl.when(n >= WA_SECS)
    def _():
        p16_ref[...] = jnp.dot(h_scr[...], wb_ref[...], preferred_element_type=F32).astype(BF16)


def _in_proj(x2d, mod_l, row_of_tile, norm_g, w_a, w_b, w_gate, layer, tm):
    m, d = x2d.shape
    return pl.pallas_call(
        _in_proj_kernel,
        grid=(m // tm, N_SEC),
        in_specs=[
            pl.BlockSpec((tm, d), lambda i, n: (i, 0)),
            pl.BlockSpec((None, None, 1, d), lambda i, n: (row_of_tile(i), 0, 0, 0)),
            pl.BlockSpec((None, None, 1, d), lambda i, n: (row_of_tile(i), 1, 0, 0)),
            pl.BlockSpec((None, 1, d), lambda i, n: (layer, 0, 0)),
            pl.BlockSpec((None, d, SEC), lambda i, n: (layer, 0, jnp.minimum(n, WA_SECS - 1))),
            pl.BlockSpec((None, d, SEC), lambda i, n: (layer, 0, jnp.maximum(n - WA_SECS, 0))),
            pl.BlockSpec((None, d, GATE_W), lambda i, n: (layer, 0, 0)),
        ],
        out_specs=[
            pl.BlockSpec((None, tm, SEC), lambda i, n: (jnp.minimum(n, N_SEC32 - 1), i, 0)),
            pl.BlockSpec((None, tm, SEC), lambda i, n: (jnp.maximum(n - N_SEC32, 0), i, 0)),
            pl.BlockSpec((tm, GATE_W), lambda i, n: (i, 0)),
        ],
        out_shape=[
            jax.ShapeDtypeStruct((N_SEC32, m, SEC), F32),
            jax.ShapeDtypeStruct((N_SEC - N_SEC32, m, SEC), BF16),
            jax.ShapeDtypeStruct((m, GATE_W), F32),
        ],
        scratch_shapes=[pltpu.VMEM((tm, d), BF16)],
        compiler_params=_cparams(("arbitrary", "arbitrary")),
        name="in_proj",
    )(x2d, mod_l, mod_l, norm_g, w_a, w_b, w_gate)


def _shifted(x, k, reverse, ident):
    n = x.shape[0]
    row = lax.broadcasted_iota(jnp.int32, x.shape, 0)
    if reverse:
        return jnp.where(row < n - k, pltpu.roll(x, n - k, axis=0), ident)
    return jnp.where(row >= k, pltpu.roll(x, k, axis=0), ident)


def _scan_rows(x, op, ident, reverse):
    n = x.shape[0]
    k = 1
    while k < n:
        x = op(x, _shifted(x, k, reverse, ident))
        k *= 2
    return x


def _scan_colmajor(x, op, ident, reverse):
    n = x.shape[0]
    k = SUBLANES
    while k < n:
        x = op(x, _shifted(x, k, reverse, ident))
        k *= 2
    tot = x[0:SUBLANES, :] if reverse else x[n - SUBLANES:n, :]
    tot = _scan_rows(tot, op, ident, reverse)
    tot = _shifted(tot, 1, reverse, ident)
    return op(x, jnp.concatenate([tot] * (n // SUBLANES), axis=0))


def _conv_rows(xs_ref, n, w_ref, b_ref, step):
    base = SUBLANES - step
    acc = b_ref[...] + w_ref[0:1, :] * xs_ref[base:base + n, :]
    for j in range(1, CONV_W):
        acc = acc + w_ref[j:j + 1, :] * xs_ref[base + j * step:base + j * step + n, :]
    return acc


def _mlstm_dir(d, qk, v, gates, bg_ref, ct_ref, m_ref, scan, posdiff, write_h):
    n = qk.shape[0]
    rev = d == 1
    gi = gates[:, :LANES] + bg_ref[0:1, :]
    lf = jax.nn.log_sigmoid(gates[:, LANES:] + bg_ref[1:2, :])
    bc = scan(lf, jnp.add, 0.0, rev)
    a = gi - bc
    m_prev = m_ref[d, 0:1, :]
    mm = jnp.maximum(scan(a, jnp.maximum, -jnp.inf, rev), m_prev)
    inter = jnp.exp(m_prev - mm)
    em = jnp.exp(-(bc + mm))
    last = 0 if rev else n - 1
    mm_last = mm[last:last + 1, :]
    m_new = bc[last:last + 1, :] + mm_last
    decay = jnp.exp(m_prev - mm_last)
    wcol = jnp.exp(a - mm_last)
    a_t = a.T
    mask = (posdiff <= 0) if rev else (posdiff >= 0)
    ones = jnp.ones((n, LANES), BF16)
    for h in range(M_HEADS):
        e = d * M_HEADS + h
        q = qk[:, h * M_DQK:(h + 1) * M_DQK].astype(BF16)
        kf = qk[:, M_QK + h * M_DQK:M_QK + (h + 1) * M_DQK] * (M_DQK ** -0.5)
        vaug = jnp.concatenate([v[:, h * M_DV:(h + 1) * M_DV].astype(BF16), ones], axis=1)
        dmat = jnp.where(mask, jnp.exp(a_t[e:e + 1, :] - mm[:, e:e + 1]), 0.0)
        s = lax.dot_general(q, kf.astype(BF16), (((1,), (1,)), ((), ())), preferred_element_type=F32)
        sw = (s * dmat).astype(BF16)
        ct = ct_ref[e]
        num = inter[:, e:e + 1] * jnp.dot(q, ct.astype(BF16), preferred_element_type=F32)
        num = num + jnp.dot(sw, vaug, preferred_element_type=F32)
        den = jnp.maximum(jnp.abs(num[:, M_DV:]), em[:, e:e + 1])
        write_h(h, num[:, :M_DV] / jnp.concatenate([den, den], axis=1))
        kw = (kf * wcol[:, e:e + 1]).astype(BF16)
        upd = lax.dot_general(kw, vaug, (((0,), (0,)), ((), ())), preferred_element_type=F32)
        ct_ref[e] = decay[:, e:e + 1] * ct + upd
    m_ref[d, 0:1, :] = m_new


def _mlstm_kernel(cqk_ref, cv_ref, cg_ref,
                  qf_ref, pf_ref, nf_ref, vf_ref, gf_ref,
                  qb_ref, pb_ref, nb_ref, vb_ref, gb_ref,
                  cw_ref, cb_ref, bg_ref,
                  hcf_ref, hcb_ref, hf_ref, hb_ref,
                  xs_ref, ct_ref, m_ref, pd_ref, *, n_lat):
    s = pl.program_id(1)
    lc = LAT_CHUNK
    zero_rows = jnp.zeros((SUBLANES, 2 * M_QK), F32)

    @pl.when(s == 0)
    def _():
        ct_ref[...] = jnp.zeros_like(ct_ref)
        m_ref[...] = jnp.zeros_like(m_ref)
        row = lax.broadcasted_iota(jnp.int32, (lc, lc), 0)
        col = lax.broadcasted_iota(jnp.int32, (lc, lc), 1)
        pos_r = (row % COL_GROUP) * GRID_W + row // COL_GROUP
        pos_c = (col % COL_GROUP) * GRID_W + col // COL_GROUP
        pd_ref[...] = pos_r - pos_c
        xs_ref[0:SUBLANES, :] = zero_rows
        xs_ref[SUBLANES + CTX_CHUNK:2 * SUBLANES + CTX_CHUNK, :] = zero_rows
        xs_ref[SUBLANES:SUBLANES + CTX_CHUNK, :] = cqk_ref[...]
        qk = _silu(_conv_rows(xs_ref, CTX_CHUNK, cw_ref, cb_ref, 1))
        v = cv_ref[...]
        g = cg_ref[...]
        crow = lax.broadcasted_iota(jnp.int32, (CTX_CHUNK, CTX_CHUNK), 0)
        ccol = lax.broadcasted_iota(jnp.int32, (CTX_CHUNK, CTX_CHUNK), 1)
        for d, out_ref in ((0, hcf_ref), (1, hcb_ref)):
            def write_h(h, val, out_ref=out_ref):
                out_ref[:, h * M_DV:(h + 1) * M_DV] = val
            _mlstm_dir(d, qk, v, g, bg_ref, ct_ref, m_ref, _scan_rows, crow - ccol, write_h)

    @pl.when(s > 0)
    def _():
        sub = lax.broadcasted_iota(jnp.int32, (SUBLANES, 2 * M_QK), 0)
        for d, q_ref, p_ref, n_ref, v_ref, g_ref, out_ref in (
                (0, qf_ref, pf_ref, nf_ref, vf_ref, gf_ref, hf_ref),
                (1, qb_ref, pb_ref, nb_ref, vb_ref, gb_ref, hb_ref)):
            j = (s - 1) if d == 0 else (n_lat - s)
            has_prev = (j > 0).astype(F32)
            has_next = (j < n_lat - 1).astype(F32)
            x = q_ref[...].reshape(lc, 2 * M_QK)
            xs_ref[SUBLANES:SUBLANES + lc, :] = x
            x_last = x[lc - SUBLANES:lc, :]
            xs_ref[0:SUBLANES, :] = jnp.where(sub == 0, pltpu.roll(p_ref[...], 1, axis=0) * has_prev,
                                              pltpu.roll(x_last, 1, axis=0))
            for k in range(2):
                xs_ref[SUBLANES + lc + k * SUBLANES:2 * SUBLANES + lc + k * SUBLANES, :] = jnp.where(
                    sub == SUBLANES - 1, pltpu.roll(n_ref[k], SUBLANES - 1, axis=0) * has_next,
                    pltpu.roll(x[k * SUBLANES:(k + 1) * SUBLANES, :], SUBLANES - 1, axis=0))
            qk = _silu(_conv_rows(xs_ref, lc, cw_ref, cb_ref, SUBLANES))
            v = v_ref[...].reshape(lc, M_WIDTH)
            g = g_ref[...].reshape(lc, GATE_W)

            def write_h(h, val, out_ref=out_ref):
                out_ref[:, :, h * M_DV:(h + 1) * M_DV] = val.reshape(GRID_W, COL_GROUP, M_DV)
            _mlstm_dir(d, qk, v, g, bg_ref, ct_ref, m_ref, _scan_colmajor, pd_ref[...], write_h)


def _mlstm(pc, gc, px, gx, conv_w, conv_b, bg2, layer):
    _, bsz, t, _ = px.shape
    rows = t // GRID_W
    n_lat = GRID_W // COL_GROUP
    pxv = px.reshape(N_SEC32, bsz, rows, GRID_W, SEC)
    gxv = gx.reshape(bsz, rows, GRID_W, GATE_W)

    def jf(s):
        return jnp.maximum(s - 1, 0)

    def jb(s):
        return jnp.minimum(n_lat - s, n_lat - 1)

    def lat_specs(jfun):
        return [
            pl.BlockSpec((None, None, rows, COL_GROUP, SEC), lambda b, s: (S32_QK, b, 0, jfun(s), 0)),
            pl.BlockSpec((None, None, None, COL_GROUP, SEC),
                         lambda b, s: (S32_QK, b, rows - 1, jnp.maximum(jfun(s) - 1, 0), 0)),
            pl.BlockSpec((None, None, 2, COL_GROUP, SEC),
                         lambda b, s: (S32_QK, b, 0, jnp.minimum(jfun(s) + 1, n_lat - 1), 0)),
            pl.BlockSpec((None, None, rows, COL_GROUP, SEC), lambda b, s: (S32_V, b, 0, jfun(s), 0)),
            pl.BlockSpec((None, rows, COL_GROUP, GATE_W), lambda b, s: (b, 0, jfun(s), 0)),
        ]

    in_specs = [
        pl.BlockSpec((None, None, CTX_LEN, SEC), lambda b, s: (S32_QK, b, 0, 0)),
        pl.BlockSpec((None, None, CTX_LEN, SEC), lambda b, s: (S32_V, b, 0, 0)),
        pl.BlockSpec((None, CTX_LEN, GATE_W), lambda b, s: (b, 0, 0)),
    ] + lat_specs(jf) + lat_specs(jb) + [
        pl.BlockSpec((None, CONV_W, SEC), lambda b, s: (layer, 0, 0)),
        pl.BlockSpec((None, 1, SEC), lambda b, s: (layer, 0, 0)),
        pl.BlockSpec((None, SUBLANES, LANES), lambda b, s: (layer, 0, 0)),
    ]
    out_specs = [
        pl.BlockSpec((None, CTX_LEN, M_WIDTH), lambda b, s: (b, 0, 0)),
        pl.BlockSpec((None, CTX_LEN, M_WIDTH), lambda b, s: (b, 0, 0)),
        pl.BlockSpec((None, rows, COL_GROUP, M_WIDTH), lambda b, s: (b, 0, jf(s), 0)),
        pl.BlockSpec((None, rows, COL_GROUP, M_WIDTH), lambda b, s: (b, 0, jb(s), 0)),
    ]
    out_shape = [
        jax.ShapeDtypeStruct((bsz, CTX_LEN, M_WIDTH), F32),
        jax.ShapeDtypeStruct((bsz, CTX_LEN, M_WIDTH), F32),
        jax.ShapeDtypeStruct((bsz, rows, GRID_W, M_WIDTH), F32),
        jax.ShapeDtypeStruct((bsz, rows, GRID_W, M_WIDTH), F32),
    ]
    hcf, hcb, hf, hb = pl.pallas_call(
        functools.partial(_mlstm_kernel, n_lat=n_lat),
        grid=(bsz, n_lat + 1),
        in_specs=in_specs,
        out_specs=out_specs,
        out_shape=out_shape,
        scratch_shapes=[
            pltpu.VMEM((LAT_CHUNK + 3 * SUBLANES, 2 * M_QK), F32),
            pltpu.VMEM((2 * M_HEADS, M_DQK, M_DV + LANES), F32),
            pltpu.VMEM((2, SUBLANES, LANES), F32),
            pltpu.VMEM((LAT_CHUNK, LAT_CHUNK), jnp.int32),
        ],
        compiler_params=_cparams(("arbitrary", "arbitrary")),
        name="mlstm",
    )(pc, pc, gc, pxv, pxv, pxv, pxv, gxv, pxv, pxv, pxv, pxv, gxv, conv_w, conv_b, bg2)
    return hcf, hcb, hf.reshape(bsz, t, M_WIDTH), hb.reshape(bsz, t, M_WIDTH)


def _lru_gates(d, xc, wd_ref, br_ref, sp_ref, a_ref, u_ref):
    xb = xc.astype(BF16)
    for j in range(R_WIDTH // LRU_TILE):
        sl = slice(j * LRU_TILE, (j + 1) * LRU_TILE)
        xj = xb[:, sl]
        r = _sigmoid_t(jnp.dot(xj, wd_ref[d, 0, j], preferred_element_type=F32) + br_ref[d, 0:1, sl])
        ii = _sigmoid_t(jnp.dot(xj, wd_ref[d, 1, j], preferred_element_type=F32) + br_ref[d, 1:2, sl])
        log_a = r * sp_ref[d:d + 1, sl]
        a = jnp.exp(log_a)
        a_ref[:, sl] = a
        u_ref[:, sl] = jnp.sqrt(-jnp.tanh(log_a) * (a * a + 1.0)) * (ii * xc[:, sl])


def _lru_scan(d, a_ref, u_ref, h_ref, out_ref, n):
    rev = d == 1
    groups = n // SUBLANES
    row = lax.broadcasted_iota(jnp.int32, (SUBLANES, R_WIDTH), 0)

    def body(g, carry):
        gg = (groups - 1 - g) if rev else g
        r0 = pl.multiple_of(gg * SUBLANES, SUBLANES)
        a = a_ref[pl.ds(r0, SUBLANES), :]
        u = u_ref[pl.ds(r0, SUBLANES), :]
        k = 1
        while k < SUBLANES:
            if rev:
                ok = row < SUBLANES - k
                a_s = pltpu.roll(a, SUBLANES - k, axis=0)
                u_s = pltpu.roll(u, SUBLANES - k, axis=0)
            else:
                ok = row >= k
                a_s = pltpu.roll(a, k, axis=0)
                u_s = pltpu.roll(u, k, axis=0)
            u = u + a * jnp.where(ok, u_s, 0.0)
            a = a * jnp.where(ok, a_s, 1.0)
            k *= 2
        hh = u + a * carry
        u_ref[pl.ds(r0, SUBLANES), :] = hh
        return hh[0:1, :] if rev else hh[SUBLANES - 1:SUBLANES, :]

    h_ref[d, 0:1, :] = lax.fori_loop(0, groups, body, h_ref[d, 0:1, :])
    out_ref[...] = u_ref[...].astype(out_ref.dtype)


def _rglru_kernel(cx_ref, xf_ref, pf_ref, nf_ref, xb_ref, pb_ref, nb_ref,
                  cw_ref, cb_ref, wd_ref, br_ref, lam_ref,
                  ycf_ref, ycb_ref, yf_ref, yb_ref,
                  xs_ref, a_ref, u_ref, h_ref, sp_ref, *, n_lat):
    s = pl.program_id(1)
    zero_rows = jnp.zeros((SUBLANES, R_WIDTH), F32)

    @pl.when(s == 0)
    def _():
        h_ref[...] = jnp.zeros_like(h_ref)
        sp_ref[...] = (-LRU_C) * jax.nn.softplus(-lam_ref[...])
        xs_ref[0:SUBLANES, :] = zero_rows
        xs_ref[SUBLANES + LRU_BLK:2 * SUBLANES + LRU_BLK, :] = zero_rows
        xs_ref[SUBLANES:SUBLANES + LRU_BLK, :] = cx_ref[...].astype(F32)
        xc = _conv_rows(xs_ref, LRU_BLK, cw_ref, cb_ref, 1)
        for d, out_ref in ((0, ycf_ref), (1, ycb_ref)):
            _lru_gates(d, xc, wd_ref, br_ref, sp_ref, a_ref, u_ref)
            _lru_scan(d, a_ref, u_ref, h_ref, out_ref, LRU_BLK)

    @pl.when(s > 0)
    def _():
        for d, x_ref, p_ref, n_ref, out_ref in ((0, xf_ref, pf_ref, nf_ref, yf_ref),
                                                (1, xb_ref, pb_ref, nb_ref, yb_ref)):
            j = (s - 1) if d == 0 else (n_lat - s)
            has_prev = (j > 0).astype(F32)
            has_next = (j < n_lat - 1).astype(F32)
            xs_ref[0:SUBLANES, :] = p_ref[SUBLANES:BF16_ROWS, :].astype(F32) * has_prev
            xs_ref[SUBLANES + LRU_BLK:2 * SUBLANES + LRU_BLK, :] = n_ref[0:SUBLANES, :].astype(F32) * has_next
            xs_ref[SUBLANES:SUBLANES + LRU_BLK, :] = x_ref[...].astype(F32)
            xc = _conv_rows(xs_ref, LRU_BLK, cw_ref, cb_ref, 1)
            _lru_gates(d, xc, wd_ref, br_ref, sp_ref, a_ref, u_ref)
            _lru_scan(d, a_ref, u_ref, h_ref, out_ref, LRU_BLK)


def _rglru(pc, px, conv_w, conv_b, wd, b_rg, lam, layer):
    _, bsz, t, _ = px.shape
    n_lat = t // LRU_BLK
    per_blk = LRU_BLK // BF16_ROWS
    n_halo = t // BF16_ROWS

    def jf(s):
        return jnp.maximum(s - 1, 0)

    def jb(s):
        return jnp.minimum(n_lat - s, n_lat - 1)

    def lat_specs(jfun):
        return [
            pl.BlockSpec((None, None, LRU_BLK, SEC), lambda b, s: (S16_XL, b, jfun(s), 0)),
            pl.BlockSpec((None, None, BF16_ROWS, SEC),
                         lambda b, s: (S16_XL, b, jnp.maximum(jfun(s) * per_blk - 1, 0), 0)),
            pl.BlockSpec((None, None, BF16_ROWS, SEC),
                         lambda b, s: (S16_XL, b, jnp.minimum((jfun(s) + 1) * per_blk, n_halo - 1), 0)),
        ]

    in_specs = [pl.BlockSpec((None, None, CTX_LEN, SEC), lambda b, s: (S16_XL, b, 0, 0))]
    in_specs += lat_specs(jf) + lat_specs(jb) + [
        pl.BlockSpec((None, CONV_W, SEC), lambda b, s: (layer, 0, 0)),
        pl.BlockSpec((None, 1, SEC), lambda b, s: (layer, 0, 0)),
        pl.BlockSpec((None,) + wd.shape[1:], lambda b, s: (layer, 0, 0, 0, 0, 0)),
        pl.BlockSpec((None,) + b_rg.shape[1:], lambda b, s: (layer, 0, 0, 0)),
        pl.BlockSpec((None,) + lam.shape[1:], lambda b, s: (layer, 0, 0)),
    ]
    out_specs = [
        pl.BlockSpec((None, CTX_LEN, R_WIDTH), lambda b, s: (b, 0, 0)),
        pl.BlockSpec((None, CTX_LEN, R_WIDTH), lambda b, s: (b, 0, 0)),
        pl.BlockSpec((None, LRU_BLK, R_WIDTH), lambda b, s: (b, jf(s), 0)),
        pl.BlockSpec((None, LRU_BLK, R_WIDTH), lambda b, s: (b, jb(s), 0)),
    ]
    out_shape = [
        jax.ShapeDtypeStruct((bsz, CTX_LEN, R_WIDTH), BF16),
        jax.ShapeDtypeStruct((bsz, CTX_LEN, R_WIDTH), BF16),
        jax.ShapeDtypeStruct((bsz, t, R_WIDTH), BF16),
        jax.ShapeDtypeStruct((bsz, t, R_WIDTH), BF16),
    ]
    return pl.pallas_call(
        functools.partial(_rglru_kernel, n_lat=n_lat),
        grid=(bsz, n_lat + 1),
        in_specs=in_specs,
        out_specs=out_specs,
        out_shape=out_shape,
        scratch_shapes=[
            pltpu.VMEM((LRU_BLK + 2 * SUBLANES, R_WIDTH), F32),
            pltpu.VMEM((LRU_BLK, R_WIDTH), F32),
            pltpu.VMEM((LRU_BLK, R_WIDTH), F32),
            pltpu.VMEM((2, SUBLANES, R_WIDTH), F32),
            pltpu.VMEM((2, R_WIDTH), F32),
        ],
        compiler_params=_cparams(("arbitrary", "arbitrary")),
        name="rglru",
    )(pc, px, px, px, px, px, px, conv_w, conv_b, wd, b_rg, lam)


def _out_proj_kernel(hf_ref, hb_ref, yf_ref, yb_ref, o_ref, zm_ref, zl_ref, x_ref, gt_ref, mg_ref, w_ref, fg_ref,
                     out_ref, *, final):
    hm = hf_ref[...] + hb_ref[...]
    parts = []
    for h in range(M_HEADS):
        hh = hm[:, h * M_DV:(h + 1) * M_DV]
        parts.append(hh * lax.rsqrt(jnp.mean(hh * hh, axis=-1, keepdims=True) + EPS))
    hn = jnp.concatenate(parts, axis=1) * mg_ref[...]
    ym = hn * jax.nn.sigmoid(o_ref[...].astype(F32)) * _silu(zm_ref[...].astype(F32))
    yr = (yf_ref[...].astype(F32) + yb_ref[...].astype(F32)) * _silu(zl_ref[...].astype(F32))
    y = jnp.concatenate([ym, yr], axis=1).astype(BF16)
    xn = x_ref[...] + gt_ref[...] * jnp.dot(y, w_ref[...], preferred_element_type=F32)
    if final:
        xn = xn * lax.rsqrt(jnp.mean(xn * xn, axis=-1, keepdims=True) + EPS) * fg_ref[...]
    out_ref[...] = xn


def _out_proj(hf, hb, yf, yb, p16, x2d, mod_l, row_of_tile, m_norm_g, w_out, final_g, layer, tm, final):
    m, d = x2d.shape

    def tok(width):
        return pl.BlockSpec((tm, width), lambda i: (i, 0))

    def sec(k):
        return pl.BlockSpec((None, tm, SEC), lambda i: (k, i, 0))

    return pl.pallas_call(
        functools.partial(_out_proj_kernel, final=final),
        grid=(m // tm,),
        in_specs=[
            tok(M_WIDTH), tok(M_WIDTH), tok(R_WIDTH), tok(R_WIDTH),
            sec(S16_O), sec(S16_ZM), sec(S16_ZL),
            tok(d),
            pl.BlockSpec((None, None, 1, d), lambda i: (row_of_tile(i), 2, 0, 0)),
            pl.BlockSpec((None, 1, M_WIDTH), lambda i: (layer, 0, 0)),
            pl.BlockSpec((None, d, d), lambda i: (layer, 0, 0)),
            pl.BlockSpec((1, d), lambda i: (0, 0)),
        ],
        out_specs=tok(d),
        out_shape=jax.ShapeDtypeStruct((m, d), F32),
        compiler_params=_cparams(("arbitrary",)),
        name="out_proj",
    )(hf, hb, yf, yb, p16, p16, p16, x2d, mod_l, m_norm_g, w_out, final_g)


def _dense_gate_tiles(w_rg):
    depth = w_rg.shape[0]
    per = LRU_TILE // R_BLOCK
    w = w_rg.reshape(depth, 2, 2, R_WIDTH // LRU_TILE, per, R_BLOCK, R_BLOCK)
    eye = jnp.eye(per, dtype=w.dtype)
    dense = jnp.einsum('ldgtpij,pq->ldgtpiqj', w, eye)
    return dense.reshape(depth, 2, 2, R_WIDTH // LRU_TILE, LRU_TILE, LRU_TILE).astype(BF16)


def kernel(x, c, ctx, c_ctx, w_mod, b_mod, norm_g, w_in, b_gate, conv_qk_w, conv_qk_b, m_norm_g, conv_r_w, conv_r_b,
           w_rg, b_rg, lru_lambda, w_out, final_g):
    bsz, t, d = x.shape
    depth = w_mod.shape[0]
    nh = M_HEADS

    o_g = WA_SECS * SEC
    o_xl = o_g + 4 * nh
    w_a = w_in[:, :, :o_g].astype(BF16)
    w_b = w_in[:, :, o_xl:].astype(BF16)
    wg = w_in[:, :, o_g:o_xl]
    zpad = jnp.zeros((depth, d, LANES - 2 * nh), w_in.dtype)
    w_gate = jnp.concatenate([wg[:, :, 0:nh], wg[:, :, 2 * nh:3 * nh], zpad,
                              wg[:, :, nh:2 * nh], wg[:, :, 3 * nh:4 * nh], zpad], axis=-1).astype(BF16)
    bpad = jnp.zeros((depth, LANES - 2 * nh), F32)
    bg_i = jnp.concatenate([b_gate[:, 0:nh], b_gate[:, 2 * nh:3 * nh], bpad], axis=-1)
    bg_f = jnp.concatenate([b_gate[:, nh:2 * nh], b_gate[:, 3 * nh:4 * nh], bpad], axis=-1)
    bg2 = jnp.concatenate([bg_i[:, None, :], bg_f[:, None, :], jnp.zeros((depth, SUBLANES - 2, LANES), F32)], axis=1)
    w_out_b = w_out.astype(BF16)
    wd = _dense_gate_tiles(w_rg)
    norm_g3 = norm_g[:, None, :]
    m_norm_g3 = m_norm_g[:, None, :]
    conv_qk_b3 = conv_qk_b[:, None, :]
    conv_r_b3 = conv_r_b[:, None, :]
    b_rg4 = b_rg.reshape(depth, 2, 2, R_WIDTH)
    fg = final_g[None, :]

    cvec = jnp.concatenate([c, c_ctx[None, :], jnp.zeros((SUBLANES - bsz - 1, d), F32)], axis=0)
    mods = _modulation(cvec, w_mod, b_mod).reshape(depth, SUBLANES, 3, 1, d)

    tm_in = 1024
    tm_ctx = bsz * CTX_LEN
    tm_out = 256
    x2d = x.reshape(bsz * t, d)
    c2d = ctx.reshape(bsz * CTX_LEN, d)
    lat_row_in = lambda i: i // (t // tm_in)
    lat_row_out = lambda i: i // (t // tm_out)
    ctx_row = lambda i: bsz

    for l in range(depth):
        last = l == depth - 1
        mod_l = mods[l]
        px32, px16, gx = _in_proj(x2d, mod_l, lat_row_in, norm_g3, w_a, w_b, w_gate, l, tm_in)
        pc32, pc16, gc = _in_proj(c2d, mod_l, ctx_row, norm_g3, w_a, w_b, w_gate, l, tm_ctx)
        hcf, hcb, hf, hb = _mlstm(pc32.reshape(N_SEC32, bsz, CTX_LEN, SEC), gc.reshape(bsz, CTX_LEN, GATE_W),
                                  px32.reshape(N_SEC32, bsz, t, SEC), gx.reshape(bsz, t, GATE_W),
                                  conv_qk_w, conv_qk_b3, bg2, l)
        ycf, ycb, yf, yb = _rglru(pc16.reshape(N_SEC - N_SEC32, bsz, CTX_LEN, SEC),
                                  px16.reshape(N_SEC - N_SEC32, bsz, t, SEC),
                                  conv_r_w, conv_r_b3, wd, b_rg4, lru_lambda, l)
        x2d = _out_proj(hf.reshape(bsz * t, M_WIDTH), hb.reshape(bsz * t, M_WIDTH),
                        yf.reshape(bsz * t, R_WIDTH), yb.reshape(bsz * t, R_WIDTH),
                        px16, x2d, mod_l, lat_row_out, m_norm_g3, w_out_b, fg, l, tm_out, last)
        if not last:
            c2d = _out_proj(hcf.reshape(bsz * CTX_LEN, M_WIDTH), hcb.reshape(bsz * CTX_LEN, M_WIDTH),
                            ycf.reshape(bsz * CTX_LEN, R_WIDTH), ycb.reshape(bsz * CTX_LEN, R_WIDTH),
                            pc16, c2d, mod_l, ctx_row, m_norm_g3, w_out_b, fg, l, tm_out, False)
    return x2d.reshape(bsz, t, d)
```

```python
import functools

import jax
import jax.numpy as jnp
from jax import lax
from jax.experimental import pallas as pl
from jax.experimental.pallas import tpu as pltpu

D_MODEL = 2048
DEPTH = 4
CTX_LEN = 256
GRID_W = 64
M_WIDTH = 1024
R_WIDTH = 1024
M_HEADS = 4
M_DV = 256
M_DQK = 128
M_QK = 512
R_BLOCKS = 16
R_BLOCK = 64
CONV_W = 4
LRU_C = 8.0
EPS = 1e-6

LANES = 128
SUBLANES = 8
BF16_ROWS = 16
SEC = 1024
N_SEC = 6
N_SEC32 = 2
S32_QK, S32_V = 0, 1
S16_O, S16_ZM, S16_XL, S16_ZL = 0, 1, 2, 3
WA_SECS = 4
GATE_W = 2 * LANES
CTX_CHUNK = CTX_LEN
COL_GROUP = SUBLANES
LAT_CHUNK = COL_GROUP * GRID_W
LRU_BLK = 256
LRU_TILE = 256
VMEM_LIMIT = 56 * 1024 * 1024
LOG2E = 1.4426950408889634

F32 = jnp.float32
BF16 = jnp.bfloat16


def _cparams(sem):
    return pltpu.CompilerParams(dimension_semantics=sem, vmem_limit_bytes=VMEM_LIMIT)


def _silu_t(x):
    h = 0.5 * x
    return h + h * jnp.tanh(h)


def _sigmoid_t(x):
    return 0.5 * jnp.tanh(0.5 * x) + 0.5


def _mod_kernel(c_ref, w_ref, b_ref, o_ref):
    c = c_ref[...]
    s = (c * jax.nn.sigmoid(c)).astype(BF16)
    o_ref[...] = jnp.dot(s, w_ref[...].astype(BF16), preferred_element_type=F32) + b_ref[...]


def _modulation(cvec, w_mod, b_mod):
    depth, d, n = w_mod.shape
    tn = 1024
    return pl.pallas_call(
        _mod_kernel,
        grid=(depth, n // tn),
        in_specs=[
            pl.BlockSpec((SUBLANES, d), lambda l, j: (0, 0)),
            pl.BlockSpec((None, d, tn), lambda l, j: (l, 0, j)),
            pl.BlockSpec((None, 1, tn), lambda l, j: (l, 0, j)),
        ],
        out_specs=pl.BlockSpec((None, SUBLANES, tn), lambda l, j: (l, 0, j)),
        out_shape=jax.ShapeDtypeStruct((depth, SUBLANES, n), F32),
        compiler_params=_cparams(("arbitrary", "arbitrary")),
        name="mod",
    )(cvec, w_mod, b_mod.reshape(depth, 1, n))


def _in_proj_kernel(x_ref, sh_ref, sc_ref, g_ref, w_ref, wg_ref, p32_ref, p16_ref, gate_ref, h_scr):
    n = pl.program_id(1)

    @pl.when(n == 0)
    def _():
        x = x_ref[...]
        ms = jnp.mean(x * x, axis=-1, keepdims=True)
        y = x * lax.rsqrt(ms + EPS) * g_ref[...]
        h = (y * (1.0 + sc_ref[...]) + sh_ref[...]).astype(BF16)
        h_scr[...] = h
        gate_ref[...] = jnp.dot(h, wg_ref[...], preferred_element_type=F32)

    @pl.when(n < N_SEC32)
    def _():
        p32_ref[...] = jnp.dot(h_scr[...], w_ref[...], preferred_element_type=F32)

    @pl.when(n >= N_SEC32)
    def _():
        p16_ref[...] = jnp.dot(h_scr[...], w_ref[...], preferred_element_type=F32).astype(BF16)


def _in_proj(x2d, mod_l, row_of_tile, norm_g, w_main, w_gate, layer, tm):
    m, d = x2d.shape
    return pl.pallas_call(
        _in_proj_kernel,
        grid=(m // tm, N_SEC),
        in_specs=[
            pl.BlockSpec((tm, d), lambda i, n: (i, 0)),
            pl.BlockSpec((None, None, 1, d), lambda i, n: (row_of_tile(i), 0, 0, 0)),
            pl.BlockSpec((None, None, 1, d), lambda i, n: (row_of_tile(i), 1, 0, 0)),
            pl.BlockSpec((None, 1, d), lambda i, n: (layer, 0, 0)),
            pl.BlockSpec((None, d, SEC), lambda i, n: (layer, 0, n)),
            pl.BlockSpec((None, d, GATE_W), lambda i, n: (layer, 0, 0)),
        ],
        out_specs=[
            pl.BlockSpec((None, tm, SEC), lambda i, n: (jnp.minimum(n, N_SEC32 - 1), i, 0)),
            pl.BlockSpec((None, tm, SEC), lambda i, n: (jnp.maximum(n - N_SEC32, 0), i, 0)),
            pl.BlockSpec((tm, GATE_W), lambda i, n: (i, 0)),
        ],
        out_shape=[
            jax.ShapeDtypeStruct((N_SEC32, m, SEC), F32),
            jax.ShapeDtypeStruct((N_SEC - N_SEC32, m, SEC), BF16),
            jax.ShapeDtypeStruct((m, GATE_W), F32),
        ],
        scratch_shapes=[pltpu.VMEM((tm, d), BF16)],
        compiler_params=_cparams(("arbitrary", "arbitrary")),
        name="in_proj",
    )(x2d, mod_l, mod_l, norm_g, w_main, w_gate)


def _win_cast_kernel(w_ref, o_ref, *, split, skip):
    w = w_ref[...]
    o_ref[:, :split] = w[:, :split].astype(BF16)
    o_ref[:, split:] = w[:, split + skip:].astype(BF16)


def _win_cast(w_in, split, skip):
    depth, d, n_in = w_in.shape
    tr = 256
    return pl.pallas_call(
        functools.partial(_win_cast_kernel, split=split, skip=skip),
        grid=(depth, d // tr),
        in_specs=[pl.BlockSpec((None, tr, n_in), lambda l, r: (l, r, 0))],
        out_specs=pl.BlockSpec((None, tr, n_in - skip), lambda l, r: (l, r, 0)),
        out_shape=jax.ShapeDtypeStruct((depth, d, n_in - skip), BF16),
        compiler_params=_cparams(("arbitrary", "arbitrary")),
        name="win_cast",
    )(w_in)


def _cast_kernel(w_ref, o_ref):
    o_ref[...] = w_ref[...].astype(BF16)


def _cast_bf16(w):
    depth, r, c = w.shape
    tr = 512
    return pl.pallas_call(
        _cast_kernel,
        grid=(depth, r // tr),
        in_specs=[pl.BlockSpec((None, tr, c), lambda l, j: (l, j, 0))],
        out_specs=pl.BlockSpec((None, tr, c), lambda l, j: (l, j, 0)),
        out_shape=jax.ShapeDtypeStruct((depth, r, c), BF16),
        compiler_params=_cparams(("arbitrary", "arbitrary")),
        name="cast_bf16",
    )(w)


def _shifted(x, k, reverse, ident):
    n = x.shape[0]
    row = lax.broadcasted_iota(jnp.int32, x.shape, 0)
    if reverse:
        return jnp.where(row < n - k, pltpu.roll(x, n - k, axis=0), ident)
    return jnp.where(row >= k, pltpu.roll(x, k, axis=0), ident)


def _scan_rows(x, op, ident, reverse):
    n = x.shape[0]
    k = 1
    while k < n:
        x = op(x, _shifted(x, k, reverse, ident))
        k *= 2
    return x


def _scan_colmajor(x, op, ident, reverse):
    n = x.shape[0]
    k = SUBLANES
    while k < n:
        x = op(x, _shifted(x, k, reverse, ident))
        k *= 2
    tot = x[0:SUBLANES, :] if reverse else x[n - SUBLANES:n, :]
    tot = _scan_rows(tot, op, ident, reverse)
    tot = _shifted(tot, 1, reverse, ident)
    return op(x, jnp.concatenate([tot] * (n // SUBLANES), axis=0))


def _conv_rows(xs_ref, n, w_ref, b_ref, step):
    base = SUBLANES - step
    acc = b_ref[...] + w_ref[0:1, :] * xs_ref[base:base + n, :]
    for j in range(1, CONV_W):
        acc = acc + w_ref[j:j + 1, :] * xs_ref[base + j * step:base + j * step + n, :]
    return acc


def _mlstm_dir(d, qk, v, gates, bg_ref, ct_ref, m_ref, scan, posdiff, write_h):
    n = qk.shape[0]
    rev = d == 1
    gi = gates[:, :LANES] + bg_ref[0:1, :]
    lf = jax.nn.log_sigmoid(gates[:, LANES:] + bg_ref[1:2, :])
    bc = scan(lf, jnp.add, 0.0, rev)
    a = gi - bc
    m_prev = m_ref[d, 0:1, :]
    mm = jnp.maximum(scan(a, jnp.maximum, -jnp.inf, rev), m_prev)
    inter = jnp.exp(m_prev - mm)
    em = jnp.exp(-(bc + mm))
    last = 0 if rev else n - 1
    mm_last = mm[last:last + 1, :]
    m_new = bc[last:last + 1, :] + mm_last
    decay = jnp.exp(m_prev - mm_last)
    wcol = jnp.exp(a - mm_last)
    a_t = (a * LOG2E).T
    mm2 = mm * LOG2E
    mask = (posdiff <= 0) if rev else (posdiff >= 0)
    ones = jnp.ones((n, LANES), BF16)
    for h in range(M_HEADS):
        e = d * M_HEADS + h
        qf = qk[:, h * M_DQK:(h + 1) * M_DQK]
        q = qf.astype(BF16)
        kf = qk[:, M_QK + h * M_DQK:M_QK + (h + 1) * M_DQK] * (M_DQK ** -0.5)
        vaug = jnp.concatenate([v[:, h * M_DV:(h + 1) * M_DV].astype(BF16), ones], axis=1)
        dmat = jnp.where(mask, jnp.exp2(a_t[e:e + 1, :] - mm2[:, e:e + 1]), 0.0)
        s = lax.dot_general(q, kf.astype(BF16), (((1,), (1,)), ((), ())), preferred_element_type=F32)
        sw = (s * dmat).astype(BF16)
        ct = ct_ref[e]
        qi = (qf * inter[:, e:e + 1]).astype(BF16)
        num = jnp.dot(jnp.concatenate([sw, qi], axis=1), jnp.concatenate([vaug, ct.astype(BF16)], axis=0),
                      preferred_element_type=F32)
        den = jnp.maximum(jnp.abs(num[:, M_DV:]), em[:, e:e + 1])
        write_h(h, num[:, :M_DV] / jnp.concatenate([den, den], axis=1))
        kw = (kf * wcol[:, e:e + 1]).astype(BF16)
        upd = lax.dot_general(kw, vaug, (((0,), (0,)), ((), ())), preferred_element_type=F32)
        ct_ref[e] = decay[:, e:e + 1] * ct + upd
    m_ref[d, 0:1, :] = m_new


def _mlstm_kernel(cqk_ref, cv_ref, cg_ref,
                  qf_ref, pf_ref, nf_ref, vf_ref, gf_ref,
                  qb_ref, pb_ref, nb_ref, vb_ref, gb_ref,
                  cw_ref, cb_ref, bg_ref,
                  hcf_ref, hcb_ref, hf_ref, hb_ref,
                  xs_ref, ct_ref, m_ref, pd_ref, *, n_lat):
    s = pl.program_id(1)
    lc = LAT_CHUNK
    zero_rows = jnp.zeros((SUBLANES, 2 * M_QK), F32)

    @pl.when(s == 0)
    def _():
        ct_ref[...] = jnp.zeros_like(ct_ref)
        m_ref[...] = jnp.zeros_like(m_ref)
        row = lax.broadcasted_iota(jnp.int32, (lc, lc), 0)
        col = lax.broadcasted_iota(jnp.int32, (lc, lc), 1)
        pos_r = (row % COL_GROUP) * GRID_W + row // COL_GROUP
        pos_c = (col % COL_GROUP) * GRID_W + col // COL_GROUP
        pd_ref[...] = pos_r - pos_c
        xs_ref[0:SUBLANES, :] = zero_rows
        xs_ref[SUBLANES + CTX_CHUNK:2 * SUBLANES + CTX_CHUNK, :] = zero_rows
        xs_ref[SUBLANES:SUBLANES + CTX_CHUNK, :] = cqk_ref[...]
        qk = _silu_t(_conv_rows(xs_ref, CTX_CHUNK, cw_ref, cb_ref, 1))
        v = cv_ref[...]
        g = cg_ref[...]
        crow = lax.broadcasted_iota(jnp.int32, (CTX_CHUNK, CTX_CHUNK), 0)
        ccol = lax.broadcasted_iota(jnp.int32, (CTX_CHUNK, CTX_CHUNK), 1)
        for d, out_ref in ((0, hcf_ref), (1, hcb_ref)):
            def write_h(h, val, out_ref=out_ref):
                out_ref[:, h * M_DV:(h + 1) * M_DV] = val
            _mlstm_dir(d, qk, v, g, bg_ref, ct_ref, m_ref, _scan_rows, crow - ccol, write_h)

    @pl.when(s > 0)
    def _():
        sub = lax.broadcasted_iota(jnp.int32, (SUBLANES, 2 * M_QK), 0)
        for d, q_ref, p_ref, n_ref, v_ref, g_ref, out_ref in (
                (0, qf_ref, pf_ref, nf_ref, vf_ref, gf_ref, hf_ref),
                (1, qb_ref, pb_ref, nb_ref, vb_ref, gb_ref, hb_ref)):
            j = (s - 1) if d == 0 else (n_lat - s)
            has_prev = (j > 0).astype(F32)
            has_next = (j < n_lat - 1).astype(F32)
            x = q_ref[...].reshape(lc, 2 * M_QK)
            xs_ref[SUBLANES:SUBLANES + lc, :] = x
            x_last = x[lc - SUBLANES:lc, :]
            xs_ref[0:SUBLANES, :] = jnp.where(sub == 0, pltpu.roll(p_ref[...], 1, axis=0) * has_prev,
                                              pltpu.roll(x_last, 1, axis=0))
            for k in range(2):
                xs_ref[SUBLANES + lc + k * SUBLANES:2 * SUBLANES + lc + k * SUBLANES, :] = jnp.where(
                    sub == SUBLANES - 1, pltpu.roll(n_ref[k], SUBLANES - 1, axis=0) * has_next,
                    pltpu.roll(x[k * SUBLANES:(k + 1) * SUBLANES, :], SUBLANES - 1, axis=0))
            qk = _silu_t(_conv_rows(xs_ref, lc, cw_ref, cb_ref, SUBLANES))
            v = v_ref[...].reshape(lc, M_WIDTH)
            g = g_ref[...].reshape(lc, GATE_W)

            def write_h(h, val, out_ref=out_ref):
                out_ref[:, :, h * M_DV:(h + 1) * M_DV] = val.reshape(GRID_W, COL_GROUP, M_DV)
            _mlstm_dir(d, qk, v, g, bg_ref, ct_ref, m_ref, _scan_colmajor, pd_ref[...], write_h)


def _mlstm(pc, gc, px, gx, conv_w, conv_b, bg2, layer):
    _, bsz, t, _ = px.shape
    rows = t // GRID_W
    n_lat = GRID_W // COL_GROUP
    pxv = px.reshape(N_SEC32, bsz, rows, GRID_W, SEC)
    gxv = gx.reshape(bsz, rows, GRID_W, GATE_W)

    def jf(s):
        return jnp.maximum(s - 1, 0)

    def jb(s):
        return jnp.minimum(n_lat - s, n_lat - 1)

    def lat_specs(jfun):
        return [
            pl.BlockSpec((None, None, rows, COL_GROUP, SEC), lambda b, s: (S32_QK, b, 0, jfun(s), 0)),
            pl.BlockSpec((None, None, None, COL_GROUP, SEC),
                         lambda b, s: (S32_QK, b, rows - 1, jnp.maximum(jfun(s) - 1, 0), 0)),
            pl.BlockSpec((None, None, 2, COL_GROUP, SEC),
                         lambda b, s: (S32_QK, b, 0, jnp.minimum(jfun(s) + 1, n_lat - 1), 0)),
            pl.BlockSpec((None, None, rows, COL_GROUP, SEC), lambda b, s: (S32_V, b, 0, jfun(s), 0)),
            pl.BlockSpec((None, rows, COL_GROUP, GATE_W), lambda b, s: (b, 0, jfun(s), 0)),
        ]

    in_specs = [
        pl.BlockSpec((None, None, CTX_LEN, SEC), lambda b, s: (S32_QK, b, 0, 0)),
        pl.BlockSpec((None, None, CTX_LEN, SEC), lambda b, s: (S32_V, b, 0, 0)),
        pl.BlockSpec((None, CTX_LEN, GATE_W), lambda b, s: (b, 0, 0)),
    ] + lat_specs(jf) + lat_specs(jb) + [
        pl.BlockSpec((None, CONV_W, SEC), lambda b, s: (layer, 0, 0)),
        pl.BlockSpec((None, 1, SEC), lambda b, s: (layer, 0, 0)),
        pl.BlockSpec((None, SUBLANES, LANES), lambda b, s: (layer, 0, 0)),
    ]
    out_specs = [
        pl.BlockSpec((None, CTX_LEN, M_WIDTH), lambda b, s: (b, 0, 0)),
        pl.BlockSpec((None, CTX_LEN, M_WIDTH), lambda b, s: (b, 0, 0)),
        pl.BlockSpec((None, rows, COL_GROUP, M_WIDTH), lambda b, s: (b, 0, jf(s), 0)),
        pl.BlockSpec((None, rows, COL_GROUP, M_WIDTH), lambda b, s: (b, 0, jb(s), 0)),
    ]
    out_shape = [
        jax.ShapeDtypeStruct((bsz, CTX_LEN, M_WIDTH), F32),
        jax.ShapeDtypeStruct((bsz, CTX_LEN, M_WIDTH), F32),
        jax.ShapeDtypeStruct((bsz, rows, GRID_W, M_WIDTH), F32),
        jax.ShapeDtypeStruct((bsz, rows, GRID_W, M_WIDTH), F32),
    ]
    hcf, hcb, hf, hb = pl.pallas_call(
        functools.partial(_mlstm_kernel, n_lat=n_lat),
        grid=(bsz, n_lat + 1),
        in_specs=in_specs,
        out_specs=out_specs,
        out_shape=out_shape,
        scratch_shapes=[
            pltpu.VMEM((LAT_CHUNK + 3 * SUBLANES, 2 * M_QK), F32),
            pltpu.VMEM((2 * M_HEADS, M_DQK, M_DV + LANES), F32),
            pltpu.VMEM((2, SUBLANES, LANES), F32),
            pltpu.VMEM((LAT_CHUNK, LAT_CHUNK), jnp.int32),
        ],
        compiler_params=_cparams(("arbitrary", "arbitrary")),
        name="mlstm",
    )(pc, pc, gc, pxv, pxv, pxv, pxv, gxv, pxv, pxv, pxv, pxv, gxv, conv_w, conv_b, bg2)
    return hcf, hcb, hf.reshape(bsz, t, M_WIDTH), hb.reshape(bsz, t, M_WIDTH)


def _lru_conv(x_bf, halo, sh_ref, cw_ref, cb_ref):
    n = x_bf.shape[0]
    xm1 = jnp.dot(sh_ref[0], x_bf, preferred_element_type=F32)
    xp1 = jnp.dot(sh_ref[1], x_bf, preferred_element_type=F32)
    xp2 = jnp.dot(sh_ref[2], x_bf, preferred_element_type=F32)
    if halo is not None:
        prev_row, next_rows = halo
        sub = lax.broadcasted_iota(jnp.int32, (SUBLANES, x_bf.shape[1]), 0)
        head = xm1[0:SUBLANES, :] + jnp.where(sub == 0, prev_row, 0.0)
        xm1 = jnp.concatenate([head, xm1[SUBLANES:, :]], axis=0)
        tail = xp1[n - SUBLANES:, :] + jnp.where(sub == SUBLANES - 1, next_rows[0:1, :], 0.0)
        xp1 = jnp.concatenate([xp1[:n - SUBLANES, :], tail], axis=0)
        tail = (xp2[n - SUBLANES:, :] + jnp.where(sub == SUBLANES - 2, next_rows[0:1, :], 0.0)
                + jnp.where(sub == SUBLANES - 1, next_rows[1:2, :], 0.0))
        xp2 = jnp.concatenate([xp2[:n - SUBLANES, :], tail], axis=0)
    return (cb_ref[...] + cw_ref[0:1, :] * xm1 + cw_ref[1:2, :] * x_bf.astype(F32)
            + cw_ref[2:3, :] * xp1 + cw_ref[3:4, :] * xp2)


def _lru_gates(d, xc, wd_ref, br_ref, sp_ref, a_ref, u_ref):
    xb = xc.astype(BF16)
    for j in range(R_WIDTH // LRU_TILE):
        sl = slice(j * LRU_TILE, (j + 1) * LRU_TILE)
        xj = xb[:, sl]
        tr = jnp.tanh(jnp.dot(xj, wd_ref[d, 0, j], preferred_element_type=F32) + br_ref[d, 0:1, sl])
        ti = jnp.tanh(jnp.dot(xj, wd_ref[d, 1, j], preferred_element_type=F32) + br_ref[d, 1:2, sl])
        sp = sp_ref[d:d + 1, sl]
        nla = tr * sp + sp
        a = jnp.exp(-nla)
        a_ref[:, sl] = a
        xh = 0.5 * xc[:, sl]
        z = jnp.tanh(nla) * (a * a + 1.0)
        root = jnp.where(z > 0.0, z * lax.rsqrt(z), 0.0)
        u_ref[:, sl] = root * (ti * xh + xh)


def _lru_scan(d, a_ref, u_ref, h_ref, out_ref, n):
    rev = d == 1
    groups = n // SUBLANES
    row = lax.broadcasted_iota(jnp.int32, (SUBLANES, R_WIDTH), 0)

    def body(g, carry):
        gg = (groups - 1 - g) if rev else g
        r0 = pl.multiple_of(gg * SUBLANES, SUBLANES)
        a = a_ref[pl.ds(r0, SUBLANES), :]
        u = u_ref[pl.ds(r0, SUBLANES), :]
        k = 1
        while k < SUBLANES:
            if rev:
                ok = row < SUBLANES - k
                a_s = pltpu.roll(a, SUBLANES - k, axis=0)
                u_s = pltpu.roll(u, SUBLANES - k, axis=0)
            else:
                ok = row >= k
                a_s = pltpu.roll(a, k, axis=0)
                u_s = pltpu.roll(u, k, axis=0)
            u = u + a * jnp.where(ok, u_s, 0.0)
            a = a * jnp.where(ok, a_s, 1.0)
            k *= 2
        hh = u + a * carry
        u_ref[pl.ds(r0, SUBLANES), :] = hh
        return hh[0:1, :] if rev else hh[SUBLANES - 1:SUBLANES, :]

    h_ref[d, 0:1, :] = lax.fori_loop(0, groups, body, h_ref[d, 0:1, :])
    out_ref[...] = u_ref[...].astype(out_ref.dtype)


def _rglru_kernel(cx_ref, xf_ref, pf_ref, nf_ref, xb_ref, pb_ref, nb_ref,
                  cw_ref, cb_ref, wd_ref, br_ref, lam_ref,
                  ycf_ref, ycb_ref, yf_ref, yb_ref,
                  sh_ref, a_ref, u_ref, h_ref, sp_ref, *, n_lat):
    s = pl.program_id(1)

    @pl.when(s == 0)
    def _():
        h_ref[...] = jnp.zeros_like(h_ref)
        sp_ref[...] = (0.5 * LRU_C) * jax.nn.softplus(-lam_ref[...])
        row = lax.broadcasted_iota(jnp.int32, (LRU_BLK, LRU_BLK), 0)
        col = lax.broadcasted_iota(jnp.int32, (LRU_BLK, LRU_BLK), 1)
        for k, off in enumerate((-1, 1, 2)):
            sh_ref[k] = jnp.where(col == row + off, 1.0, 0.0).astype(BF16)
        xc = _lru_conv(cx_ref[...], None, sh_ref, cw_ref, cb_ref)
        for d, out_ref in ((0, ycf_ref), (1, ycb_ref)):
            _lru_gates(d, xc, wd_ref, br_ref, sp_ref, a_ref, u_ref)
            _lru_scan(d, a_ref, u_ref, h_ref, out_ref, LRU_BLK)

    @pl.when(s > 0)
    def _():
        for d, x_ref, p_ref, n_ref, out_ref in ((0, xf_ref, pf_ref, nf_ref, yf_ref),
                                                (1, xb_ref, pb_ref, nb_ref, yb_ref)):
            j = (s - 1) if d == 0 else (n_lat - s)
            has_prev = (j > 0).astype(F32)
            has_next = (j < n_lat - 1).astype(F32)
            prev_row = p_ref[...].astype(F32)[BF16_ROWS - 1:BF16_ROWS, :] * has_prev
            next_rows = n_ref[...].astype(F32)[0:2, :] * has_next
            xc = _lru_conv(x_ref[...], (prev_row, next_rows), sh_ref, cw_ref, cb_ref)
            _lru_gates(d, xc, wd_ref, br_ref, sp_ref, a_ref, u_ref)
            _lru_scan(d, a_ref, u_ref, h_ref, out_ref, LRU_BLK)


def _rglru(pc, px, conv_w, conv_b, wd, b_rg, lam, layer):
    _, bsz, t, _ = px.shape
    n_lat = t // LRU_BLK
    per_blk = LRU_BLK // BF16_ROWS
    n_halo = t // BF16_ROWS

    def jf(s):
        return jnp.maximum(s - 1, 0)

    def jb(s):
        return jnp.minimum(n_lat - s, n_lat - 1)

    def lat_specs(jfun):
        return [
            pl.BlockSpec((None, None, LRU_BLK, SEC), lambda b, s: (S16_XL, b, jfun(s), 0)),
            pl.BlockSpec((None, None, BF16_ROWS, SEC),
                         lambda b, s: (S16_XL, b, jnp.maximum(jfun(s) * per_blk - 1, 0), 0)),
            pl.BlockSpec((None, None, BF16_ROWS, SEC),
                         lambda b, s: (S16_XL, b, jnp.minimum((jfun(s) + 1) * per_blk, n_halo - 1), 0)),
        ]

    in_specs = [pl.BlockSpec((None, None, CTX_LEN, SEC), lambda b, s: (S16_XL, b, 0, 0))]
    in_specs += lat_specs(jf) + lat_specs(jb) + [
        pl.BlockSpec((None, CONV_W, SEC), lambda b, s: (layer, 0, 0)),
        pl.BlockSpec((None, 1, SEC), lambda b, s: (layer, 0, 0)),
        pl.BlockSpec((None,) + wd.shape[1:], lambda b, s: (layer, 0, 0, 0, 0, 0)),
        pl.BlockSpec((None,) + b_rg.shape[1:], lambda b, s: (layer, 0, 0, 0)),
        pl.BlockSpec((None,) + lam.shape[1:], lambda b, s: (layer, 0, 0)),
    ]
    out_specs = [
        pl.BlockSpec((None, CTX_LEN, R_WIDTH), lambda b, s: (b, 0, 0)),
        pl.BlockSpec((None, CTX_LEN, R_WIDTH), lambda b, s: (b, 0, 0)),
        pl.BlockSpec((None, LRU_BLK, R_WIDTH), lambda b, s: (b, jf(s), 0)),
        pl.BlockSpec((None, LRU_BLK, R_WIDTH), lambda b, s: (b, jb(s), 0)),
    ]
    out_shape = [
        jax.ShapeDtypeStruct((bsz, CTX_LEN, R_WIDTH), BF16),
        jax.ShapeDtypeStruct((bsz, CTX_LEN, R_WIDTH), BF16),
        jax.ShapeDtypeStruct((bsz, t, R_WIDTH), BF16),
        jax.ShapeDtypeStruct((bsz, t, R_WIDTH), BF16),
    ]
    return pl.pallas_call(
        functools.partial(_rglru_kernel, n_lat=n_lat),
        grid=(bsz, n_lat + 1),
        in_specs=in_specs,
        out_specs=out_specs,
        out_shape=out_shape,
        scratch_shapes=[
            pltpu.VMEM((3, LRU_BLK, LRU_BLK), BF16),
            pltpu.VMEM((LRU_BLK, R_WIDTH), F32),
            pltpu.VMEM((LRU_BLK, R_WIDTH), F32),
            pltpu.VMEM((2, SUBLANES, R_WIDTH), F32),
            pltpu.VMEM((2, R_WIDTH), F32),
        ],
        compiler_params=_cparams(("arbitrary", "arbitrary")),
        name="rglru",
    )(pc, px, px, px, px, px, px, conv_w, conv_b, wd, b_rg, lam)


def _out_proj_kernel(hf_ref, hb_ref, yf_ref, yb_ref, o_ref, zm_ref, zl_ref, x_ref, gt_ref, mg_ref, w_ref, fg_ref,
                     out_ref, *, final):
    hm = hf_ref[...] + hb_ref[...]
    parts = []
    for h in range(M_HEADS):
        hh = hm[:, h * M_DV:(h + 1) * M_DV]
        parts.append(hh * lax.rsqrt(jnp.mean(hh * hh, axis=-1, keepdims=True) + EPS))
    hn = jnp.concatenate(parts, axis=1) * mg_ref[...]
    ym = hn * _sigmoid_t(o_ref[...].astype(F32)) * _silu_t(zm_ref[...].astype(F32))
    yr = (yf_ref[...].astype(F32) + yb_ref[...].astype(F32)) * _silu_t(zl_ref[...].astype(F32))
    y = jnp.concatenate([ym, yr], axis=1).astype(BF16)
    xn = x_ref[...] + gt_ref[...] * jnp.dot(y, w_ref[...], preferred_element_type=F32)
    if final:
        xn = xn * lax.rsqrt(jnp.mean(xn * xn, axis=-1, keepdims=True) + EPS) * fg_ref[...]
    out_ref[...] = xn


def _out_proj(hf, hb, yf, yb, p16, x2d, mod_l, row_of_tile, m_norm_g, w_out, final_g, layer, tm, final):
    m, d = x2d.shape

    def tok(width):
        return pl.BlockSpec((tm, width), lambda i: (i, 0))

    def sec(k):
        return pl.BlockSpec((None, tm, SEC), lambda i: (k, i, 0))

    return pl.pallas_call(
        functools.partial(_out_proj_kernel, final=final),
        grid=(m // tm,),
        in_specs=[
            tok(M_WIDTH), tok(M_WIDTH), tok(R_WIDTH), tok(R_WIDTH),
            sec(S16_O), sec(S16_ZM), sec(S16_ZL),
            tok(d),
            pl.BlockSpec((None, None, 1, d), lambda i: (row_of_tile(i), 2, 0, 0)),
            pl.BlockSpec((None, 1, M_WIDTH), lambda i: (layer, 0, 0)),
            pl.BlockSpec((None, d, d), lambda i: (layer, 0, 0)),
            pl.BlockSpec((1, d), lambda i: (0, 0)),
        ],
        out_specs=tok(d),
        out_shape=jax.ShapeDtypeStruct((m, d), F32),
        compiler_params=_cparams(("arbitrary",)),
        name="out_proj",
    )(hf, hb, yf, yb, p16, p16, p16, x2d, mod_l, m_norm_g, w_out, final_g)


def _dense_gate_tiles(w_rg):
    depth = w_rg.shape[0]
    per = LRU_TILE // R_BLOCK
    w = w_rg.reshape(depth, 2, 2, R_WIDTH // LRU_TILE, per, R_BLOCK, R_BLOCK)
    eye = jnp.eye(per, dtype=w.dtype)
    dense = jnp.einsum('ldgtpij,pq->ldgtpiqj', w, eye)
    return dense.reshape(depth, 2, 2, R_WIDTH // LRU_TILE, LRU_TILE, LRU_TILE).astype(BF16)


def kernel(x, c, ctx, c_ctx, w_mod, b_mod, norm_g, w_in, b_gate, conv_qk_w, conv_qk_b, m_norm_g, conv_r_w, conv_r_b,
           w_rg, b_rg, lru_lambda, w_out, final_g):
    bsz, t, d = x.shape
    depth = w_mod.shape[0]
    nh = M_HEADS

    o_g = WA_SECS * SEC
    o_xl = o_g + 4 * nh
    w_main = _win_cast(w_in, o_g, 4 * nh)
    wg = w_in[:, :, o_g:o_xl]
    zpad = jnp.zeros((depth, d, LANES - 2 * nh), w_in.dtype)
    w_gate = jnp.concatenate([wg[:, :, 0:nh], wg[:, :, 2 * nh:3 * nh], zpad,
                              wg[:, :, nh:2 * nh], wg[:, :, 3 * nh:4 * nh], zpad], axis=-1).astype(BF16)
    bpad = jnp.zeros((depth, LANES - 2 * nh), F32)
    bg_i = jnp.concatenate([b_gate[:, 0:nh], b_gate[:, 2 * nh:3 * nh], bpad], axis=-1)
    bg_f = jnp.concatenate([b_gate[:, nh:2 * nh], b_gate[:, 3 * nh:4 * nh], bpad], axis=-1)
    bg2 = jnp.concatenate([bg_i[:, None, :], bg_f[:, None, :], jnp.zeros((depth, SUBLANES - 2, LANES), F32)], axis=1)
    w_out_b = _cast_bf16(w_out)
    wd = _dense_gate_tiles(0.5 * w_rg)
    norm_g3 = norm_g[:, None, :]
    m_norm_g3 = m_norm_g[:, None, :]
    conv_qk_b3 = conv_qk_b[:, None, :]
    conv_r_b3 = conv_r_b[:, None, :]
    b_rg4 = 0.5 * b_rg.reshape(depth, 2, 2, R_WIDTH)
    fg = final_g[None, :]

    cvec = jnp.concatenate([c, c_ctx[None, :], jnp.zeros((SUBLANES - bsz - 1, d), F32)], axis=0)
    mods = _modulation(cvec, w_mod, b_mod).reshape(depth, SUBLANES, 3, 1, d)

    tm_in = 1024
    tm_ctx = bsz * CTX_LEN
    tm_out = 256
    x2d = x.reshape(bsz * t, d)
    c2d = ctx.reshape(bsz * CTX_LEN, d)
    lat_row_in = lambda i: i // (t // tm_in)
    lat_row_out = lambda i: i // (t // tm_out)
    ctx_row = lambda i: bsz

    for l in range(depth):
        last = l == depth - 1
        mod_l = mods[l]
        px32, px16, gx = _in_proj(x2d, mod_l, lat_row_in, norm_g3, w_main, w_gate, l, tm_in)
        pc32, pc16, gc = _in_proj(c2d, mod_l, ctx_row, norm_g3, w_main, w_gate, l, tm_ctx)
        hcf, hcb, hf, hb = _mlstm(pc32.reshape(N_SEC32, bsz, CTX_LEN, SEC), gc.reshape(bsz, CTX_LEN, GATE_W),
                                  px32.reshape(N_SEC32, bsz, t, SEC), gx.reshape(bsz, t, GATE_W),
                                  conv_qk_w, conv_qk_b3, bg2, l)
        ycf, ycb, yf, yb = _rglru(pc16.reshape(N_SEC - N_SEC32, bsz, CTX_LEN, SEC),
                                  px16.reshape(N_SEC - N_SEC32, bsz, t, SEC),
                                  conv_r_w, conv_r_b3, wd, b_rg4, lru_lambda, l)
        x2d = _out_proj(hf.reshape(bsz * t, M_WIDTH), hb.reshape(bsz * t, M_WIDTH),
                        yf.reshape(bsz * t, R_WIDTH), yb.reshape(bsz * t, R_WIDTH),
                        px16, x2d, mod_l, lat_row_out, m_norm_g3, w_out_b, fg, l, tm_out, last)
        if not last:
            c2d = _out_proj(hcf.reshape(bsz * CTX_LEN, M_WIDTH), hcb.reshape(bsz * CTX_LEN, M_WIDTH),
                            ycf.reshape(bsz * CTX_LEN, R_WIDTH), ycb.reshape(bsz * CTX_LEN, R_WIDTH),
                            pc16, c2d, mod_l, ctx_row, m_norm_g3, w_out_b, fg, l, tm_out, False)
    return x2d.reshape(bsz, t, d)
```

```python
import functools

import jax
import jax.numpy as jnp
from jax import lax
from jax.experimental import pallas as pl
from jax.experimental.pallas import tpu as pltpu

D_MODEL = 2048
DEPTH = 4
CTX_LEN = 256
GRID_W = 64
M_WIDTH = 1024
R_WIDTH = 1024
M_HEADS = 4
M_DV = 256
M_DQK = 128
M_QK = 512
R_BLOCKS = 16
R_BLOCK = 64
CONV_W = 4
LRU_C = 8.0
EPS = 1e-6

LANES = 128
SUBLANES = 8
BF16_ROWS = 16
SEC = 1024
N_SEC = 6
N_SEC32 = 2
S32_QK, S32_V = 0, 1
S16_O, S16_ZM, S16_XL, S16_ZL = 0, 1, 2, 3
WA_SECS = 4
GATE_W = LANES
CTX_CHUNK = CTX_LEN
COL_GROUP = SUBLANES
LAT_CHUNK = COL_GROUP * GRID_W
LRU_BLK = 256
LRU_TILE = 256
VMEM_LIMIT = 56 * 1024 * 1024
LOG2E = 1.4426950408889634

F32 = jnp.float32
BF16 = jnp.bfloat16


def _cparams(sem):
    return pltpu.CompilerParams(dimension_semantics=sem, vmem_limit_bytes=VMEM_LIMIT)


def _silu_t(x):
    h = 0.5 * x
    return h + h * jnp.tanh(h)


def _sigmoid_t(x):
    return 0.5 * jnp.tanh(0.5 * x) + 0.5


def _mod_kernel(c_ref, w_ref, b_ref, o_ref):
    c = c_ref[...]
    s = (c * jax.nn.sigmoid(c)).astype(BF16)
    o_ref[...] = jnp.dot(s, w_ref[...].astype(BF16), preferred_element_type=F32) + b_ref[...]


def _modulation(cvec, w_mod, b_mod):
    depth, d, n = w_mod.shape
    tn = 1024
    return pl.pallas_call(
        _mod_kernel,
        grid=(depth, n // tn),
        in_specs=[
            pl.BlockSpec((SUBLANES, d), lambda l, j: (0, 0)),
            pl.BlockSpec((None, d, tn), lambda l, j: (l, 0, j)),
            pl.BlockSpec((None, 1, tn), lambda l, j: (l, 0, j)),
        ],
        out_specs=pl.BlockSpec((None, SUBLANES, tn), lambda l, j: (l, 0, j)),
        out_shape=jax.ShapeDtypeStruct((depth, SUBLANES, n), F32),
        compiler_params=_cparams(("arbitrary", "arbitrary")),
        name="mod",
    )(cvec, w_mod, b_mod.reshape(depth, 1, n))


def _in_proj_kernel(x_ref, sh_ref, sc_ref, g_ref, w_ref, wg_ref, p32_ref, p16_ref, gate_ref, h_scr):
    n = pl.program_id(1)

    @pl.when(n == 0)
    def _():
        x = x_ref[...]
        ms = jnp.mean(x * x, axis=-1, keepdims=True)
        y = x * lax.rsqrt(ms + EPS) * g_ref[...]
        h = (y * (1.0 + sc_ref[...]) + sh_ref[...]).astype(BF16)
        h_scr[...] = h
        gate_ref[...] = jnp.dot(h, wg_ref[...], preferred_element_type=F32)

    @pl.when(n < N_SEC32)
    def _():
        p32_ref[...] = jnp.dot(h_scr[...], w_ref[...], preferred_element_type=F32)

    @pl.when(n >= N_SEC32)
    def _():
        p16_ref[...] = jnp.dot(h_scr[...], w_ref[...], preferred_element_type=F32).astype(BF16)


def _in_proj(x2d, mod_l, row_of_tile, norm_g, w_main, w_gate, layer, tm):
    m, d = x2d.shape
    return pl.pallas_call(
        _in_proj_kernel,
        grid=(m // tm, N_SEC),
        in_specs=[
            pl.BlockSpec((tm, d), lambda i, n: (i, 0)),
            pl.BlockSpec((None, None, 1, d), lambda i, n: (row_of_tile(i), 0, 0, 0)),
            pl.BlockSpec((None, None, 1, d), lambda i, n: (row_of_tile(i), 1, 0, 0)),
            pl.BlockSpec((None, 1, d), lambda i, n: (layer, 0, 0)),
            pl.BlockSpec((None, d, SEC), lambda i, n: (layer, 0, n)),
            pl.BlockSpec((None, d, GATE_W), lambda i, n: (layer, 0, 0)),
        ],
        out_specs=[
            pl.BlockSpec((None, tm, SEC), lambda i, n: (jnp.minimum(n, N_SEC32 - 1), i, 0)),
            pl.BlockSpec((None, tm, SEC), lambda i, n: (jnp.maximum(n - N_SEC32, 0), i, 0)),
            pl.BlockSpec((tm, GATE_W), lambda i, n: (i, 0)),
        ],
        out_shape=[
            jax.ShapeDtypeStruct((N_SEC32, m, SEC), F32),
            jax.ShapeDtypeStruct((N_SEC - N_SEC32, m, SEC), BF16),
            jax.ShapeDtypeStruct((m, GATE_W), F32),
        ],
        scratch_shapes=[pltpu.VMEM((tm, d), BF16)],
        compiler_params=_cparams(("arbitrary", "arbitrary")),
        name="in_proj",
    )(x2d, mod_l, mod_l, norm_g, w_main, w_gate)


def _win_cast_kernel(w_ref, o_ref, g_ref, *, split, skip):
    w = w_ref[...]
    o_ref[:, :split] = w[:, :split].astype(BF16)
    o_ref[:, split:] = w[:, split + skip:].astype(BF16)
    lane = lax.broadcasted_iota(jnp.int32, (w.shape[0], GATE_W), 1)
    g_ref[...] = jnp.where(lane < skip, w[:, split:split + GATE_W], 0.0).astype(BF16)


def _win_cast(w_in, split, skip):
    depth, d, n_in = w_in.shape
    tr = 256
    return pl.pallas_call(
        functools.partial(_win_cast_kernel, split=split, skip=skip),
        grid=(depth, d // tr),
        in_specs=[pl.BlockSpec((None, tr, n_in), lambda l, r: (l, r, 0))],
        out_specs=[pl.BlockSpec((None, tr, n_in - skip), lambda l, r: (l, r, 0)),
                   pl.BlockSpec((None, tr, GATE_W), lambda l, r: (l, r, 0))],
        out_shape=[jax.ShapeDtypeStruct((depth, d, n_in - skip), BF16),
                   jax.ShapeDtypeStruct((depth, d, GATE_W), BF16)],
        compiler_params=_cparams(("arbitrary", "arbitrary")),
        name="win_cast",
    )(w_in)


def _cast_kernel(w_ref, o_ref):
    o_ref[...] = w_ref[...].astype(BF16)


def _cast_bf16(w):
    depth, r, c = w.shape
    tr = 512
    return pl.pallas_call(
        _cast_kernel,
        grid=(depth, r // tr),
        in_specs=[pl.BlockSpec((None, tr, c), lambda l, j: (l, j, 0))],
        out_specs=pl.BlockSpec((None, tr, c), lambda l, j: (l, j, 0)),
        out_shape=jax.ShapeDtypeStruct((depth, r, c), BF16),
        compiler_params=_cparams(("arbitrary", "arbitrary")),
        name="cast_bf16",
    )(w)


def _shifted(x, k, reverse, ident):
    n = x.shape[0]
    row = lax.broadcasted_iota(jnp.int32, x.shape, 0)
    if reverse:
        return jnp.where(row < n - k, pltpu.roll(x, n - k, axis=0), ident)
    return jnp.where(row >= k, pltpu.roll(x, k, axis=0), ident)


def _scan_rows(x, op, ident, reverse):
    n = x.shape[0]
    k = 1
    while k < n:
        x = op(x, _shifted(x, k, reverse, ident))
        k *= 2
    return x


def _scan_colmajor(x, op, ident, reverse):
    n = x.shape[0]
    k = SUBLANES
    while k < n:
        x = op(x, _shifted(x, k, reverse, ident))
        k *= 2
    tot = x[0:SUBLANES, :] if reverse else x[n - SUBLANES:n, :]
    tot = _scan_rows(tot, op, ident, reverse)
    tot = _shifted(tot, 1, reverse, ident)
    return op(x, jnp.concatenate([tot] * (n // SUBLANES), axis=0))


def _conv_rows(xs_ref, n, w_ref, b_ref, step):
    base = SUBLANES - step
    acc = b_ref[...] + w_ref[0:1, :] * xs_ref[base:base + n, :]
    for j in range(1, CONV_W):
        acc = acc + w_ref[j:j + 1, :] * xs_ref[base + j * step:base + j * step + n, :]
    return acc


def _mlstm_dir(d, qk, v, gates, bg_ref, ct_ref, m_ref, scan, posdiff, write_h):
    n = qk.shape[0]
    rev = d == 1
    gi = gates + bg_ref[0:1, :]
    lf = jax.nn.log_sigmoid(pltpu.roll(gi, LANES - M_HEADS, axis=1))
    bc = scan(lf, jnp.add, 0.0, rev)
    a = gi - bc
    m_prev = m_ref[d, 0:1, :]
    mm = jnp.maximum(scan(a, jnp.maximum, -jnp.inf, rev), m_prev)
    inter = jnp.exp(m_prev - mm)
    em = jnp.exp(-(bc + mm))
    last = 0 if rev else n - 1
    mm_last = mm[last:last + 1, :]
    m_new = bc[last:last + 1, :] + mm_last
    decay = jnp.exp(m_prev - mm_last)
    wcol = jnp.exp(a - mm_last)
    a_t = (a * LOG2E).T
    mm2 = mm * LOG2E
    mask = (posdiff <= 0) if rev else (posdiff >= 0)
    ones = jnp.ones((n, LANES), BF16)
    for h in range(M_HEADS):
        e = 2 * M_HEADS * d + h
        st = d * M_HEADS + h
        qf = qk[:, h * M_DQK:(h + 1) * M_DQK]
        q = qf.astype(BF16)
        kf = qk[:, M_QK + h * M_DQK:M_QK + (h + 1) * M_DQK] * (M_DQK ** -0.5)
        vaug = jnp.concatenate([v[:, h * M_DV:(h + 1) * M_DV].astype(BF16), ones], axis=1)
        dmat = jnp.where(mask, jnp.exp2(a_t[e:e + 1, :] - mm2[:, e:e + 1]), 0.0)
        s = lax.dot_general(q, kf.astype(BF16), (((1,), (1,)), ((), ())), preferred_element_type=F32)
        sw = (s * dmat).astype(BF16)
        ct = ct_ref[st]
        qi = (qf * inter[:, e:e + 1]).astype(BF16)
        num = jnp.dot(jnp.concatenate([sw, qi], axis=1), jnp.concatenate([vaug, ct.astype(BF16)], axis=0),
                      preferred_element_type=F32)
        den = jnp.maximum(jnp.abs(num[:, M_DV:]), em[:, e:e + 1])
        write_h(h, num[:, :M_DV] / jnp.concatenate([den, den], axis=1))
        kw = (kf * wcol[:, e:e + 1]).astype(BF16)
        upd = lax.dot_general(kw, vaug, (((0,), (0,)), ((), ())), preferred_element_type=F32)
        ct_ref[st] = decay[:, e:e + 1] * ct + upd
    m_ref[d, 0:1, :] = m_new


def _mlstm_kernel(cqk_ref, cv_ref, cg_ref,
                  qf_ref, pf_ref, nf_ref, vf_ref, gf_ref,
                  qb_ref, pb_ref, nb_ref, vb_ref, gb_ref,
                  cw_ref, cb_ref, bg_ref,
                  hcf_ref, hcb_ref, hf_ref, hb_ref,
                  xs_ref, ct_ref, m_ref, pd_ref, *, n_lat):
    s = pl.program_id(1)
    lc = LAT_CHUNK
    zero_rows = jnp.zeros((SUBLANES, 2 * M_QK), F32)

    @pl.when(s == 0)
    def _():
        ct_ref[...] = jnp.zeros_like(ct_ref)
        m_ref[...] = jnp.zeros_like(m_ref)
        row = lax.broadcasted_iota(jnp.int32, (lc, lc), 0)
        col = lax.broadcasted_iota(jnp.int32, (lc, lc), 1)
        pos_r = (row % COL_GROUP) * GRID_W + row // COL_GROUP
        pos_c = (col % COL_GROUP) * GRID_W + col // COL_GROUP
        pd_ref[...] = pos_r - pos_c
        xs_ref[0:SUBLANES, :] = zero_rows
        xs_ref[SUBLANES + CTX_CHUNK:2 * SUBLANES + CTX_CHUNK, :] = zero_rows
        xs_ref[SUBLANES:SUBLANES + CTX_CHUNK, :] = cqk_ref[...]
        qk = _silu_t(_conv_rows(xs_ref, CTX_CHUNK, cw_ref, cb_ref, 1))
        v = cv_ref[...]
        g = cg_ref[...]
        crow = lax.broadcasted_iota(jnp.int32, (CTX_CHUNK, CTX_CHUNK), 0)
        ccol = lax.broadcasted_iota(jnp.int32, (CTX_CHUNK, CTX_CHUNK), 1)
        for d, out_ref in ((0, hcf_ref), (1, hcb_ref)):
            def write_h(h, val, out_ref=out_ref):
                out_ref[:, h * M_DV:(h + 1) * M_DV] = val
            _mlstm_dir(d, qk, v, g, bg_ref, ct_ref, m_ref, _scan_rows, crow - ccol, write_h)

    @pl.when(s > 0)
    def _():
        sub = lax.broadcasted_iota(jnp.int32, (SUBLANES, 2 * M_QK), 0)
        for d, q_ref, p_ref, n_ref, v_ref, g_ref, out_ref in (
                (0, qf_ref, pf_ref, nf_ref, vf_ref, gf_ref, hf_ref),
                (1, qb_ref, pb_ref, nb_ref, vb_ref, gb_ref, hb_ref)):
            j = (s - 1) if d == 0 else (n_lat - s)
            has_prev = (j > 0).astype(F32)
            has_next = (j < n_lat - 1).astype(F32)
            x = q_ref[...].reshape(lc, 2 * M_QK)
            xs_ref[SUBLANES:SUBLANES + lc, :] = x
            x_last = x[lc - SUBLANES:lc, :]
            xs_ref[0:SUBLANES, :] = jnp.where(sub == 0, pltpu.roll(p_ref[...], 1, axis=0) * has_prev,
                                              pltpu.roll(x_last, 1, axis=0))
            for k in range(2):
                xs_ref[SUBLANES + lc + k * SUBLANES:2 * SUBLANES + lc + k * SUBLANES, :] = jnp.where(
                    sub == SUBLANES - 1, pltpu.roll(n_ref[k], SUBLANES - 1, axis=0) * has_next,
                    pltpu.roll(x[k * SUBLANES:(k + 1) * SUBLANES, :], SUBLANES - 1, axis=0))
            qk = _silu_t(_conv_rows(xs_ref, lc, cw_ref, cb_ref, SUBLANES))
            v = v_ref[...].reshape(lc, M_WIDTH)
            g = g_ref[...].reshape(lc, GATE_W)

            def write_h(h, val, out_ref=out_ref):
                out_ref[:, :, h * M_DV:(h + 1) * M_DV] = val.reshape(GRID_W, COL_GROUP, M_DV)
            _mlstm_dir(d, qk, v, g, bg_ref, ct_ref, m_ref, _scan_colmajor, pd_ref[...], write_h)


def _mlstm(pc, gc, px, gx, conv_w, conv_b, bg2, layer):
    _, bsz, t, _ = px.shape
    rows = t // GRID_W
    n_lat = GRID_W // COL_GROUP
    pxv = px.reshape(N_SEC32, bsz, rows, GRID_W, SEC)
    gxv = gx.reshape(bsz, rows, GRID_W, GATE_W)

    def jf(s):
        return jnp.maximum(s - 1, 0)

    def jb(s):
        return jnp.minimum(n_lat - s, n_lat - 1)

    def lat_specs(jfun):
        return [
            pl.BlockSpec((None, None, rows, COL_GROUP, SEC), lambda b, s: (S32_QK, b, 0, jfun(s), 0)),
            pl.BlockSpec((None, None, None, COL_GROUP, SEC),
                         lambda b, s: (S32_QK, b, rows - 1, jnp.maximum(jfun(s) - 1, 0), 0)),
            pl.BlockSpec((None, None, 2, COL_GROUP, SEC),
                         lambda b, s: (S32_QK, b, 0, jnp.minimum(jfun(s) + 1, n_lat - 1), 0)),
            pl.BlockSpec((None, None, rows, COL_GROUP, SEC), lambda b, s: (S32_V, b, 0, jfun(s), 0)),
            pl.BlockSpec((None, rows, COL_GROUP, GATE_W), lambda b, s: (b, 0, jfun(s), 0)),
        ]

    in_specs = [
        pl.BlockSpec((None, None, CTX_LEN, SEC), lambda b, s: (S32_QK, b, 0, 0)),
        pl.BlockSpec((None, None, CTX_LEN, SEC), lambda b, s: (S32_V, b, 0, 0)),
        pl.BlockSpec((None, CTX_LEN, GATE_W), lambda b, s: (b, 0, 0)),
    ] + lat_specs(jf) + lat_specs(jb) + [
        pl.BlockSpec((None, CONV_W, SEC), lambda b, s: (layer, 0, 0)),
        pl.BlockSpec((None, 1, SEC), lambda b, s: (layer, 0, 0)),
        pl.BlockSpec((None, SUBLANES, LANES), lambda b, s: (layer, 0, 0)),
    ]
    out_specs = [
        pl.BlockSpec((None, CTX_LEN, M_WIDTH), lambda b, s: (b, 0, 0)),
        pl.BlockSpec((None, CTX_LEN, M_WIDTH), lambda b, s: (b, 0, 0)),
        pl.BlockSpec((None, rows, COL_GROUP, M_WIDTH), lambda b, s: (b, 0, jf(s), 0)),
        pl.BlockSpec((None, rows, COL_GROUP, M_WIDTH), lambda b, s: (b, 0, jb(s), 0)),
    ]
    out_shape = [
        jax.ShapeDtypeStruct((bsz, CTX_LEN, M_WIDTH), F32),
        jax.ShapeDtypeStruct((bsz, CTX_LEN, M_WIDTH), F32),
        jax.ShapeDtypeStruct((bsz, rows, GRID_W, M_WIDTH), F32),
        jax.ShapeDtypeStruct((bsz, rows, GRID_W, M_WIDTH), F32),
    ]
    hcf, hcb, hf, hb = pl.pallas_call(
        functools.partial(_mlstm_kernel, n_lat=n_lat),
        grid=(bsz, n_lat + 1),
        in_specs=in_specs,
        out_specs=out_specs,
        out_shape=out_shape,
        scratch_shapes=[
            pltpu.VMEM((LAT_CHUNK + 3 * SUBLANES, 2 * M_QK), F32),
            pltpu.VMEM((2 * M_HEADS, M_DQK, M_DV + LANES), F32),
            pltpu.VMEM((2, SUBLANES, LANES), F32),
            pltpu.VMEM((LAT_CHUNK, LAT_CHUNK), jnp.int32),
        ],
        compiler_params=_cparams(("arbitrary", "arbitrary")),
        name="mlstm",
    )(pc, pc, gc, pxv, pxv, pxv, pxv, gxv, pxv, pxv, pxv, pxv, gxv, conv_w, conv_b, bg2)
    return hcf, hcb, hf.reshape(bsz, t, M_WIDTH), hb.reshape(bsz, t, M_WIDTH)


def _lru_conv(x_bf, prev_row, next_rows, pm_ref, cw_ref, cb_ref):
    n = x_bf.shape[0]
    s8 = SUBLANES
    xp = jnp.dot(pm_ref[0], x_bf, preferred_element_type=F32)
    sub = lax.broadcasted_iota(jnp.int32, (s8, x_bf.shape[1]), 0)

    def from_next_segment(v, fill):
        return jnp.where(sub == s8 - 1, fill, pltpu.roll(v, s8 - 1, axis=0))

    def from_prev_segment(v, fill):
        return jnp.where(sub == 0, fill, pltpu.roll(v, 1, axis=0))

    first, second, last = xp[0:s8, :], xp[s8:2 * s8, :], xp[n - s8:, :]
    xm1 = jnp.concatenate([from_prev_segment(last, prev_row), xp[:n - s8, :]], axis=0)
    xp1 = jnp.concatenate([xp[s8:, :], from_next_segment(first, next_rows[0:1, :])], axis=0)
    xp2 = jnp.concatenate([xp[2 * s8:, :], from_next_segment(first, next_rows[0:1, :]),
                           from_next_segment(second, next_rows[1:2, :])], axis=0)
    return (cb_ref[...] + cw_ref[0:1, :] * xm1 + cw_ref[1:2, :] * xp
            + cw_ref[2:3, :] * xp1 + cw_ref[3:4, :] * xp2)


def _lru_gates(d, xc, wd_ref, br_ref, sp_ref, a_ref, u_ref):
    xb = xc.astype(BF16)
    for j in range(R_WIDTH // LRU_TILE):
        sl = slice(j * LRU_TILE, (j + 1) * LRU_TILE)
        xj = xb[:, sl]
        tr = jnp.tanh(jnp.dot(xj, wd_ref[d, 0, j], preferred_element_type=F32) + br_ref[d, 0:1, sl])
        ti = jnp.tanh(jnp.dot(xj, wd_ref[d, 1, j], preferred_element_type=F32) + br_ref[d, 1:2, sl])
        sp = sp_ref[d:d + 1, sl]
        nla = tr * sp + sp
        a = jnp.exp2(nla * (-LOG2E))
        a_ref[:, sl] = a
        xh = 0.5 * xc[:, sl]
        z = jnp.tanh(nla) * (a * a + 1.0)
        root = jnp.where(z > 0.0, z * lax.rsqrt(z), 0.0)
        u_ref[:, sl] = root * (ti * xh + xh)


def _lru_scan(d, a_ref, u_ref, h_ref, pm_ref, out_ref, n):
    rev = d == 1
    s8 = SUBLANES
    groups = n // s8
    sub = lax.broadcasted_iota(jnp.int32, (s8, R_WIDTH), 0)

    def body(g, carry):
        h, acc = carry
        gg = (groups - 1 - g) if rev else g
        r0 = pl.multiple_of(gg * s8, s8)
        a = a_ref[pl.ds(r0, s8), :]
        h = a * h + u_ref[pl.ds(r0, s8), :]
        acc = a * acc
        u_ref[pl.ds(r0, s8), :] = h
        a_ref[pl.ds(r0, s8), :] = acc
        return h, acc

    u, a = lax.fori_loop(0, groups, body, (jnp.zeros((s8, R_WIDTH), F32), jnp.ones((s8, R_WIDTH), F32)))
    k = 1
    while k < s8:
        if rev:
            ok = sub < s8 - k
            a_s = pltpu.roll(a, s8 - k, axis=0)
            u_s = pltpu.roll(u, s8 - k, axis=0)
        else:
            ok = sub >= k
            a_s = pltpu.roll(a, k, axis=0)
            u_s = pltpu.roll(u, k, axis=0)
        u = u + a * jnp.where(ok, u_s, 0.0)
        a = a * jnp.where(ok, a_s, 1.0)
        k *= 2
    c0 = h_ref[d, 0:1, :]
    after = u + a * c0
    if rev:
        entry = jnp.where(sub == s8 - 1, c0, pltpu.roll(after, s8 - 1, axis=0))
        h_ref[d, 0:1, :] = after[0:1, :]
    else:
        entry = jnp.where(sub == 0, c0, pltpu.roll(after, 1, axis=0))
        h_ref[d, 0:1, :] = after[s8 - 1:s8, :]
    hs = u_ref[...] + a_ref[...] * jnp.concatenate([entry] * groups, axis=0)
    out_ref[...] = jnp.dot(pm_ref[1], hs.astype(BF16), preferred_element_type=F32).astype(out_ref.dtype)


def _rglru_kernel(cx_ref, xf_ref, pf_ref, nf_ref, xb_ref, pb_ref, nb_ref,
                  cw_ref, cb_ref, wd_ref, br_ref, lam_ref,
                  ycf_ref, ycb_ref, yf_ref, yb_ref,
                  pm_ref, a_ref, u_ref, h_ref, sp_ref, *, n_lat):
    s = pl.program_id(1)
    seg = LRU_BLK // SUBLANES

    @pl.when(s == 0)
    def _():
        h_ref[...] = jnp.zeros_like(h_ref)
        sp_ref[...] = (0.5 * LRU_C) * jax.nn.softplus(-lam_ref[...])
        row = lax.broadcasted_iota(jnp.int32, (LRU_BLK, LRU_BLK), 0)
        col = lax.broadcasted_iota(jnp.int32, (LRU_BLK, LRU_BLK), 1)
        pm_ref[0] = jnp.where(col == (row % SUBLANES) * seg + row // SUBLANES, 1.0, 0.0).astype(BF16)
        pm_ref[1] = jnp.where(row == (col % SUBLANES) * seg + col // SUBLANES, 1.0, 0.0).astype(BF16)
        zero_rows = jnp.zeros((2, R_WIDTH), F32)
        xc = _lru_conv(cx_ref[...], zero_rows[0:1, :], zero_rows, pm_ref, cw_ref, cb_ref)
        for d, out_ref in ((0, ycf_ref), (1, ycb_ref)):
            _lru_gates(d, xc, wd_ref, br_ref, sp_ref, a_ref, u_ref)
            _lru_scan(d, a_ref, u_ref, h_ref, pm_ref, out_ref, LRU_BLK)

    @pl.when(s > 0)
    def _():
        for d, x_ref, p_ref, n_ref, out_ref in ((0, xf_ref, pf_ref, nf_ref, yf_ref),
                                                (1, xb_ref, pb_ref, nb_ref, yb_ref)):
            j = (s - 1) if d == 0 else (n_lat - s)
            has_prev = (j > 0).astype(F32)
            has_next = (j < n_lat - 1).astype(F32)
            prev_row = p_ref[...].astype(F32)[BF16_ROWS - 1:BF16_ROWS, :] * has_prev
            next_rows = n_ref[...].astype(F32)[0:2, :] * has_next
            xc = _lru_conv(x_ref[...], prev_row, next_rows, pm_ref, cw_ref, cb_ref)
            _lru_gates(d, xc, wd_ref, br_ref, sp_ref, a_ref, u_ref)
            _lru_scan(d, a_ref, u_ref, h_ref, pm_ref, out_ref, LRU_BLK)


def _rglru(pc, px, conv_w, conv_b, wd, b_rg, lam, layer):
    _, bsz, t, _ = px.shape
    n_lat = t // LRU_BLK
    per_blk = LRU_BLK // BF16_ROWS
    n_halo = t // BF16_ROWS

    def jf(s):
        return jnp.maximum(s - 1, 0)

    def jb(s):
        return jnp.minimum(n_lat - s, n_lat - 1)

    def lat_specs(jfun):
        return [
            pl.BlockSpec((None, None, LRU_BLK, SEC), lambda b, s: (S16_XL, b, jfun(s), 0)),
            pl.BlockSpec((None, None, BF16_ROWS, SEC),
                         lambda b, s: (S16_XL, b, jnp.maximum(jfun(s) * per_blk - 1, 0), 0)),
            pl.BlockSpec((None, None, BF16_ROWS, SEC),
                         lambda b, s: (S16_XL, b, jnp.minimum((jfun(s) + 1) * per_blk, n_halo - 1), 0)),
        ]

    in_specs = [pl.BlockSpec((None, None, CTX_LEN, SEC), lambda b, s: (S16_XL, b, 0, 0))]
    in_specs += lat_specs(jf) + lat_specs(jb) + [
        pl.BlockSpec((None, CONV_W, SEC), lambda b, s: (layer, 0, 0)),
        pl.BlockSpec((None, 1, SEC), lambda b, s: (layer, 0, 0)),
        pl.BlockSpec((None,) + wd.shape[1:], lambda b, s: (layer, 0, 0, 0, 0, 0)),
        pl.BlockSpec((None,) + b_rg.shape[1:], lambda b, s: (layer, 0, 0, 0)),
        pl.BlockSpec((None,) + lam.shape[1:], lambda b, s: (layer, 0, 0)),
    ]
    out_specs = [
        pl.BlockSpec((None, CTX_LEN, R_WIDTH), lambda b, s: (b, 0, 0)),
        pl.BlockSpec((None, CTX_LEN, R_WIDTH), lambda b, s: (b, 0, 0)),
        pl.BlockSpec((None, LRU_BLK, R_WIDTH), lambda b, s: (b, jf(s), 0)),
        pl.BlockSpec((None, LRU_BLK, R_WIDTH), lambda b, s: (b, jb(s), 0)),
    ]
    out_shape = [
        jax.ShapeDtypeStruct((bsz, CTX_LEN, R_WIDTH), BF16),
        jax.ShapeDtypeStruct((bsz, CTX_LEN, R_WIDTH), BF16),
        jax.ShapeDtypeStruct((bsz, t, R_WIDTH), BF16),
        jax.ShapeDtypeStruct((bsz, t, R_WIDTH), BF16),
    ]
    return pl.pallas_call(
        functools.partial(_rglru_kernel, n_lat=n_lat),
        grid=(bsz, n_lat + 1),
        in_specs=in_specs,
        out_specs=out_specs,
        out_shape=out_shape,
        scratch_shapes=[
            pltpu.VMEM((2, LRU_BLK, LRU_BLK), BF16),
            pltpu.VMEM((LRU_BLK, R_WIDTH), F32),
            pltpu.VMEM((LRU_BLK, R_WIDTH), F32),
            pltpu.VMEM((2, SUBLANES, R_WIDTH), F32),
            pltpu.VMEM((2, R_WIDTH), F32),
        ],
        compiler_params=_cparams(("arbitrary", "arbitrary")),
        name="rglru",
    )(pc, px, px, px, px, px, px, conv_w, conv_b, wd, b_rg, lam)


def _out_proj_kernel(hf_ref, hb_ref, yf_ref, yb_ref, o_ref, zm_ref, zl_ref, x_ref, gt_ref, mg_ref, w_ref, fg_ref,
                     out_ref, *, final):
    hm = hf_ref[...] + hb_ref[...]
    parts = []
    for h in range(M_HEADS):
        hh = hm[:, h * M_DV:(h + 1) * M_DV]
        parts.append(hh * lax.rsqrt(jnp.mean(hh * hh, axis=-1, keepdims=True) + EPS))
    hn = jnp.concatenate(parts, axis=1) * mg_ref[...]
    ym = hn * _sigmoid_t(o_ref[...].astype(F32)) * _silu_t(zm_ref[...].astype(F32))
    yr = (yf_ref[...].astype(F32) + yb_ref[...].astype(F32)) * _silu_t(zl_ref[...].astype(F32))
    y = jnp.concatenate([ym, yr], axis=1).astype(BF16)
    xn = x_ref[...] + gt_ref[...] * jnp.dot(y, w_ref[...], preferred_element_type=F32)
    if final:
        xn = xn * lax.rsqrt(jnp.mean(xn * xn, axis=-1, keepdims=True) + EPS) * fg_ref[...]
    out_ref[...] = xn


def _out_proj(hf, hb, yf, yb, p16, x2d, mod_l, row_of_tile, m_norm_g, w_out, final_g, layer, tm, final):
    m, d = x2d.shape

    def tok(width):
        return pl.BlockSpec((tm, width), lambda i: (i, 0))

    def sec(k):
        return pl.BlockSpec((None, tm, SEC), lambda i: (k, i, 0))

    return pl.pallas_call(
        functools.partial(_out_proj_kernel, final=final),
        grid=(m // tm,),
        in_specs=[
            tok(M_WIDTH), tok(M_WIDTH), tok(R_WIDTH), tok(R_WIDTH),
            sec(S16_O), sec(S16_ZM), sec(S16_ZL),
            tok(d),
            pl.BlockSpec((None, None, 1, d), lambda i: (row_of_tile(i), 2, 0, 0)),
            pl.BlockSpec((None, 1, M_WIDTH), lambda i: (layer, 0, 0)),
            pl.BlockSpec((None, d, d), lambda i: (layer, 0, 0), pipeline_mode=pl.Buffered(1)),
            pl.BlockSpec((1, d), lambda i: (0, 0)),
        ],
        out_specs=tok(d),
        out_shape=jax.ShapeDtypeStruct((m, d), F32),
        compiler_params=_cparams(("arbitrary",)),
        name="out_proj",
    )(hf, hb, yf, yb, p16, p16, p16, x2d, mod_l, m_norm_g, w_out, final_g)


def _dense_gate_tiles(w_rg):
    depth = w_rg.shape[0]
    per = LRU_TILE // R_BLOCK
    w = w_rg.reshape(depth, 2, 2, R_WIDTH // LRU_TILE, per, R_BLOCK, R_BLOCK).astype(BF16)
    dense = jnp.zeros((depth, 2, 2, R_WIDTH // LRU_TILE, LRU_TILE, LRU_TILE), BF16)
    for p in range(per):
        blk = slice(p * R_BLOCK, (p + 1) * R_BLOCK)
        dense = dense.at[:, :, :, :, blk, blk].set(w[:, :, :, :, p])
    return dense


def kernel(x, c, ctx, c_ctx, w_mod, b_mod, norm_g, w_in, b_gate, conv_qk_w, conv_qk_b, m_norm_g, conv_r_w, conv_r_b,
           w_rg, b_rg, lru_lambda, w_out, final_g):
    bsz, t, d = x.shape
    depth = w_mod.shape[0]
    nh = M_HEADS

    w_main, w_gate = _win_cast(w_in, WA_SECS * SEC, 4 * nh)
    bg2 = jnp.zeros((depth, SUBLANES, GATE_W), F32).at[:, 0, :4 * nh].set(b_gate)
    w_out_b = _cast_bf16(w_out)
    wd = _dense_gate_tiles(0.5 * w_rg)
    norm_g3 = norm_g[:, None, :]
    m_norm_g3 = m_norm_g[:, None, :]
    conv_qk_b3 = conv_qk_b[:, None, :]
    conv_r_b3 = conv_r_b[:, None, :]
    b_rg4 = 0.5 * b_rg.reshape(depth, 2, 2, R_WIDTH)
    fg = final_g[None, :]

    cvec = jnp.concatenate([c, c_ctx[None, :], jnp.zeros((SUBLANES - bsz - 1, d), F32)], axis=0)
    mods = _modulation(cvec, w_mod, b_mod).reshape(depth, SUBLANES, 3, 1, d)

    tm_in = 1024
    tm_ctx = bsz * CTX_LEN
    tm_out = 512
    x2d = x.reshape(bsz * t, d)
    c2d = ctx.reshape(bsz * CTX_LEN, d)
    lat_row_in = lambda i: i // (t // tm_in)
    lat_row_out = lambda i: i // (t // tm_out)
    ctx_row = lambda i: bsz

    for l in range(depth):
        last = l == depth - 1
        mod_l = mods[l]
        px32, px16, gx = _in_proj(x2d, mod_l, lat_row_in, norm_g3, w_main, w_gate, l, tm_in)
        pc32, pc16, gc = _in_proj(c2d, mod_l, ctx_row, norm_g3, w_main, w_gate, l, tm_ctx)
        hcf, hcb, hf, hb = _mlstm(pc32.reshape(N_SEC32, bsz, CTX_LEN, SEC), gc.reshape(bsz, CTX_LEN, GATE_W),
                                  px32.reshape(N_SEC32, bsz, t, SEC), gx.reshape(bsz, t, GATE_W),
                                  conv_qk_w, conv_qk_b3, bg2, l)
        ycf, ycb, yf, yb = _rglru(pc16.reshape(N_SEC - N_SEC32, bsz, CTX_LEN, SEC),
                                  px16.reshape(N_SEC - N_SEC32, bsz, t, SEC),
                                  conv_r_w, conv_r_b3, wd, b_rg4, lru_lambda, l)
        x2d = _out_proj(hf.reshape(bsz * t, M_WIDTH), hb.reshape(bsz * t, M_WIDTH),
                        yf.reshape(bsz * t, R_WIDTH), yb.reshape(bsz * t, R_WIDTH),
                        px16, x2d, mod_l, lat_row_out, m_norm_g3, w_out_b, fg, l, tm_out, last)
        if not last:
            c2d = _out_proj(hcf.reshape(bsz * CTX_LEN, M_WIDTH), hcb.reshape(bsz * CTX_LEN, M_WIDTH),
                            ycf.reshape(bsz * CTX_LEN, R_WIDTH), ycb.reshape(bsz * CTX_LEN, R_WIDTH),
                            pc16, c2d, mod_l, ctx_row, m_norm_g3, w_out_b, fg, l, tm_out, False)
    return x2d.reshape(bsz, t, d)
```

```python
import functools

import jax
import jax.numpy as jnp
from jax import lax
from jax.experimental import pallas as pl
from jax.experimental.pallas import tpu as pltpu

D_MODEL = 2048
DEPTH = 4
CTX_LEN = 256
GRID_W = 64
M_WIDTH = 1024
R_WIDTH = 1024
M_HEADS = 4
M_DV = 256
M_DQK = 128
M_QK = 512
R_BLOCKS = 16
R_BLOCK = 64
CONV_W = 4
LRU_C = 8.0
EPS = 1e-6

LANES = 128
SUBLANES = 8
BF16_ROWS = 16
SEC = 1024
N_SEC = 6
N_SEC32 = 2
S32_QK, S32_V = 0, 1
S16_O, S16_ZM, S16_XL, S16_ZL = 0, 1, 2, 3
WA_SECS = 4
GATE_W = LANES
CTX_CHUNK = CTX_LEN
COL_GROUP = SUBLANES
LAT_CHUNK = COL_GROUP * GRID_W
LRU_BLK = 256
LRU_TILE = 256
VMEM_LIMIT = 56 * 1024 * 1024
LOG2E = 1.4426950408889634

F32 = jnp.float32
BF16 = jnp.bfloat16


def _cparams(sem):
    return pltpu.CompilerParams(dimension_semantics=sem, vmem_limit_bytes=VMEM_LIMIT)


def _silu_t(x):
    h = 0.5 * x
    return h + h * jnp.tanh(h)


def _dot_nt(a, b):
    return lax.dot_general(a, b, (((1,), (1,)), ((), ())), preferred_element_type=F32)


def _sigmoid_t(x):
    return 0.5 * jnp.tanh(0.5 * x) + 0.5


def _mod_kernel(c_ref, w_ref, b_ref, o_ref):
    c = c_ref[...]
    s = (c * jax.nn.sigmoid(c)).astype(BF16)
    o_ref[...] = jnp.dot(s, w_ref[...].astype(BF16), preferred_element_type=F32) + b_ref[...]


def _modulation(cvec, w_mod, b_mod):
    depth, d, n = w_mod.shape
    tn = 1024
    return pl.pallas_call(
        _mod_kernel,
        grid=(depth, n // tn),
        in_specs=[
            pl.BlockSpec((SUBLANES, d), lambda l, j: (0, 0)),
            pl.BlockSpec((None, d, tn), lambda l, j: (l, 0, j)),
            pl.BlockSpec((None, 1, tn), lambda l, j: (l, 0, j)),
        ],
        out_specs=pl.BlockSpec((None, SUBLANES, tn), lambda l, j: (l, 0, j)),
        out_shape=jax.ShapeDtypeStruct((depth, SUBLANES, n), F32),
        compiler_params=_cparams(("arbitrary", "arbitrary")),
        name="mod",
    )(cvec, w_mod, b_mod.reshape(depth, 1, n))


def _in_proj_kernel(x_ref, sh_ref, sc_ref, g_ref, w_ref, wg_ref, p32_ref, p16_ref, gate_ref, h_scr):
    n = pl.program_id(1)

    @pl.when(n == 0)
    def _():
        x = x_ref[...]
        ms = jnp.mean(x * x, axis=-1, keepdims=True)
        y = x * lax.rsqrt(ms + EPS) * g_ref[...]
        h = (y * (1.0 + sc_ref[...]) + sh_ref[...]).astype(BF16)
        h_scr[...] = h
        gate_ref[...] = _dot_nt(h, wg_ref[...])

    @pl.when(n < N_SEC32)
    def _():
        p32_ref[...] = _dot_nt(h_scr[...], w_ref[...])

    @pl.when(n >= N_SEC32)
    def _():
        p16_ref[...] = _dot_nt(h_scr[...], w_ref[...]).astype(BF16)


def _in_proj(x2d, mod_l, row_of_tile, norm_g, w_main, w_gate, layer, tm):
    m, d = x2d.shape
    return pl.pallas_call(
        _in_proj_kernel,
        grid=(m // tm, N_SEC),
        in_specs=[
            pl.BlockSpec((tm, d), lambda i, n: (i, 0)),
            pl.BlockSpec((None, None, 1, d), lambda i, n: (row_of_tile(i), 0, 0, 0)),
            pl.BlockSpec((None, None, 1, d), lambda i, n: (row_of_tile(i), 1, 0, 0)),
            pl.BlockSpec((None, 1, d), lambda i, n: (layer, 0, 0)),
            pl.BlockSpec((None, SEC, d), lambda i, n: (layer, n, 0)),
            pl.BlockSpec((None, GATE_W, d), lambda i, n: (layer, 0, 0)),
        ],
        out_specs=[
            pl.BlockSpec((None, tm, SEC), lambda i, n: (jnp.minimum(n, N_SEC32 - 1), i, 0)),
            pl.BlockSpec((None, tm, SEC), lambda i, n: (jnp.maximum(n - N_SEC32, 0), i, 0)),
            pl.BlockSpec((tm, GATE_W), lambda i, n: (i, 0)),
        ],
        out_shape=[
            jax.ShapeDtypeStruct((N_SEC32, m, SEC), F32),
            jax.ShapeDtypeStruct((N_SEC - N_SEC32, m, SEC), BF16),
            jax.ShapeDtypeStruct((m, GATE_W), F32),
        ],
        scratch_shapes=[pltpu.VMEM((tm, d), BF16)],
        compiler_params=_cparams(("arbitrary", "arbitrary")),
        name="in_proj",
    )(x2d, mod_l, mod_l, norm_g, w_main, w_gate)


def _win_cast_kernel(a_ref, b_ref, o_ref, g_ref, *, n_direct, skip):
    r = pl.program_id(1)
    tr = o_ref.shape[0]

    @pl.when(r < n_direct)
    def _():
        o_ref[...] = a_ref[...].astype(BF16)

    @pl.when(r >= n_direct)
    def _():
        o_ref[:tr - skip, :] = a_ref[skip:, :].astype(BF16)
        o_ref[tr - skip:, :] = b_ref[...].astype(BF16)

    @pl.when(r == n_direct)
    def _():
        g_ref[0:skip, :] = a_ref[0:skip, :].astype(BF16)
        g_ref[skip:, :] = jnp.zeros((g_ref.shape[0] - skip, g_ref.shape[1]), BF16)


def _win_cast(w_t, split, skip):
    depth, n_in, d = w_t.shape
    tr = 256
    return pl.pallas_call(
        functools.partial(_win_cast_kernel, n_direct=split // tr, skip=skip),
        grid=(depth, (n_in - skip) // tr),
        in_specs=[pl.BlockSpec((None, tr, d), lambda l, r: (l, r, 0)),
                  pl.BlockSpec((None, skip, d), lambda l, r: (l, (r + 1) * (tr // skip), 0))],
        out_specs=[pl.BlockSpec((None, tr, d), lambda l, r: (l, r, 0)),
                   pl.BlockSpec((None, GATE_W, d), lambda l, r: (l, 0, 0))],
        out_shape=[jax.ShapeDtypeStruct((depth, n_in - skip, d), BF16),
                   jax.ShapeDtypeStruct((depth, GATE_W, d), BF16)],
        compiler_params=_cparams(("arbitrary", "arbitrary")),
        name="win_cast",
    )(w_t, w_t)


def _cast_kernel(w_ref, o_ref):
    o_ref[...] = w_ref[...].astype(BF16)


def _cast_bf16(w):
    depth, r, c = w.shape
    tr = 512
    return pl.pallas_call(
        _cast_kernel,
        grid=(depth, r // tr),
        in_specs=[pl.BlockSpec((None, tr, c), lambda l, j: (l, j, 0))],
        out_specs=pl.BlockSpec((None, tr, c), lambda l, j: (l, j, 0)),
        out_shape=jax.ShapeDtypeStruct((depth, r, c), BF16),
        compiler_params=_cparams(("arbitrary", "arbitrary")),
        name="cast_bf16",
    )(w)


def _shifted(x, k, reverse, ident):
    n = x.shape[0]
    row = lax.broadcasted_iota(jnp.int32, x.shape, 0)
    if reverse:
        return jnp.where(row < n - k, pltpu.roll(x, n - k, axis=0), ident)
    return jnp.where(row >= k, pltpu.roll(x, k, axis=0), ident)


def _scan_rows(x, op, ident, reverse):
    n = x.shape[0]
    k = 1
    while k < n:
        x = op(x, _shifted(x, k, reverse, ident))
        k *= 2
    return x


def _scan_colmajor(x, op, ident, reverse):
    n = x.shape[0]
    k = SUBLANES
    while k < n:
        x = op(x, _shifted(x, k, reverse, ident))
        k *= 2
    tot = x[0:SUBLANES, :] if reverse else x[n - SUBLANES:n, :]
    tot = _scan_rows(tot, op, ident, reverse)
    tot = _shifted(tot, 1, reverse, ident)
    return op(x, jnp.concatenate([tot] * (n // SUBLANES), axis=0))


def _conv_rows(xs_ref, n, w_ref, b_ref, step):
    base = SUBLANES - step
    acc = b_ref[...] + w_ref[0:1, :] * xs_ref[base:base + n, :]
    for j in range(1, CONV_W):
        acc = acc + w_ref[j:j + 1, :] * xs_ref[base + j * step:base + j * step + n, :]
    return acc


def _mlstm_dir(d, qk, v, gates, bg_ref, ct_ref, m_ref, scan, posdiff, write_h):
    n = qk.shape[0]
    rev = d == 1
    gi = gates + bg_ref[0:1, :]
    lf = jax.nn.log_sigmoid(pltpu.roll(gi, LANES - M_HEADS, axis=1))
    bc = scan(lf, jnp.add, 0.0, rev)
    a = gi - bc
    m_prev = m_ref[d, 0:1, :]
    mm = jnp.maximum(scan(a, jnp.maximum, -jnp.inf, rev), m_prev)
    inter = jnp.exp(m_prev - mm)
    em = jnp.exp(-(bc + mm))
    last = 0 if rev else n - 1
    mm_last = mm[last:last + 1, :]
    m_new = bc[last:last + 1, :] + mm_last
    decay = jnp.exp(m_prev - mm_last)
    wcol = jnp.exp(a - mm_last)
    a_t = (a * LOG2E).T
    mm2 = mm * LOG2E
    mask = (posdiff <= 0) if rev else (posdiff >= 0)
    ones = jnp.ones((n, LANES), BF16)
    for h in range(M_HEADS):
        e = 2 * M_HEADS * d + h
        st = d * M_HEADS + h
        qf = qk[:, h * M_DQK:(h + 1) * M_DQK]
        q = qf.astype(BF16)
        kf = qk[:, M_QK + h * M_DQK:M_QK + (h + 1) * M_DQK] * (M_DQK ** -0.5)
        vaug = jnp.concatenate([v[:, h * M_DV:(h + 1) * M_DV].astype(BF16), ones], axis=1)
        dmat = jnp.where(mask, jnp.exp2(a_t[e:e + 1, :] - mm2[:, e:e + 1]), 0.0)
        s = lax.dot_general(q, kf.astype(BF16), (((1,), (1,)), ((), ())), preferred_element_type=F32)
        sw = (s * dmat).astype(BF16)
        ct = ct_ref[st]
        qi = (qf * inter[:, e:e + 1]).astype(BF16)
        num = jnp.dot(jnp.concatenate([sw, qi], axis=1), jnp.concatenate([vaug, ct.astype(BF16)], axis=0),
                      preferred_element_type=F32)
        den = jnp.maximum(jnp.abs(num[:, M_DV:]), em[:, e:e + 1])
        write_h(h, num[:, :M_DV] / jnp.concatenate([den, den], axis=1))
        kw = (kf * wcol[:, e:e + 1]).astype(BF16)
        upd = lax.dot_general(kw, vaug, (((0,), (0,)), ((), ())), preferred_element_type=F32)
        ct_ref[st] = decay[:, e:e + 1] * ct + upd
    m_ref[d, 0:1, :] = m_new


def _mlstm_kernel(cqk_ref, cv_ref, cg_ref,
                  qf_ref, pf_ref, nf_ref, vf_ref, gf_ref,
                  qb_ref, pb_ref, nb_ref, vb_ref, gb_ref,
                  cw_ref, cb_ref, bg_ref,
                  hcf_ref, hcb_ref, hf_ref, hb_ref,
                  xs_ref, ct_ref, m_ref, pd_ref, *, n_lat):
    s = pl.program_id(1)
    lc = LAT_CHUNK
    zero_rows = jnp.zeros((SUBLANES, 2 * M_QK), F32)

    @pl.when(s == 0)
    def _():
        ct_ref[...] = jnp.zeros_like(ct_ref)
        m_ref[...] = jnp.zeros_like(m_ref)
        row = lax.broadcasted_iota(jnp.int32, (lc, lc), 0)
        col = lax.broadcasted_iota(jnp.int32, (lc, lc), 1)
        pos_r = (row % COL_GROUP) * GRID_W + row // COL_GROUP
        pos_c = (col % COL_GROUP) * GRID_W + col // COL_GROUP
        pd_ref[...] = pos_r - pos_c
        xs_ref[0:SUBLANES, :] = zero_rows
        xs_ref[SUBLANES + CTX_CHUNK:2 * SUBLANES + CTX_CHUNK, :] = zero_rows
        xs_ref[SUBLANES:SUBLANES + CTX_CHUNK, :] = cqk_ref[...]
        qk = _silu_t(_conv_rows(xs_ref, CTX_CHUNK, cw_ref, cb_ref, 1))
        v = cv_ref[...]
        g = cg_ref[...]
        crow = lax.broadcasted_iota(jnp.int32, (CTX_CHUNK, CTX_CHUNK), 0)
        ccol = lax.broadcasted_iota(jnp.int32, (CTX_CHUNK, CTX_CHUNK), 1)
        for d, out_ref in ((0, hcf_ref), (1, hcb_ref)):
            def write_h(h, val, out_ref=out_ref):
                out_ref[:, h * M_DV:(h + 1) * M_DV] = val
            _mlstm_dir(d, qk, v, g, bg_ref, ct_ref, m_ref, _scan_rows, crow - ccol, write_h)

    @pl.when(s > 0)
    def _():
        sub = lax.broadcasted_iota(jnp.int32, (SUBLANES, 2 * M_QK), 0)
        for d, q_ref, p_ref, n_ref, v_ref, g_ref, out_ref in (
                (0, qf_ref, pf_ref, nf_ref, vf_ref, gf_ref, hf_ref),
                (1, qb_ref, pb_ref, nb_ref, vb_ref, gb_ref, hb_ref)):
            j = (s - 1) if d == 0 else (n_lat - s)
            has_prev = (j > 0).astype(F32)
            has_next = (j < n_lat - 1).astype(F32)
            x = q_ref[...].reshape(lc, 2 * M_QK)
            xs_ref[SUBLANES:SUBLANES + lc, :] = x
            x_last = x[lc - SUBLANES:lc, :]
            xs_ref[0:SUBLANES, :] = jnp.where(sub == 0, pltpu.roll(p_ref[...], 1, axis=0) * has_prev,
                                              pltpu.roll(x_last, 1, axis=0))
            for k in range(2):
                xs_ref[SUBLANES + lc + k * SUBLANES:2 * SUBLANES + lc + k * SUBLANES, :] = jnp.where(
                    sub == SUBLANES - 1, pltpu.roll(n_ref[k], SUBLANES - 1, axis=0) * has_next,
                    pltpu.roll(x[k * SUBLANES:(k + 1) * SUBLANES, :], SUBLANES - 1, axis=0))
            qk = _silu_t(_conv_rows(xs_ref, lc, cw_ref, cb_ref, SUBLANES))
            v = v_ref[...].reshape(lc, M_WIDTH)
            g = g_ref[...].reshape(lc, GATE_W)

            def write_h(h, val, out_ref=out_ref):
                out_ref[:, :, h * M_DV:(h + 1) * M_DV] = val.reshape(GRID_W, COL_GROUP, M_DV)
            _mlstm_dir(d, qk, v, g, bg_ref, ct_ref, m_ref, _scan_colmajor, pd_ref[...], write_h)


def _mlstm(pc, gc, px, gx, conv_w, conv_b, bg2, layer):
    _, bsz, t, _ = px.shape
    rows = t // GRID_W
    n_lat = GRID_W // COL_GROUP
    pxv = px.reshape(N_SEC32, bsz, rows, GRID_W, SEC)
    gxv = gx.reshape(bsz, rows, GRID_W, GATE_W)

    def jf(s):
        return jnp.maximum(s - 1, 0)

    def jb(s):
        return jnp.minimum(n_lat - s, n_lat - 1)

    def lat_specs(jfun):
        return [
            pl.BlockSpec((None, None, rows, COL_GROUP, SEC), lambda b, s: (S32_QK, b, 0, jfun(s), 0)),
            pl.BlockSpec((None, None, None, COL_GROUP, SEC),
                         lambda b, s: (S32_QK, b, rows - 1, jnp.maximum(jfun(s) - 1, 0), 0)),
            pl.BlockSpec((None, None, 2, COL_GROUP, SEC),
                         lambda b, s: (S32_QK, b, 0, jnp.minimum(jfun(s) + 1, n_lat - 1), 0)),
            pl.BlockSpec((None, None, rows, COL_GROUP, SEC), lambda b, s: (S32_V, b, 0, jfun(s), 0)),
            pl.BlockSpec((None, rows, COL_GROUP, GATE_W), lambda b, s: (b, 0, jfun(s), 0)),
        ]

    in_specs = [
        pl.BlockSpec((None, None, CTX_LEN, SEC), lambda b, s: (S32_QK, b, 0, 0)),
        pl.BlockSpec((None, None, CTX_LEN, SEC), lambda b, s: (S32_V, b, 0, 0)),
        pl.BlockSpec((None, CTX_LEN, GATE_W), lambda b, s: (b, 0, 0)),
    ] + lat_specs(jf) + lat_specs(jb) + [
        pl.BlockSpec((None, CONV_W, SEC), lambda b, s: (layer, 0, 0)),
        pl.BlockSpec((None, 1, SEC), lambda b, s: (layer, 0, 0)),
        pl.BlockSpec((None, SUBLANES, LANES), lambda b, s: (layer, 0, 0)),
    ]
    out_specs = [
        pl.BlockSpec((None, CTX_LEN, M_WIDTH), lambda b, s: (b, 0, 0)),
        pl.BlockSpec((None, CTX_LEN, M_WIDTH), lambda b, s: (b, 0, 0)),
        pl.BlockSpec((None, rows, COL_GROUP, M_WIDTH), lambda b, s: (b, 0, jf(s), 0)),
        pl.BlockSpec((None, rows, COL_GROUP, M_WIDTH), lambda b, s: (b, 0, jb(s), 0)),
    ]
    out_shape = [
        jax.ShapeDtypeStruct((bsz, CTX_LEN, M_WIDTH), F32),
        jax.ShapeDtypeStruct((bsz, CTX_LEN, M_WIDTH), F32),
        jax.ShapeDtypeStruct((bsz, rows, GRID_W, M_WIDTH), F32),
        jax.ShapeDtypeStruct((bsz, rows, GRID_W, M_WIDTH), F32),
    ]
    hcf, hcb, hf, hb = pl.pallas_call(
        functools.partial(_mlstm_kernel, n_lat=n_lat),
        grid=(bsz, n_lat + 1),
        in_specs=in_specs,
        out_specs=out_specs,
        out_shape=out_shape,
        scratch_shapes=[
            pltpu.VMEM((LAT_CHUNK + 3 * SUBLANES, 2 * M_QK), F32),
            pltpu.VMEM((2 * M_HEADS, M_DQK, M_DV + LANES), F32),
            pltpu.VMEM((2, SUBLANES, LANES), F32),
            pltpu.VMEM((LAT_CHUNK, LAT_CHUNK), jnp.int32),
        ],
        compiler_params=_cparams(("arbitrary", "arbitrary")),
        name="mlstm",
    )(pc, pc, gc, pxv, pxv, pxv, pxv, gxv, pxv, pxv, pxv, pxv, gxv, conv_w, conv_b, bg2)
    return hcf, hcb, hf.reshape(bsz, t, M_WIDTH), hb.reshape(bsz, t, M_WIDTH)


def _lru_conv(x_bf, prev_row, next_rows, pm_ref, cw_ref, cb_ref):
    n = x_bf.shape[0]
    s8 = SUBLANES
    xp = jnp.dot(pm_ref[0], x_bf, preferred_element_type=F32)
    sub = lax.broadcasted_iota(jnp.int32, (s8, x_bf.shape[1]), 0)

    def from_next_segment(v, fill):
        return jnp.where(sub == s8 - 1, fill, pltpu.roll(v, s8 - 1, axis=0))

    def from_prev_segment(v, fill):
        return jnp.where(sub == 0, fill, pltpu.roll(v, 1, axis=0))

    first, second, last = xp[0:s8, :], xp[s8:2 * s8, :], xp[n - s8:, :]
    xm1 = jnp.concatenate([from_prev_segment(last, prev_row), xp[:n - s8, :]], axis=0)
    xp1 = jnp.concatenate([xp[s8:, :], from_next_segment(first, next_rows[0:1, :])], axis=0)
    xp2 = jnp.concatenate([xp[2 * s8:, :], from_next_segment(first, next_rows[0:1, :]),
                           from_next_segment(second, next_rows[1:2, :])], axis=0)
    return (cb_ref[...] + cw_ref[0:1, :] * xm1 + cw_ref[1:2, :] * xp
            + cw_ref[2:3, :] * xp1 + cw_ref[3:4, :] * xp2)


def _lru_gates(d, xc, wd_ref, br_ref, sp_ref, a_ref, u_ref):
    xb = xc.astype(BF16)
    for j in range(R_WIDTH // LRU_TILE):
        sl = slice(j * LRU_TILE, (j + 1) * LRU_TILE)
        xj = xb[:, sl]
        tr = jnp.tanh(jnp.dot(xj, wd_ref[d, 0, j], preferred_element_type=F32) + br_ref[d, 0:1, sl])
        ti = jnp.tanh(jnp.dot(xj, wd_ref[d, 1, j], preferred_element_type=F32) + br_ref[d, 1:2, sl])
        sp = sp_ref[d:d + 1, sl]
        nla = tr * sp + sp
        a = jnp.exp2(nla * (-LOG2E))
        a_ref[:, sl] = a
        xh = 0.5 * xc[:, sl]
        z = jnp.tanh(nla) * (a * a + 1.0)
        root = jnp.where(z > 0.0, z * lax.rsqrt(z), 0.0)
        u_ref[:, sl] = root * (ti * xh + xh)


def _lru_scan(d, a_ref, u_ref, h_ref, pm_ref, out_ref, n):
    rev = d == 1
    s8 = SUBLANES
    groups = n // s8
    sub = lax.broadcasted_iota(jnp.int32, (s8, R_WIDTH), 0)

    def body(g, carry):
        h, acc = carry
        gg = (groups - 1 - g) if rev else g
        r0 = pl.multiple_of(gg * s8, s8)
        a = a_ref[pl.ds(r0, s8), :]
        h = a * h + u_ref[pl.ds(r0, s8), :]
        acc = a * acc
        u_ref[pl.ds(r0, s8), :] = h
        a_ref[pl.ds(r0, s8), :] = acc
        return h, acc

    u, a = lax.fori_loop(0, groups, body, (jnp.zeros((s8, R_WIDTH), F32), jnp.ones((s8, R_WIDTH), F32)))
    k = 1
    while k < s8:
        if rev:
            ok = sub < s8 - k
            a_s = pltpu.roll(a, s8 - k, axis=0)
            u_s = pltpu.roll(u, s8 - k, axis=0)
        else:
            ok = sub >= k
            a_s = pltpu.roll(a, k, axis=0)
            u_s = pltpu.roll(u, k, axis=0)
        u = u + a * jnp.where(ok, u_s, 0.0)
        a = a * jnp.where(ok, a_s, 1.0)
        k *= 2
    c0 = h_ref[d, 0:1, :]
    after = u + a * c0
    if rev:
        entry = jnp.where(sub == s8 - 1, c0, pltpu.roll(after, s8 - 1, axis=0))
        h_ref[d, 0:1, :] = after[0:1, :]
    else:
        entry = jnp.where(sub == 0, c0, pltpu.roll(after, 1, axis=0))
        h_ref[d, 0:1, :] = after[s8 - 1:s8, :]
    hs = u_ref[...] + a_ref[...] * jnp.concatenate([entry] * groups, axis=0)
    out_ref[...] = jnp.dot(pm_ref[1], hs.astype(BF16), preferred_element_type=F32).astype(out_ref.dtype)


def _rglru_kernel(cx_ref, xf_ref, pf_ref, nf_ref, xb_ref, pb_ref, nb_ref,
                  cw_ref, cb_ref, wr_ref, br_ref, lam_ref,
                  ycf_ref, ycb_ref, yf_ref, yb_ref,
                  pm_ref, a_ref, u_ref, h_ref, sp_ref, wd_ref, *, n_lat):
    s = pl.program_id(1)
    seg = LRU_BLK // SUBLANES

    @pl.when(s == 0)
    def _():
        h_ref[...] = jnp.zeros_like(h_ref)
        sp_ref[...] = (0.5 * LRU_C) * jax.nn.softplus(-lam_ref[...])
        wd_ref[...] = jnp.zeros_like(wd_ref)
        per = LRU_TILE // R_BLOCK
        for dd in range(2):
            for g in range(2):
                for blk in range(R_BLOCKS):
                    j, p = divmod(blk, per)
                    rows = slice(p * R_BLOCK, (p + 1) * R_BLOCK)
                    wd_ref[dd, g, j, rows, rows] = (0.5 * wr_ref[dd, g, blk]).astype(BF16)
        row = lax.broadcasted_iota(jnp.int32, (LRU_BLK, LRU_BLK), 0)
        col = lax.broadcasted_iota(jnp.int32, (LRU_BLK, LRU_BLK), 1)
        pm_ref[0] = jnp.where(col == (row % SUBLANES) * seg + row // SUBLANES, 1.0, 0.0).astype(BF16)
        pm_ref[1] = jnp.where(row == (col % SUBLANES) * seg + col // SUBLANES, 1.0, 0.0).astype(BF16)
        zero_rows = jnp.zeros((2, R_WIDTH), F32)
        xc = _lru_conv(cx_ref[...], zero_rows[0:1, :], zero_rows, pm_ref, cw_ref, cb_ref)
        for d, out_ref in ((0, ycf_ref), (1, ycb_ref)):
            _lru_gates(d, xc, wd_ref, br_ref, sp_ref, a_ref, u_ref)
            _lru_scan(d, a_ref, u_ref, h_ref, pm_ref, out_ref, LRU_BLK)

    @pl.when(s > 0)
    def _():
        for d, x_ref, p_ref, n_ref, out_ref in ((0, xf_ref, pf_ref, nf_ref, yf_ref),
                                                (1, xb_ref, pb_ref, nb_ref, yb_ref)):
            j = (s - 1) if d == 0 else (n_lat - s)
            has_prev = (j > 0).astype(F32)
            has_next = (j < n_lat - 1).astype(F32)
            prev_row = p_ref[...].astype(F32)[BF16_ROWS - 1:BF16_ROWS, :] * has_prev
            next_rows = n_ref[...].astype(F32)[0:2, :] * has_next
            xc = _lru_conv(x_ref[...], prev_row, next_rows, pm_ref, cw_ref, cb_ref)
            _lru_gates(d, xc, wd_ref, br_ref, sp_ref, a_ref, u_ref)
            _lru_scan(d, a_ref, u_ref, h_ref, pm_ref, out_ref, LRU_BLK)


def _rglru(pc, px, conv_w, conv_b, w_rg, b_rg, lam, layer):
    _, bsz, t, _ = px.shape
    n_lat = t // LRU_BLK
    per_blk = LRU_BLK // BF16_ROWS
    n_halo = t // BF16_ROWS

    def jf(s):
        return jnp.maximum(s - 1, 0)

    def jb(s):
        return jnp.minimum(n_lat - s, n_lat - 1)

    def lat_specs(jfun):
        return [
            pl.BlockSpec((None, None, LRU_BLK, SEC), lambda b, s: (S16_XL, b, jfun(s), 0)),
            pl.BlockSpec((None, None, BF16_ROWS, SEC),
                         lambda b, s: (S16_XL, b, jnp.maximum(jfun(s) * per_blk - 1, 0), 0)),
            pl.BlockSpec((None, None, BF16_ROWS, SEC),
                         lambda b, s: (S16_XL, b, jnp.minimum((jfun(s) + 1) * per_blk, n_halo - 1), 0)),
        ]

    in_specs = [pl.BlockSpec((None, None, CTX_LEN, SEC), lambda b, s: (S16_XL, b, 0, 0))]
    in_specs += lat_specs(jf) + lat_specs(jb) + [
        pl.BlockSpec((None, CONV_W, SEC), lambda b, s: (layer, 0, 0)),
        pl.BlockSpec((None, 1, SEC), lambda b, s: (layer, 0, 0)),
        pl.BlockSpec((None,) + w_rg.shape[1:], lambda b, s: (layer, 0, 0, 0, 0, 0)),
        pl.BlockSpec((None,) + b_rg.shape[1:], lambda b, s: (layer, 0, 0, 0)),
        pl.BlockSpec((None,) + lam.shape[1:], lambda b, s: (layer, 0, 0)),
    ]
    out_specs = [
        pl.BlockSpec((None, CTX_LEN, R_WIDTH), lambda b, s: (b, 0, 0)),
        pl.BlockSpec((None, CTX_LEN, R_WIDTH), lambda b, s: (b, 0, 0)),
        pl.BlockSpec((None, LRU_BLK, R_WIDTH), lambda b, s: (b, jf(s), 0)),
        pl.BlockSpec((None, LRU_BLK, R_WIDTH), lambda b, s: (b, jb(s), 0)),
    ]
    out_shape = [
        jax.ShapeDtypeStruct((bsz, CTX_LEN, R_WIDTH), BF16),
        jax.ShapeDtypeStruct((bsz, CTX_LEN, R_WIDTH), BF16),
        jax.ShapeDtypeStruct((bsz, t, R_WIDTH), BF16),
        jax.ShapeDtypeStruct((bsz, t, R_WIDTH), BF16),
    ]
    return pl.pallas_call(
        functools.partial(_rglru_kernel, n_lat=n_lat),
        grid=(bsz, n_lat + 1),
        in_specs=in_specs,
        out_specs=out_specs,
        out_shape=out_shape,
        scratch_shapes=[
            pltpu.VMEM((2, LRU_BLK, LRU_BLK), BF16),
            pltpu.VMEM((LRU_BLK, R_WIDTH), F32),
            pltpu.VMEM((LRU_BLK, R_WIDTH), F32),
            pltpu.VMEM((2, SUBLANES, R_WIDTH), F32),
            pltpu.VMEM((2, R_WIDTH), F32),
            pltpu.VMEM((2, 2, R_WIDTH // LRU_TILE, LRU_TILE, LRU_TILE), BF16),
        ],
        compiler_params=_cparams(("arbitrary", "arbitrary")),
        name="rglru",
    )(pc, px, px, px, px, px, px, conv_w, conv_b, w_rg, b_rg, lam)


def _out_proj_kernel(hf_ref, hb_ref, yf_ref, yb_ref, o_ref, zm_ref, zl_ref, x_ref, gt_ref, mg_ref, w_ref, fg_ref,
                     out_ref, *, final):
    hm = hf_ref[...] + hb_ref[...]
    parts = []
    for h in range(M_HEADS):
        hh = hm[:, h * M_DV:(h + 1) * M_DV]
        parts.append(hh * lax.rsqrt(jnp.mean(hh * hh, axis=-1, keepdims=True) + EPS))
    hn = jnp.concatenate(parts, axis=1) * mg_ref[...]
    ym = hn * _sigmoid_t(o_ref[...].astype(F32)) * _silu_t(zm_ref[...].astype(F32))
    yr = (yf_ref[...].astype(F32) + yb_ref[...].astype(F32)) * _silu_t(zl_ref[...].astype(F32))
    y = jnp.concatenate([ym, yr], axis=1).astype(BF16)
    xn = x_ref[...] + gt_ref[...] * jnp.dot(y, w_ref[...], preferred_element_type=F32)
    if final:
        xn = xn * lax.rsqrt(jnp.mean(xn * xn, axis=-1, keepdims=True) + EPS) * fg_ref[...]
    out_ref[...] = xn


def _out_proj(hf, hb, yf, yb, p16, x2d, mod_l, row_of_tile, m_norm_g, w_out, final_g, layer, tm, final):
    m, d = x2d.shape

    def tok(width):
        return pl.BlockSpec((tm, width), lambda i: (i, 0))

    def sec(k):
        return pl.BlockSpec((None, tm, SEC), lambda i: (k, i, 0))

    return pl.pallas_call(
        functools.partial(_out_proj_kernel, final=final),
        grid=(m // tm,),
        in_specs=[
            tok(M_WIDTH), tok(M_WIDTH), tok(R_WIDTH), tok(R_WIDTH),
            sec(S16_O), sec(S16_ZM), sec(S16_ZL),
            tok(d),
            pl.BlockSpec((None, None, 1, d), lambda i: (row_of_tile(i), 2, 0, 0)),
            pl.BlockSpec((None, 1, M_WIDTH), lambda i: (layer, 0, 0)),
            pl.BlockSpec((None, d, d), lambda i: (layer, 0, 0), pipeline_mode=pl.Buffered(1)),
            pl.BlockSpec((1, d), lambda i: (0, 0)),
        ],
        out_specs=tok(d),
        out_shape=jax.ShapeDtypeStruct((m, d), F32),
        compiler_params=_cparams(("arbitrary",)),
        name="out_proj",
    )(hf, hb, yf, yb, p16, p16, p16, x2d, mod_l, m_norm_g, w_out, final_g)


def kernel(x, c, ctx, c_ctx, w_mod, b_mod, norm_g, w_in, b_gate, conv_qk_w, conv_qk_b, m_norm_g, conv_r_w, conv_r_b,
           w_rg, b_rg, lru_lambda, w_out, final_g):
    bsz, t, d = x.shape
    depth = w_mod.shape[0]
    nh = M_HEADS

    w_main, w_gate = _win_cast(jnp.swapaxes(w_in, 1, 2), WA_SECS * SEC, 4 * nh)
    bg2 = jnp.zeros((depth, SUBLANES, GATE_W), F32).at[:, 0, :4 * nh].set(b_gate)
    w_out_b = _cast_bf16(w_out)
    norm_g3 = norm_g[:, None, :]
    m_norm_g3 = m_norm_g[:, None, :]
    conv_qk_b3 = conv_qk_b[:, None, :]
    conv_r_b3 = conv_r_b[:, None, :]
    b_rg4 = 0.5 * b_rg.reshape(depth, 2, 2, R_WIDTH)
    fg = final_g[None, :]

    cvec = jnp.concatenate([c, c_ctx[None, :], jnp.zeros((SUBLANES - bsz - 1, d), F32)], axis=0)
    mods = _modulation(cvec, w_mod, b_mod).reshape(depth, SUBLANES, 3, 1, d)

    tm_in = 1024
    tm_ctx = bsz * CTX_LEN
    tm_out = 512
    x2d = x.reshape(bsz * t, d)
    c2d = ctx.reshape(bsz * CTX_LEN, d)
    lat_row_in = lambda i: i // (t // tm_in)
    lat_row_out = lambda i: i // (t // tm_out)
    ctx_row = lambda i: bsz

    for l in range(depth):
        last = l == depth - 1
        mod_l = mods[l]
        px32, px16, gx = _in_proj(x2d, mod_l, lat_row_in, norm_g3, w_main, w_gate, l, tm_in)
        pc32, pc16, gc = _in_proj(c2d, mod_l, ctx_row, norm_g3, w_main, w_gate, l, tm_ctx)
        hcf, hcb, hf, hb = _mlstm(pc32.reshape(N_SEC32, bsz, CTX_LEN, SEC), gc.reshape(bsz, CTX_LEN, GATE_W),
                                  px32.reshape(N_SEC32, bsz, t, SEC), gx.reshape(bsz, t, GATE_W),
                                  conv_qk_w, conv_qk_b3, bg2, l)
        ycf, ycb, yf, yb = _rglru(pc16.reshape(N_SEC - N_SEC32, bsz, CTX_LEN, SEC),
                                  px16.reshape(N_SEC - N_SEC32, bsz, t, SEC),
                                  conv_r_w, conv_r_b3, w_rg, b_rg4, lru_lambda, l)
        x2d = _out_proj(hf.reshape(bsz * t, M_WIDTH), hb.reshape(bsz * t, M_WIDTH),
                        yf.reshape(bsz * t, R_WIDTH), yb.reshape(bsz * t, R_WIDTH),
                        px16, x2d, mod_l, lat_row_out, m_norm_g3, w_out_b, fg, l, tm_out, last)
        if not last:
            c2d = _out_proj(hcf.reshape(bsz * CTX_LEN, M_WIDTH), hcb.reshape(bsz * CTX_LEN, M_WIDTH),
                            ycf.reshape(bsz * CTX_LEN, R_WIDTH), ycb.reshape(bsz * CTX_LEN, R_WIDTH),
                            pc16, c2d, mod_l, ctx_row, m_norm_g3, w_out_b, fg, l, tm_out, False)
    return x2d.reshape(bsz, t, d)
```

```python
import functools

import jax
import jax.numpy as jnp
from jax import lax
from jax.experimental import pallas as pl
from jax.experimental.pallas import tpu as pltpu

D_MODEL = 2048
DEPTH = 4
CTX_LEN = 256
GRID_W = 64
M_WIDTH = 1024
R_WIDTH = 1024
M_HEADS = 4
M_DV = 256
M_DQK = 128
M_QK = 512
R_BLOCKS = 16
R_BLOCK = 64
CONV_W = 4
LRU_C = 8.0
EPS = 1e-6

LANES = 128
SUBLANES = 8
BF16_ROWS = 16
SEC = 1024
N_SEC = 6
N_SEC32 = 2
S32_QK, S32_V = 0, 1
S16_O, S16_ZM, S16_XL, S16_ZL = 0, 1, 2, 3
WA_SECS = 4
GATE_W = LANES
CTX_CHUNK = CTX_LEN
COL_GROUP = SUBLANES
LAT_CHUNK = COL_GROUP * GRID_W
LRU_BLK = 256
LRU_PER_STEP = 2
LRU_TILE = 256
VMEM_LIMIT = 56 * 1024 * 1024
LOG2E = 1.4426950408889634

F32 = jnp.float32
BF16 = jnp.bfloat16


def _cparams(sem):
    return pltpu.CompilerParams(dimension_semantics=sem, vmem_limit_bytes=VMEM_LIMIT)


def _silu_t(x):
    h = 0.5 * x
    return h + h * jnp.tanh(h)


def _dot_nt(a, b):
    return lax.dot_general(a, b, (((1,), (1,)), ((), ())), preferred_element_type=F32)


def _sigmoid_t(x):
    return 0.5 * jnp.tanh(0.5 * x) + 0.5


def _mod_kernel(c_ref, w_ref, b_ref, o_ref):
    c = c_ref[...]
    s = (c * jax.nn.sigmoid(c)).astype(BF16)
    o_ref[...] = jnp.dot(s, w_ref[...].astype(BF16), preferred_element_type=F32) + b_ref[...]


def _modulation(cvec, w_mod, b_mod3, layer):
    _, d, n = w_mod.shape
    tn = 1024
    return pl.pallas_call(
        _mod_kernel,
        grid=(n // tn,),
        in_specs=[
            pl.BlockSpec((SUBLANES, d), lambda j: (0, 0)),
            pl.BlockSpec((None, d, tn), lambda j: (layer, 0, j)),
            pl.BlockSpec((None, 1, tn), lambda j: (layer, 0, j)),
        ],
        out_specs=pl.BlockSpec((SUBLANES, tn), lambda j: (0, j)),
        out_shape=jax.ShapeDtypeStruct((SUBLANES, n), F32),
        compiler_params=_cparams(("arbitrary",)),
        name="mod",
    )(cvec, w_mod, b_mod3)


def _cast_rows(r, a_ref, b_ref, o_ref, g_ref, n_direct, skip):
    tr = o_ref.shape[0]

    @pl.when(r < n_direct)
    def _():
        o_ref[...] = a_ref[...].astype(BF16)

    @pl.when(r >= n_direct)
    def _():
        o_ref[:tr - skip, :] = a_ref[skip:, :].astype(BF16)
        o_ref[tr - skip:, :] = b_ref[...].astype(BF16)

    @pl.when(r == n_direct)
    def _():
        g_ref[0:skip, :] = a_ref[0:skip, :].astype(BF16)
        g_ref[skip:, :] = jnp.zeros((g_ref.shape[0] - skip, g_ref.shape[1]), BF16)


def _in_proj_kernel(x_ref, sh_ref, sc_ref, g_ref, w_ref, wg_ref, *refs, prep):
    n = pl.program_id(1)
    if prep is None:
        p32_ref, p16_ref, gate_ref, h_scr = refs
    else:
        c_ref, wm_ref, bm_ref, wa_ref, wb_ref, p32_ref, p16_ref, gate_ref, modn_ref, wn_ref, wgn_ref, h_scr = refs
        n_direct, skip = prep
        c = c_ref[...]
        modn_ref[...] = jnp.dot((c * jax.nn.sigmoid(c)).astype(BF16), wm_ref[...].astype(BF16),
                                preferred_element_type=F32) + bm_ref[...]
        _cast_rows(pl.program_id(0) * N_SEC + n, wa_ref, wb_ref, wn_ref, wgn_ref, n_direct, skip)

    @pl.when(n == 0)
    def _():
        x = x_ref[...]
        ms = jnp.mean(x * x, axis=-1, keepdims=True)
        y = x * lax.rsqrt(ms + EPS) * g_ref[...]
        h = (y * (1.0 + sc_ref[...]) + sh_ref[...]).astype(BF16)
        h_scr[...] = h
        gate_ref[...] = _dot_nt(h, wg_ref[...])

    @pl.when(n < N_SEC32)
    def _():
        p32_ref[...] = _dot_nt(h_scr[...], w_ref[...])

    @pl.when(n >= N_SEC32)
    def _():
        p16_ref[...] = _dot_nt(h_scr[...], w_ref[...]).astype(BF16)


def _in_proj(x2d, mod_l, row_of_tile, norm_g, w_main, w_gate, layer, tm, nxt=None):
    m, d = x2d.shape
    n_tiles = m // tm
    in_specs = [
        pl.BlockSpec((tm, d), lambda i, n: (i, 0)),
        pl.BlockSpec((None, None, 1, d), lambda i, n: (row_of_tile(i), 0, 0, 0)),
        pl.BlockSpec((None, None, 1, d), lambda i, n: (row_of_tile(i), 1, 0, 0)),
        pl.BlockSpec((None, 1, d), lambda i, n: (layer, 0, 0)),
        pl.BlockSpec((SEC, d), lambda i, n: (n, 0)),
        pl.BlockSpec((GATE_W, d), lambda i, n: (0, 0)),
    ]
    out_specs = [
        pl.BlockSpec((None, tm, SEC), lambda i, n: (jnp.minimum(n, N_SEC32 - 1), i, 0)),
        pl.BlockSpec((None, tm, SEC), lambda i, n: (jnp.maximum(n - N_SEC32, 0), i, 0)),
        pl.BlockSpec((tm, GATE_W), lambda i, n: (i, 0)),
    ]
    out_shape = [
        jax.ShapeDtypeStruct((N_SEC32, m, SEC), F32),
        jax.ShapeDtypeStruct((N_SEC - N_SEC32, m, SEC), BF16),
        jax.ShapeDtypeStruct((m, GATE_W), F32),
    ]
    operands = [x2d, mod_l, mod_l, norm_g, w_main, w_gate]
    prep = None
    if nxt is not None:
        cvec, w_mod, b_mod3, w_t, skip = nxt
        steps = n_tiles * N_SEC
        n_mod = w_mod.shape[2]
        tr, tn = (N_SEC * SEC) // steps, n_mod // steps
        assert tr * steps == N_SEC * SEC and tr % skip == 0 and tn * steps == n_mod and tn % LANES == 0
        prep = (WA_SECS * SEC // tr, skip)

        def step(i, n):
            return i * N_SEC + n

        in_specs += [
            pl.BlockSpec((SUBLANES, d), lambda i, n: (0, 0)),
            pl.BlockSpec((None, d, tn), lambda i, n: (layer + 1, 0, step(i, n))),
            pl.BlockSpec((None, 1, tn), lambda i, n: (layer + 1, 0, step(i, n))),
            pl.BlockSpec((None, tr, d), lambda i, n: (layer + 1, step(i, n), 0)),
            pl.BlockSpec((None, skip, d), lambda i, n: (layer + 1, (step(i, n) + 1) * (tr // skip), 0)),
        ]
        out_specs += [
            pl.BlockSpec((SUBLANES, tn), lambda i, n: (0, step(i, n))),
            pl.BlockSpec((tr, d), lambda i, n: (step(i, n), 0)),
            pl.BlockSpec((GATE_W, d), lambda i, n: (0, 0)),
        ]
        out_shape += [
            jax.ShapeDtypeStruct((SUBLANES, n_mod), F32),
            jax.ShapeDtypeStruct((N_SEC * SEC, d), BF16),
            jax.ShapeDtypeStruct((GATE_W, d), BF16),
        ]
        operands += [cvec, w_mod, b_mod3, w_t, w_t]
    return pl.pallas_call(
        functools.partial(_in_proj_kernel, prep=prep),
        grid=(n_tiles, N_SEC),
        in_specs=in_specs,
        out_specs=out_specs,
        out_shape=out_shape,
        scratch_shapes=[pltpu.VMEM((tm, d), BF16)],
        compiler_params=_cparams(("arbitrary", "arbitrary")),
        name="in_proj",
    )(*operands)


def _win_cast_kernel(a_ref, b_ref, o_ref, g_ref, *, n_direct, skip):
    _cast_rows(pl.program_id(0), a_ref, b_ref, o_ref, g_ref, n_direct, skip)


def _win_cast(w_t, layer, split, skip):
    _, n_in, d = w_t.shape
    tr = 256
    return pl.pallas_call(
        functools.partial(_win_cast_kernel, n_direct=split // tr, skip=skip),
        grid=((n_in - skip) // tr,),
        in_specs=[pl.BlockSpec((None, tr, d), lambda r: (layer, r, 0)),
                  pl.BlockSpec((None, skip, d), lambda r: (layer, (r + 1) * (tr // skip), 0))],
        out_specs=[pl.BlockSpec((tr, d), lambda r: (r, 0)),
                   pl.BlockSpec((GATE_W, d), lambda r: (0, 0))],
        out_shape=[jax.ShapeDtypeStruct((n_in - skip, d), BF16),
                   jax.ShapeDtypeStruct((GATE_W, d), BF16)],
        compiler_params=_cparams(("arbitrary",)),
        name="win_cast",
    )(w_t, w_t)


def _cast_kernel(w_ref, o_ref):
    o_ref[...] = w_ref[...].astype(BF16)


def _cast_bf16(w):
    depth, r, c = w.shape
    tr = 512
    return pl.pallas_call(
        _cast_kernel,
        grid=(depth, r // tr),
        in_specs=[pl.BlockSpec((None, tr, c), lambda l, j: (l, j, 0))],
        out_specs=pl.BlockSpec((None, tr, c), lambda l, j: (l, j, 0)),
        out_shape=jax.ShapeDtypeStruct((depth, r, c), BF16),
        compiler_params=_cparams(("arbitrary", "arbitrary")),
        name="cast_bf16",
    )(w)


def _shifted(x, k, reverse, ident):
    n = x.shape[0]
    row = lax.broadcasted_iota(jnp.int32, x.shape, 0)
    if reverse:
        return jnp.where(row < n - k, pltpu.roll(x, n - k, axis=0), ident)
    return jnp.where(row >= k, pltpu.roll(x, k, axis=0), ident)


def _scan_rows(x, op, ident, reverse):
    n = x.shape[0]
    k = 1
    while k < n:
        x = op(x, _shifted(x, k, reverse, ident))
        k *= 2
    return x


def _scan_colmajor(x, op, ident, reverse):
    n = x.shape[0]
    k = SUBLANES
    while k < n:
        x = op(x, _shifted(x, k, reverse, ident))
        k *= 2
    tot = x[0:SUBLANES, :] if reverse else x[n - SUBLANES:n, :]
    tot = _scan_rows(tot, op, ident, reverse)
    tot = _shifted(tot, 1, reverse, ident)
    return op(x, jnp.concatenate([tot] * (n // SUBLANES), axis=0))


def _conv_rows(xs_ref, n, w_ref, b_ref, step):
    base = SUBLANES - step
    acc = b_ref[...] + w_ref[0:1, :] * xs_ref[base:base + n, :]
    for j in range(1, CONV_W):
        acc = acc + w_ref[j:j + 1, :] * xs_ref[base + j * step:base + j * step + n, :]
    return acc


def _mlstm_dir(d, qk, v, gates, bg_ref, ct_ref, m_ref, scan, posdiff, write_h):
    n = qk.shape[0]
    rev = d == 1
    gi = gates + bg_ref[0:1, :]
    lf = jax.nn.log_sigmoid(pltpu.roll(gi, LANES - M_HEADS, axis=1))
    bc = scan(lf, jnp.add, 0.0, rev)
    a = gi - bc
    m_prev = m_ref[d, 0:1, :]
    mm = jnp.maximum(scan(a, jnp.maximum, -jnp.inf, rev), m_prev)
    inter = jnp.exp(m_prev - mm)
    em = jnp.exp(-(bc + mm))
    last = 0 if rev else n - 1
    mm_last = mm[last:last + 1, :]
    m_new = bc[last:last + 1, :] + mm_last
    decay = jnp.exp(m_prev - mm_last)
    wcol = jnp.exp(a - mm_last)
    a_t = (a * LOG2E).T
    mm2 = mm * LOG2E
    mask = (posdiff <= 0) if rev else (posdiff >= 0)
    ones = jnp.ones((n, LANES), BF16)
    for h in range(M_HEADS):
        e = 2 * M_HEADS * d + h
        st = d * M_HEADS + h
        qf = qk[:, h * M_DQK:(h + 1) * M_DQK]
        q = qf.astype(BF16)
        kf = qk[:, M_QK + h * M_DQK:M_QK + (h + 1) * M_DQK] * (M_DQK ** -0.5)
        vaug = jnp.concatenate([v[:, h * M_DV:(h + 1) * M_DV].astype(BF16), ones], axis=1)
        dmat = jnp.where(mask, jnp.exp2(a_t[e:e + 1, :] - mm2[:, e:e + 1]), 0.0)
        s = lax.dot_general(q, kf.astype(BF16), (((1,), (1,)), ((), ())), preferred_element_type=F32)
        sw = (s * dmat).astype(BF16)
        ct = ct_ref[st]
        qi = (qf * inter[:, e:e + 1]).astype(BF16)
        num = jnp.dot(jnp.concatenate([sw, qi], axis=1), jnp.concatenate([vaug, ct.astype(BF16)], axis=0),
                      preferred_element_type=F32)
        den = jnp.maximum(jnp.abs(num[:, M_DV:]), em[:, e:e + 1])
        write_h(h, num[:, :M_DV] / jnp.concatenate([den, den], axis=1))
        kw = (kf * wcol[:, e:e + 1]).astype(BF16)
        upd = lax.dot_general(kw, vaug, (((0,), (0,)), ((), ())), preferred_element_type=F32)
        ct_ref[st] = decay[:, e:e + 1] * ct + upd
    m_ref[d, 0:1, :] = m_new


def _mlstm_kernel(cqk_ref, cv_ref, cg_ref,
                  qf_ref, pf_ref, nf_ref, vf_ref, gf_ref,
                  qb_ref, pb_ref, nb_ref, vb_ref, gb_ref,
                  cw_ref, cb_ref, bg_ref,
                  hcf_ref, hcb_ref, hf_ref, hb_ref,
                  xs_ref, ct_ref, m_ref, pd_ref, *, n_lat):
    s = pl.program_id(1)
    lc = LAT_CHUNK
    zero_rows = jnp.zeros((SUBLANES, 2 * M_QK), F32)

    @pl.when(s == 0)
    def _():
        ct_ref[...] = jnp.zeros_like(ct_ref)
        m_ref[...] = jnp.zeros_like(m_ref)
        row = lax.broadcasted_iota(jnp.int32, (lc, lc), 0)
        col = lax.broadcasted_iota(jnp.int32, (lc, lc), 1)
        pos_r = (row % COL_GROUP) * GRID_W + row // COL_GROUP
        pos_c = (col % COL_GROUP) * GRID_W + col // COL_GROUP
        pd_ref[...] = pos_r - pos_c
        xs_ref[0:SUBLANES, :] = zero_rows
        xs_ref[SUBLANES + CTX_CHUNK:2 * SUBLANES + CTX_CHUNK, :] = zero_rows
        xs_ref[SUBLANES:SUBLANES + CTX_CHUNK, :] = cqk_ref[...]
        qk = _silu_t(_conv_rows(xs_ref, CTX_CHUNK, cw_ref, cb_ref, 1))
        v = cv_ref[...]
        g = cg_ref[...]
        crow = lax.broadcasted_iota(jnp.int32, (CTX_CHUNK, CTX_CHUNK), 0)
        ccol = lax.broadcasted_iota(jnp.int32, (CTX_CHUNK, CTX_CHUNK), 1)
        for d, out_ref in ((0, hcf_ref), (1, hcb_ref)):
            def write_h(h, val, out_ref=out_ref):
                out_ref[:, h * M_DV:(h + 1) * M_DV] = val
            _mlstm_dir(d, qk, v, g, bg_ref, ct_ref, m_ref, _scan_rows, crow - ccol, write_h)

    @pl.when(s > 0)
    def _():
        sub = lax.broadcasted_iota(jnp.int32, (SUBLANES, 2 * M_QK), 0)
        for d, q_ref, p_ref, n_ref, v_ref, g_ref, out_ref in (
                (0, qf_ref, pf_ref, nf_ref, vf_ref, gf_ref, hf_ref),
                (1, qb_ref, pb_ref, nb_ref, vb_ref, gb_ref, hb_ref)):
            j = (s - 1) if d == 0 else (n_lat - s)
            has_prev = (j > 0).astype(F32)
            has_next = (j < n_lat - 1).astype(F32)
            x = q_ref[...].reshape(lc, 2 * M_QK)
            xs_ref[SUBLANES:SUBLANES + lc, :] = x
            x_last = x[lc - SUBLANES:lc, :]
            xs_ref[0:SUBLANES, :] = jnp.where(sub == 0, pltpu.roll(p_ref[...], 1, axis=0) * has_prev,
                                              pltpu.roll(x_last, 1, axis=0))
            for k in range(2):
                xs_ref[SUBLANES + lc + k * SUBLANES:2 * SUBLANES + lc + k * SUBLANES, :] = jnp.where(
                    sub == SUBLANES - 1, pltpu.roll(n_ref[k], SUBLANES - 1, axis=0) * has_next,
                    pltpu.roll(x[k * SUBLANES:(k + 1) * SUBLANES, :], SUBLANES - 1, axis=0))
            qk = _silu_t(_conv_rows(xs_ref, lc, cw_ref, cb_ref, SUBLANES))
            v = v_ref[...].reshape(lc, M_WIDTH)
            g = g_ref[...].reshape(lc, GATE_W)

            def write_h(h, val, out_ref=out_ref):
                out_ref[:, :, h * M_DV:(h + 1) * M_DV] = val.reshape(GRID_W, COL_GROUP, M_DV)
            _mlstm_dir(d, qk, v, g, bg_ref, ct_ref, m_ref, _scan_colmajor, pd_ref[...], write_h)


def _mlstm(pc, gc, px, gx, conv_w, conv_b, bg2, layer):
    _, bsz, t, _ = px.shape
    rows = t // GRID_W
    n_lat = GRID_W // COL_GROUP
    pxv = px.reshape(N_SEC32, bsz, rows, GRID_W, SEC)
    gxv = gx.reshape(bsz, rows, GRID_W, GATE_W)

    def jf(s):
        return jnp.maximum(s - 1, 0)

    def jb(s):
        return jnp.minimum(n_lat - s, n_lat - 1)

    def lat_specs(jfun):
        return [
            pl.BlockSpec((None, None, rows, COL_GROUP, SEC), lambda b, s: (S32_QK, b, 0, jfun(s), 0)),
            pl.BlockSpec((None, None, None, COL_GROUP, SEC),
                         lambda b, s: (S32_QK, b, rows - 1, jnp.maximum(jfun(s) - 1, 0), 0)),
            pl.BlockSpec((None, None, 2, COL_GROUP, SEC),
                         lambda b, s: (S32_QK, b, 0, jnp.minimum(jfun(s) + 1, n_lat - 1), 0)),
            pl.BlockSpec((None, None, rows, COL_GROUP, SEC), lambda b, s: (S32_V, b, 0, jfun(s), 0)),
            pl.BlockSpec((None, rows, COL_GROUP, GATE_W), lambda b, s: (b, 0, jfun(s), 0)),
        ]

    in_specs = [
        pl.BlockSpec((None, None, CTX_LEN, SEC), lambda b, s: (S32_QK, b, 0, 0)),
        pl.BlockSpec((None, None, CTX_LEN, SEC), lambda b, s: (S32_V, b, 0, 0)),
        pl.BlockSpec((None, CTX_LEN, GATE_W), lambda b, s: (b, 0, 0)),
    ] + lat_specs(jf) + lat_specs(jb) + [
        pl.BlockSpec((None, CONV_W, SEC), lambda b, s: (layer, 0, 0)),
        pl.BlockSpec((None, 1, SEC), lambda b, s: (layer, 0, 0)),
        pl.BlockSpec((None, SUBLANES, LANES), lambda b, s: (layer, 0, 0)),
    ]
    out_specs = [
        pl.BlockSpec((None, CTX_LEN, M_WIDTH), lambda b, s: (b, 0, 0)),
        pl.BlockSpec((None, CTX_LEN, M_WIDTH), lambda b, s: (b, 0, 0)),
        pl.BlockSpec((None, rows, COL_GROUP, M_WIDTH), lambda b, s: (b, 0, jf(s), 0)),
        pl.BlockSpec((None, rows, COL_GROUP, M_WIDTH), lambda b, s: (b, 0, jb(s), 0)),
    ]
    out_shape = [
        jax.ShapeDtypeStruct((bsz, CTX_LEN, M_WIDTH), F32),
        jax.ShapeDtypeStruct((bsz, CTX_LEN, M_WIDTH), F32),
        jax.ShapeDtypeStruct((bsz, rows, GRID_W, M_WIDTH), F32),
        jax.ShapeDtypeStruct((bsz, rows, GRID_W, M_WIDTH), F32),
    ]
    hcf, hcb, hf, hb = pl.pallas_call(
        functools.partial(_mlstm_kernel, n_lat=n_lat),
        grid=(bsz, n_lat + 1),
        in_specs=in_specs,
        out_specs=out_specs,
        out_shape=out_shape,
        scratch_shapes=[
            pltpu.VMEM((LAT_CHUNK + 3 * SUBLANES, 2 * M_QK), F32),
            pltpu.VMEM((2 * M_HEADS, M_DQK, M_DV + LANES), F32),
            pltpu.VMEM((2, SUBLANES, LANES), F32),
            pltpu.VMEM((LAT_CHUNK, LAT_CHUNK), jnp.int32),
        ],
        compiler_params=_cparams(("arbitrary", "arbitrary")),
        name="mlstm",
    )(pc, pc, gc, pxv, pxv, pxv, pxv, gxv, pxv, pxv, pxv, pxv, gxv, conv_w, conv_b, bg2)
    return hcf, hcb, hf.reshape(bsz, t, M_WIDTH), hb.reshape(bsz, t, M_WIDTH)


def _lru_conv(x_bf, prev_row, next_rows, pm_ref, cw_ref, cb_ref):
    n = x_bf.shape[0]
    s8 = SUBLANES
    xp = jnp.dot(pm_ref[0], x_bf, preferred_element_type=F32)
    sub = lax.broadcasted_iota(jnp.int32, (s8, x_bf.shape[1]), 0)

    def from_next_segment(v, fill):
        return jnp.where(sub == s8 - 1, fill, pltpu.roll(v, s8 - 1, axis=0))

    def from_prev_segment(v, fill):
        return jnp.where(sub == 0, fill, pltpu.roll(v, 1, axis=0))

    first, second, last = xp[0:s8, :], xp[s8:2 * s8, :], xp[n - s8:, :]
    xm1 = jnp.concatenate([from_prev_segment(last, prev_row), xp[:n - s8, :]], axis=0)
    xp1 = jnp.concatenate([xp[s8:, :], from_next_segment(first, next_rows[0:1, :])], axis=0)
    xp2 = jnp.concatenate([xp[2 * s8:, :], from_next_segment(first, next_rows[0:1, :]),
                           from_next_segment(second, next_rows[1:2, :])], axis=0)
    return (cb_ref[...] + cw_ref[0:1, :] * xm1 + cw_ref[1:2, :] * xp
            + cw_ref[2:3, :] * xp1 + cw_ref[3:4, :] * xp2)


def _lru_gates(d, xc, wd_ref, br_ref, sp_ref, a_ref, u_ref):
    xb = xc.astype(BF16)
    for j in range(R_WIDTH // LRU_TILE):
        sl = slice(j * LRU_TILE, (j + 1) * LRU_TILE)
        xj = xb[:, sl]
        tr = jnp.tanh(jnp.dot(xj, wd_ref[d, 0, j], preferred_element_type=F32) + br_ref[d, 0:1, sl])
        ti = jnp.tanh(jnp.dot(xj, wd_ref[d, 1, j], preferred_element_type=F32) + br_ref[d, 1:2, sl])
        sp = sp_ref[d:d + 1, sl]
        nla = tr * sp + sp
        a = jnp.exp2(nla * (-LOG2E))
        a_ref[:, sl] = a
        xh = 0.5 * xc[:, sl]
        z = jnp.tanh(nla) * (a * a + 1.0)
        root = jnp.where(z > 0.0, z * lax.rsqrt(z), 0.0)
        u_ref[:, sl] = root * (ti * xh + xh)


def _lru_scan(d, a_ref, u_ref, h_ref, pm_ref, out_ref, n):
    rev = d == 1
    s8 = SUBLANES
    groups = n // s8
    sub = lax.broadcasted_iota(jnp.int32, (s8, R_WIDTH), 0)

    def body(g, carry):
        h, acc = carry
        gg = (groups - 1 - g) if rev else g
        r0 = pl.multiple_of(gg * s8, s8)
        a = a_ref[pl.ds(r0, s8), :]
        h = a * h + u_ref[pl.ds(r0, s8), :]
        acc = a * acc
        u_ref[pl.ds(r0, s8), :] = h
        a_ref[pl.ds(r0, s8), :] = acc
        return h, acc

    u, a = lax.fori_loop(0, groups, body, (jnp.zeros((s8, R_WIDTH), F32), jnp.ones((s8, R_WIDTH), F32)))
    k = 1
    while k < s8:
        if rev:
            ok = sub < s8 - k
            a_s = pltpu.roll(a, s8 - k, axis=0)
            u_s = pltpu.roll(u, s8 - k, axis=0)
        else:
            ok = sub >= k
            a_s = pltpu.roll(a, k, axis=0)
            u_s = pltpu.roll(u, k, axis=0)
        u = u + a * jnp.where(ok, u_s, 0.0)
        a = a * jnp.where(ok, a_s, 1.0)
        k *= 2
    c0 = h_ref[d, 0:1, :]
    after = u + a * c0
    if rev:
        entry = jnp.where(sub == s8 - 1, c0, pltpu.roll(after, s8 - 1, axis=0))
        h_ref[d, 0:1, :] = after[0:1, :]
    else:
        entry = jnp.where(sub == 0, c0, pltpu.roll(after, 1, axis=0))
        h_ref[d, 0:1, :] = after[s8 - 1:s8, :]
    hs = u_ref[...] + a_ref[...] * jnp.concatenate([entry] * groups, axis=0)
    out_ref[...] = jnp.dot(pm_ref[1], hs.astype(BF16), preferred_element_type=F32).astype(out_ref.dtype)


def _rglru_kernel(cx_ref, xf_ref, pf_ref, nf_ref, xb_ref, pb_ref, nb_ref,
                  cw_ref, cb_ref, wr_ref, br_ref, lam_ref,
                  ycf_ref, ycb_ref, yf_ref, yb_ref,
                  pm_ref, a_ref, u_ref, h_ref, sp_ref, wd_ref, xc_ref, *, n_lat):
    s = pl.program_id(1)
    seg = LRU_BLK // SUBLANES

    @pl.when(s == 0)
    def _():
        h_ref[...] = jnp.zeros_like(h_ref)
        sp_ref[...] = (0.5 * LRU_C) * jax.nn.softplus(-lam_ref[...])
        wd_ref[...] = jnp.zeros_like(wd_ref)
        per = LRU_TILE // R_BLOCK
        for dd in range(2):
            for g in range(2):
                for blk in range(R_BLOCKS):
                    j, p = divmod(blk, per)
                    rows = slice(p * R_BLOCK, (p + 1) * R_BLOCK)
                    wd_ref[dd, g, j, rows, rows] = (0.5 * wr_ref[dd, g, blk]).astype(BF16)
        row = lax.broadcasted_iota(jnp.int32, (LRU_BLK, LRU_BLK), 0)
        col = lax.broadcasted_iota(jnp.int32, (LRU_BLK, LRU_BLK), 1)
        pm_ref[0] = jnp.where(col == (row % SUBLANES) * seg + row // SUBLANES, 1.0, 0.0).astype(BF16)
        pm_ref[1] = jnp.where(row == (col % SUBLANES) * seg + col // SUBLANES, 1.0, 0.0).astype(BF16)
        zero_rows = jnp.zeros((2, R_WIDTH), F32)
        xc = _lru_conv(cx_ref[...], zero_rows[0:1, :], zero_rows, pm_ref, cw_ref, cb_ref)
        for d, out_ref in ((0, ycf_ref), (1, ycb_ref)):
            _lru_gates(d, xc, wd_ref, br_ref, sp_ref, a_ref, u_ref)
            _lru_scan(d, a_ref, u_ref, h_ref, pm_ref, out_ref, LRU_BLK)

    @pl.when(jnp.logical_and(s > 0, s <= n_lat // 2))
    def _():
        for d, x_ref, p_ref, n_ref in ((0, xf_ref, pf_ref, nf_ref), (1, xb_ref, pb_ref, nb_ref)):
            j = (s - 1) if d == 0 else (n_lat - s)
            has_prev = (j > 0).astype(F32)
            has_next = (j < n_lat - 1).astype(F32)
            for k in range(LRU_PER_STEP):
                lo, hi = k * LRU_BLK, (k + 1) * LRU_BLK
                if k == 0:
                    prev_row = p_ref[...].astype(F32)[BF16_ROWS - 1:BF16_ROWS, :] * has_prev
                else:
                    prev_row = x_ref[lo - BF16_ROWS:lo, :].astype(F32)[BF16_ROWS - 1:BF16_ROWS, :]
                if k == LRU_PER_STEP - 1:
                    next_rows = n_ref[...].astype(F32)[0:2, :] * has_next
                else:
                    next_rows = x_ref[hi:hi + BF16_ROWS, :].astype(F32)[0:2, :]
                xc_ref[j * LRU_PER_STEP + k] = _lru_conv(x_ref[lo:hi, :], prev_row, next_rows, pm_ref, cw_ref, cb_ref)

    @pl.when(s > 0)
    def _():
        for d, out_ref in ((0, yf_ref), (1, yb_ref)):
            j = (s - 1) if d == 0 else (n_lat - s)
            for k in (range(LRU_PER_STEP) if d == 0 else reversed(range(LRU_PER_STEP))):
                _lru_gates(d, xc_ref[j * LRU_PER_STEP + k], wd_ref, br_ref, sp_ref, a_ref, u_ref)
                _lru_scan(d, a_ref, u_ref, h_ref, pm_ref, out_ref.at[pl.ds(k * LRU_BLK, LRU_BLK), :], LRU_BLK)


def _rglru(pc, px, conv_w, conv_b, w_rg, b_rg, lam, layer):
    _, bsz, t, _ = px.shape
    rows_step = LRU_BLK * LRU_PER_STEP
    n_lat = t // rows_step
    per_blk = rows_step // BF16_ROWS
    n_halo = t // BF16_ROWS

    def jf(s):
        return jnp.maximum(s - 1, 0)

    def jb(s):
        return jnp.minimum(n_lat - s, n_lat - 1)

    def jfx(s):
        return jnp.clip(s - 1, 0, n_lat // 2 - 1)

    def jbx(s):
        return jnp.clip(n_lat - s, n_lat // 2, n_lat - 1)

    def lat_specs(jfun):
        return [
            pl.BlockSpec((None, None, rows_step, SEC), lambda b, s: (S16_XL, b, jfun(s), 0)),
            pl.BlockSpec((None, None, BF16_ROWS, SEC),
                         lambda b, s: (S16_XL, b, jnp.maximum(jfun(s) * per_blk - 1, 0), 0)),
            pl.BlockSpec((None, None, BF16_ROWS, SEC),
                         lambda b, s: (S16_XL, b, jnp.minimum((jfun(s) + 1) * per_blk, n_halo - 1), 0)),
        ]

    in_specs = [pl.BlockSpec((None, None, CTX_LEN, SEC), lambda b, s: (S16_XL, b, 0, 0))]
    in_specs += lat_specs(jfx) + lat_specs(jbx) + [
        pl.BlockSpec((None, CONV_W, SEC), lambda b, s: (layer, 0, 0)),
        pl.BlockSpec((None, 1, SEC), lambda b, s: (layer, 0, 0)),
        pl.BlockSpec((None,) + w_rg.shape[1:], lambda b, s: (layer, 0, 0, 0, 0, 0)),
        pl.BlockSpec((None,) + b_rg.shape[1:], lambda b, s: (layer, 0, 0, 0)),
        pl.BlockSpec((None,) + lam.shape[1:], lambda b, s: (layer, 0, 0)),
    ]
    out_specs = [
        pl.BlockSpec((None, CTX_LEN, R_WIDTH), lambda b, s: (b, 0, 0)),
        pl.BlockSpec((None, CTX_LEN, R_WIDTH), lambda b, s: (b, 0, 0)),
        pl.BlockSpec((None, rows_step, R_WIDTH), lambda b, s: (b, jf(s), 0)),
        pl.BlockSpec((None, rows_step, R_WIDTH), lambda b, s: (b, jb(s), 0)),
    ]
    out_shape = [
        jax.ShapeDtypeStruct((bsz, CTX_LEN, R_WIDTH), BF16),
        jax.ShapeDtypeStruct((bsz, CTX_LEN, R_WIDTH), BF16),
        jax.ShapeDtypeStruct((bsz, t, R_WIDTH), BF16),
        jax.ShapeDtypeStruct((bsz, t, R_WIDTH), BF16),
    ]
    return pl.pallas_call(
        functools.partial(_rglru_kernel, n_lat=n_lat),
        grid=(bsz, n_lat + 1),
        in_specs=in_specs,
        out_specs=out_specs,
        out_shape=out_shape,
        scratch_shapes=[
            pltpu.VMEM((2, LRU_BLK, LRU_BLK), BF16),
            pltpu.VMEM((LRU_BLK, R_WIDTH), F32),
            pltpu.VMEM((LRU_BLK, R_WIDTH), F32),
            pltpu.VMEM((2, SUBLANES, R_WIDTH), F32),
            pltpu.VMEM((2, R_WIDTH), F32),
            pltpu.VMEM((2, 2, R_WIDTH // LRU_TILE, LRU_TILE, LRU_TILE), BF16),
            pltpu.VMEM((t // LRU_BLK, LRU_BLK, R_WIDTH), F32),
        ],
        compiler_params=_cparams(("arbitrary", "arbitrary")),
        name="rglru",
    )(pc, px, px, px, px, px, px, conv_w, conv_b, w_rg, b_rg, lam)


def _out_proj_kernel(hf_ref, hb_ref, yf_ref, yb_ref, o_ref, zm_ref, zl_ref, x_ref, gt_ref, mg_ref, w_ref, fg_ref,
                     out_ref, *, final):
    hm = hf_ref[...] + hb_ref[...]
    parts = []
    for h in range(M_HEADS):
        hh = hm[:, h * M_DV:(h + 1) * M_DV]
        parts.append(hh * lax.rsqrt(jnp.mean(hh * hh, axis=-1, keepdims=True) + EPS))
    hn = jnp.concatenate(parts, axis=1) * mg_ref[...]
    ym = hn * _sigmoid_t(o_ref[...].astype(F32)) * _silu_t(zm_ref[...].astype(F32))
    yr = (yf_ref[...].astype(F32) + yb_ref[...].astype(F32)) * _silu_t(zl_ref[...].astype(F32))
    y = jnp.concatenate([ym, yr], axis=1).astype(BF16)
    xn = x_ref[...] + gt_ref[...] * jnp.dot(y, w_ref[...], preferred_element_type=F32)
    if final:
        xn = xn * lax.rsqrt(jnp.mean(xn * xn, axis=-1, keepdims=True) + EPS) * fg_ref[...]
    out_ref[...] = xn


def _out_proj(hf, hb, yf, yb, p16, x2d, mod_l, row_of_tile, m_norm_g, w_out, final_g, layer, tm, final):
    m, d = x2d.shape

    def tok(width):
        return pl.BlockSpec((tm, width), lambda i: (i, 0))

    def sec(k):
        return pl.BlockSpec((None, tm, SEC), lambda i: (k, i, 0))

    return pl.pallas_call(
        functools.partial(_out_proj_kernel, final=final),
        grid=(m // tm,),
        in_specs=[
            tok(M_WIDTH), tok(M_WIDTH), tok(R_WIDTH), tok(R_WIDTH),
            sec(S16_O), sec(S16_ZM), sec(S16_ZL),
            tok(d),
            pl.BlockSpec((None, None, 1, d), lambda i: (row_of_tile(i), 2, 0, 0)),
            pl.BlockSpec((None, 1, M_WIDTH), lambda i: (layer, 0, 0)),
            pl.BlockSpec((None, d, d), lambda i: (layer, 0, 0), pipeline_mode=pl.Buffered(1)),
            pl.BlockSpec((1, d), lambda i: (0, 0)),
        ],
        out_specs=tok(d),
        out_shape=jax.ShapeDtypeStruct((m, d), F32),
        compiler_params=_cparams(("arbitrary",)),
        name="out_proj",
    )(hf, hb, yf, yb, p16, p16, p16, x2d, mod_l, m_norm_g, w_out, final_g)


def kernel(x, c, ctx, c_ctx, w_mod, b_mod, norm_g, w_in, b_gate, conv_qk_w, conv_qk_b, m_norm_g, conv_r_w, conv_r_b,
           w_rg, b_rg, lru_lambda, w_out, final_g):
    bsz, t, d = x.shape
    depth = w_mod.shape[0]
    nh = M_HEADS

    w_t = jnp.swapaxes(w_in, 1, 2)
    w_main, w_gate = _win_cast(w_t, 0, WA_SECS * SEC, 4 * nh)
    b_mod3 = b_mod[:, None, :]
    bg2 = jnp.zeros((depth, SUBLANES, GATE_W), F32).at[:, 0, :4 * nh].set(b_gate)
    w_out_b = _cast_bf16(w_out)
    norm_g3 = norm_g[:, None, :]
    m_norm_g3 = m_norm_g[:, None, :]
    conv_qk_b3 = conv_qk_b[:, None, :]
    conv_r_b3 = conv_r_b[:, None, :]
    b_rg4 = 0.5 * b_rg.reshape(depth, 2, 2, R_WIDTH)
    fg = final_g[None, :]

    cvec = jnp.concatenate([c, c_ctx[None, :], jnp.zeros((SUBLANES - bsz - 1, d), F32)], axis=0)
    mod = _modulation(cvec, w_mod, b_mod3, 0)

    tm_in = 1024
    tm_ctx = bsz * CTX_LEN
    tm_out = 512
    x2d = x.reshape(bsz * t, d)
    c2d = ctx.reshape(bsz * CTX_LEN, d)
    lat_row_in = lambda i: i // (t // tm_in)
    lat_row_out = lambda i: i // (t // tm_out)
    ctx_row = lambda i: bsz

    for l in range(depth):
        last = l == depth - 1
        mod_l = mod.reshape(SUBLANES, 3, 1, d)
        nxt = None if last else (cvec, w_mod, b_mod3, w_t, 4 * nh)
        lat = _in_proj(x2d, mod_l, lat_row_in, norm_g3, w_main, w_gate, l, tm_in, nxt)
        px32, px16, gx = lat[:3]
        pc32, pc16, gc = _in_proj(c2d, mod_l, ctx_row, norm_g3, w_main, w_gate, l, tm_ctx)
        if not last:
            mod, w_main, w_gate = lat[3:]
        hcf, hcb, hf, hb = _mlstm(pc32.reshape(N_SEC32, bsz, CTX_LEN, SEC), gc.reshape(bsz, CTX_LEN, GATE_W),
                                  px32.reshape(N_SEC32, bsz, t, SEC), gx.reshape(bsz, t, GATE_W),
                                  conv_qk_w, conv_qk_b3, bg2, l)
        ycf, ycb, yf, yb = _rglru(pc16.reshape(N_SEC - N_SEC32, bsz, CTX_LEN, SEC),
                                  px16.reshape(N_SEC - N_SEC32, bsz, t, SEC),
                                  conv_r_w, conv_r_b3, w_rg, b_rg4, lru_lambda, l)
        x2d = _out_proj(hf.reshape(bsz * t, M_WIDTH), hb.reshape(bsz * t, M_WIDTH),
                        yf.reshape(bsz * t, R_WIDTH), yb.reshape(bsz * t, R_WIDTH),
                        px16, x2d, mod_l, lat_row_out, m_norm_g3, w_out_b, fg, l, tm_out, last)
        if not last:
            c2d = _out_proj(hcf.reshape(bsz * CTX_LEN, M_WIDTH), hcb.reshape(bsz * CTX_LEN, M_WIDTH),
                            ycf.reshape(bsz * CTX_LEN, R_WIDTH), ycb.reshape(bsz * CTX_LEN, R_WIDTH),
                            pc16, c2d, mod_l, ctx_row, m_norm_g3, w_out_b, fg, l, tm_out, False)
    return x2d.reshape(bsz, t, d)
```

```python
import functools

import jax
import jax.numpy as jnp
from jax import lax
from jax.experimental import pallas as pl
from jax.experimental.pallas import tpu as pltpu

D_MODEL = 2048
DEPTH = 4
CTX_LEN = 256
GRID_W = 64
M_WIDTH = 1024
R_WIDTH = 1024
M_HEADS = 4
M_DV = 256
M_DQK = 128
M_QK = 512
R_BLOCKS = 16
R_BLOCK = 64
CONV_W = 4
LRU_C = 8.0
EPS = 1e-6

LANES = 128
SUBLANES = 8
BF16_ROWS = 16
SEC = 1024
N_SEC = 6
N_SEC32 = 2
S32_QK, S32_V = 0, 1
S16_O, S16_ZM, S16_XL, S16_ZL = 0, 1, 2, 3
WA_SECS = 4
GATE_W = LANES
CTX_CHUNK = CTX_LEN
COL_GROUP = SUBLANES
LAT_CHUNK = COL_GROUP * GRID_W
LRU_BLK = 256
LRU_PER_STEP = 2
LRU_TILE = 256
VMEM_LIMIT = 56 * 1024 * 1024
LOG2E = 1.4426950408889634

F32 = jnp.float32
BF16 = jnp.bfloat16


def _cparams(sem):
    return pltpu.CompilerParams(dimension_semantics=sem, vmem_limit_bytes=VMEM_LIMIT)


def _silu_t(x):
    h = 0.5 * x
    return h + h * jnp.tanh(h)


def _dot_nt(a, b):
    return lax.dot_general(a, b, (((1,), (1,)), ((), ())), preferred_element_type=F32)


def _sigmoid_t(x):
    return 0.5 * jnp.tanh(0.5 * x) + 0.5


def _mod_kernel(c_ref, w_ref, b_ref, o_ref):
    c = c_ref[...]
    s = (c * jax.nn.sigmoid(c)).astype(BF16)
    o_ref[...] = jnp.dot(s, w_ref[...].astype(BF16), preferred_element_type=F32) + b_ref[...]


def _modulation(cvec, w_mod, b_mod3, layer):
    _, d, n = w_mod.shape
    tn = 1024
    return pl.pallas_call(
        _mod_kernel,
        grid=(n // tn,),
        in_specs=[
            pl.BlockSpec((SUBLANES, d), lambda j: (0, 0)),
            pl.BlockSpec((None, d, tn), lambda j: (layer, 0, j)),
            pl.BlockSpec((None, 1, tn), lambda j: (layer, 0, j)),
        ],
        out_specs=pl.BlockSpec((SUBLANES, tn), lambda j: (0, j)),
        out_shape=jax.ShapeDtypeStruct((SUBLANES, n), F32),
        compiler_params=_cparams(("arbitrary",)),
        name="mod",
    )(cvec, w_mod, b_mod3)


def _cast_rows(r, a_ref, b_ref, o_ref, g_ref, n_direct, skip):
    tr = o_ref.shape[0]

    @pl.when(r < n_direct)
    def _():
        o_ref[...] = a_ref[...].astype(BF16)

    @pl.when(r >= n_direct)
    def _():
        o_ref[:tr - skip, :] = a_ref[skip:, :].astype(BF16)
        o_ref[tr - skip:, :] = b_ref[...].astype(BF16)

    @pl.when(r == n_direct)
    def _():
        g_ref[0:skip, :] = a_ref[0:skip, :].astype(BF16)
        g_ref[skip:, :] = jnp.zeros((g_ref.shape[0] - skip, g_ref.shape[1]), BF16)


def _in_proj_kernel(x_ref, sh_ref, sc_ref, g_ref, w_ref, wg_ref, *refs, prep):
    n = pl.program_id(1)
    if prep is None:
        p32_ref, p16_ref, gate_ref, h_scr = refs

        def prep_step():
            pass
    else:
        c_ref, wm_ref, bm_ref, wa_ref, wb_ref, p32_ref, p16_ref, gate_ref, modn_ref, wn_ref, wgn_ref, h_scr = refs
        n_direct, skip = prep
        k = pl.program_id(0) * N_SEC + n

        @pl.when(k == n_direct)
        def _():
            wgn_ref[0:skip, :] = wa_ref[0:skip, :].astype(BF16)
            wgn_ref[skip:, :] = jnp.zeros((wgn_ref.shape[0] - skip, wgn_ref.shape[1]), BF16)

        def prep_step():
            c = c_ref[...]
            modn_ref[...] = jnp.dot((c * jax.nn.sigmoid(c)).astype(BF16), wm_ref[...].astype(BF16),
                                    preferred_element_type=F32) + bm_ref[...]
            a = wa_ref[...]
            below = jnp.concatenate([a[skip:, :], wb_ref[...]], axis=0)
            wn_ref[...] = jnp.where(k < n_direct, a, below).astype(BF16)

    @pl.when(n == 0)
    def _():
        x = x_ref[...]
        ms = jnp.mean(x * x, axis=-1, keepdims=True)
        y = x * lax.rsqrt(ms + EPS) * g_ref[...]
        h = (y * (1.0 + sc_ref[...]) + sh_ref[...]).astype(BF16)
        h_scr[...] = h
        gate_ref[...] = _dot_nt(h, wg_ref[...])

    @pl.when(n < N_SEC32)
    def _():
        prep_step()
        p32_ref[...] = _dot_nt(h_scr[...], w_ref[...])

    @pl.when(n >= N_SEC32)
    def _():
        prep_step()
        p16_ref[...] = _dot_nt(h_scr[...], w_ref[...]).astype(BF16)


def _in_proj(x2d, mod_l, row_of_tile, norm_g, w_main, w_gate, layer, tm, nxt=None):
    m, d = x2d.shape
    n_tiles = m // tm
    in_specs = [
        pl.BlockSpec((tm, d), lambda i, n: (i, 0)),
        pl.BlockSpec((None, None, 1, d), lambda i, n: (row_of_tile(i), 0, 0, 0)),
        pl.BlockSpec((None, None, 1, d), lambda i, n: (row_of_tile(i), 1, 0, 0)),
        pl.BlockSpec((None, 1, d), lambda i, n: (layer, 0, 0)),
        pl.BlockSpec((SEC, d), lambda i, n: (n, 0)),
        pl.BlockSpec((GATE_W, d), lambda i, n: (0, 0)),
    ]
    out_specs = [
        pl.BlockSpec((None, tm, SEC), lambda i, n: (jnp.minimum(n, N_SEC32 - 1), i, 0)),
        pl.BlockSpec((None, tm, SEC), lambda i, n: (jnp.maximum(n - N_SEC32, 0), i, 0)),
        pl.BlockSpec((tm, GATE_W), lambda i, n: (i, 0)),
    ]
    out_shape = [
        jax.ShapeDtypeStruct((N_SEC32, m, SEC), F32),
        jax.ShapeDtypeStruct((N_SEC - N_SEC32, m, SEC), BF16),
        jax.ShapeDtypeStruct((m, GATE_W), F32),
    ]
    operands = [x2d, mod_l, mod_l, norm_g, w_main, w_gate]
    prep = None
    if nxt is not None:
        cvec, w_mod, b_mod3, w_t, skip = nxt
        steps = n_tiles * N_SEC
        n_mod = w_mod.shape[2]
        tr, tn = (N_SEC * SEC) // steps, n_mod // steps
        assert tr * steps == N_SEC * SEC and tr % skip == 0 and tn * steps == n_mod and tn % LANES == 0
        prep = (WA_SECS * SEC // tr, skip)

        def step(i, n):
            return i * N_SEC + n

        in_specs += [
            pl.BlockSpec((SUBLANES, d), lambda i, n: (0, 0)),
            pl.BlockSpec((None, d, tn), lambda i, n: (layer + 1, 0, step(i, n))),
            pl.BlockSpec((None, 1, tn), lambda i, n: (layer + 1, 0, step(i, n))),
            pl.BlockSpec((None, tr, d), lambda i, n: (layer + 1, step(i, n), 0)),
            pl.BlockSpec((None, skip, d), lambda i, n: (layer + 1, (step(i, n) + 1) * (tr // skip), 0)),
        ]
        out_specs += [
            pl.BlockSpec((SUBLANES, tn), lambda i, n: (0, step(i, n))),
            pl.BlockSpec((tr, d), lambda i, n: (step(i, n), 0)),
            pl.BlockSpec((GATE_W, d), lambda i, n: (0, 0)),
        ]
        out_shape += [
            jax.ShapeDtypeStruct((SUBLANES, n_mod), F32),
            jax.ShapeDtypeStruct((N_SEC * SEC, d), BF16),
            jax.ShapeDtypeStruct((GATE_W, d), BF16),
        ]
        operands += [cvec, w_mod, b_mod3, w_t, w_t]
    return pl.pallas_call(
        functools.partial(_in_proj_kernel, prep=prep),
        grid=(n_tiles, N_SEC),
        in_specs=in_specs,
        out_specs=out_specs,
        out_shape=out_shape,
        scratch_shapes=[pltpu.VMEM((tm, d), BF16)],
        compiler_params=_cparams(("arbitrary", "arbitrary")),
        name="in_proj",
    )(*operands)


def _win_cast_kernel(a_ref, b_ref, o_ref, g_ref, *, n_direct, skip):
    _cast_rows(pl.program_id(0), a_ref, b_ref, o_ref, g_ref, n_direct, skip)


def _win_cast(w_t, layer, split, skip):
    _, n_in, d = w_t.shape
    tr = 256
    return pl.pallas_call(
        functools.partial(_win_cast_kernel, n_direct=split // tr, skip=skip),
        grid=((n_in - skip) // tr,),
        in_specs=[pl.BlockSpec((None, tr, d), lambda r: (layer, r, 0)),
                  pl.BlockSpec((None, skip, d), lambda r: (layer, (r + 1) * (tr // skip), 0))],
        out_specs=[pl.BlockSpec((tr, d), lambda r: (r, 0)),
                   pl.BlockSpec((GATE_W, d), lambda r: (0, 0))],
        out_shape=[jax.ShapeDtypeStruct((n_in - skip, d), BF16),
                   jax.ShapeDtypeStruct((GATE_W, d), BF16)],
        compiler_params=_cparams(("arbitrary",)),
        name="win_cast",
    )(w_t, w_t)


def _cast_kernel(w_ref, o_ref):
    o_ref[...] = w_ref[...].astype(BF16)


def _cast_bf16(w):
    depth, r, c = w.shape
    tr = 512
    return pl.pallas_call(
        _cast_kernel,
        grid=(depth, r // tr),
        in_specs=[pl.BlockSpec((None, tr, c), lambda l, j: (l, j, 0))],
        out_specs=pl.BlockSpec((None, tr, c), lambda l, j: (l, j, 0)),
        out_shape=jax.ShapeDtypeStruct((depth, r, c), BF16),
        compiler_params=_cparams(("arbitrary", "arbitrary")),
        name="cast_bf16",
    )(w)


def _shifted(x, k, reverse, ident):
    n = x.shape[0]
    row = lax.broadcasted_iota(jnp.int32, x.shape, 0)
    if reverse:
        return jnp.where(row < n - k, pltpu.roll(x, n - k, axis=0), ident)
    return jnp.where(row >= k, pltpu.roll(x, k, axis=0), ident)


def _scan_rows(x, op, ident, reverse):
    n = x.shape[0]
    k = 1
    while k < n:
        x = op(x, _shifted(x, k, reverse, ident))
        k *= 2
    return x


def _scan_colmajor(x, op, ident, reverse):
    n = x.shape[0]
    k = SUBLANES
    while k < n:
        x = op(x, _shifted(x, k, reverse, ident))
        k *= 2
    tot = x[0:SUBLANES, :] if reverse else x[n - SUBLANES:n, :]
    tot = _scan_rows(tot, op, ident, reverse)
    tot = _shifted(tot, 1, reverse, ident)
    return op(x, jnp.concatenate([tot] * (n // SUBLANES), axis=0))


def _conv_rows(xs_ref, n, w_ref, b_ref, step):
    base = SUBLANES - step
    acc = b_ref[...] + w_ref[0:1, :] * xs_ref[base:base + n, :]
    for j in range(1, CONV_W):
        acc = acc + w_ref[j:j + 1, :] * xs_ref[base + j * step:base + j * step + n, :]
    return acc


def _mlstm_dir(d, qk, v, gates, bg_ref, ct_ref, m_ref, scan, posdiff, write_h):
    n = qk.shape[0]
    rev = d == 1
    gi = gates + bg_ref[0:1, :]
    lf = jax.nn.log_sigmoid(pltpu.roll(gi, LANES - M_HEADS, axis=1))
    bc = scan(lf, jnp.add, 0.0, rev)
    a = gi - bc
    m_prev = m_ref[d, 0:1, :]
    mm = jnp.maximum(scan(a, jnp.maximum, -jnp.inf, rev), m_prev)
    inter = jnp.exp(m_prev - mm)
    em = jnp.exp(-(bc + mm))
    last = 0 if rev else n - 1
    mm_last = mm[last:last + 1, :]
    m_new = bc[last:last + 1, :] + mm_last
    decay = jnp.exp(m_prev - mm_last)
    wcol = jnp.exp(a - mm_last)
    a_t = (a * LOG2E).T
    mm2 = mm * LOG2E
    mask = (posdiff <= 0) if rev else (posdiff >= 0)
    ones = jnp.ones((n, LANES), BF16)
    for h in range(M_HEADS):
        e = 2 * M_HEADS * d + h
        st = d * M_HEADS + h
        qf = qk[:, h * M_DQK:(h + 1) * M_DQK]
        q = qf.astype(BF16)
        kf = qk[:, M_QK + h * M_DQK:M_QK + (h + 1) * M_DQK] * (M_DQK ** -0.5)
        vaug = jnp.concatenate([v[:, h * M_DV:(h + 1) * M_DV].astype(BF16), ones], axis=1)
        dmat = jnp.where(mask, jnp.exp2(a_t[e:e + 1, :] - mm2[:, e:e + 1]), 0.0)
        s = lax.dot_general(q, kf.astype(BF16), (((1,), (1,)), ((), ())), preferred_element_type=F32)
        sw = (s * dmat).astype(BF16)
        ct = ct_ref[st]
        qi = (qf * inter[:, e:e + 1]).astype(BF16)
        num = jnp.dot(jnp.concatenate([sw, qi], axis=1), jnp.concatenate([vaug, ct.astype(BF16)], axis=0),
                      preferred_element_type=F32)
        den = jnp.maximum(jnp.abs(num[:, M_DV:]), em[:, e:e + 1])
        write_h(h, num[:, :M_DV] / jnp.concatenate([den, den], axis=1))
        kw = (kf * wcol[:, e:e + 1]).astype(BF16)
        upd = lax.dot_general(kw, vaug, (((0,), (0,)), ((), ())), preferred_element_type=F32)
        ct_ref[st] = decay[:, e:e + 1] * ct + upd
    m_ref[d, 0:1, :] = m_new


def _mlstm_kernel(cqk_ref, cv_ref, cg_ref,
                  qf_ref, pf_ref, nf_ref, vf_ref, gf_ref,
                  qb_ref, pb_ref, nb_ref, vb_ref, gb_ref,
                  cw_ref, cb_ref, bg_ref,
                  hc_ref, hhi_ref, hlo_ref,
                  xs_ref, ct_ref, m_ref, pd_ref, park_ref, *, n_lat):
    s = pl.program_id(1)
    lc = LAT_CHUNK
    zero_rows = jnp.zeros((SUBLANES, 2 * M_QK), F32)

    @pl.when(s == 0)
    def _():
        ct_ref[...] = jnp.zeros_like(ct_ref)
        m_ref[...] = jnp.zeros_like(m_ref)
        park_ref[...] = jnp.zeros_like(park_ref)
        row = lax.broadcasted_iota(jnp.int32, (lc, lc), 0)
        col = lax.broadcasted_iota(jnp.int32, (lc, lc), 1)
        pos_r = (row % COL_GROUP) * GRID_W + row // COL_GROUP
        pos_c = (col % COL_GROUP) * GRID_W + col // COL_GROUP
        pd_ref[...] = pos_r - pos_c
        xs_ref[0:SUBLANES, :] = zero_rows
        xs_ref[SUBLANES + CTX_CHUNK:2 * SUBLANES + CTX_CHUNK, :] = zero_rows
        xs_ref[SUBLANES:SUBLANES + CTX_CHUNK, :] = cqk_ref[...]
        qk = _silu_t(_conv_rows(xs_ref, CTX_CHUNK, cw_ref, cb_ref, 1))
        v = cv_ref[...]
        g = cg_ref[...]
        crow = lax.broadcasted_iota(jnp.int32, (CTX_CHUNK, CTX_CHUNK), 0)
        ccol = lax.broadcasted_iota(jnp.int32, (CTX_CHUNK, CTX_CHUNK), 1)
        for d in (0, 1):
            def write_h(h, val, d=d):
                lanes = slice(h * M_DV, (h + 1) * M_DV)
                hc_ref[:, lanes] = val if d == 0 else hc_ref[:, lanes] + val
            _mlstm_dir(d, qk, v, g, bg_ref, ct_ref, m_ref, _scan_rows, crow - ccol, write_h)

    @pl.when(s > 0)
    def _():
        sub = lax.broadcasted_iota(jnp.int32, (SUBLANES, 2 * M_QK), 0)
        for d, q_ref, p_ref, n_ref, v_ref, g_ref, out_ref in (
                (0, qf_ref, pf_ref, nf_ref, vf_ref, gf_ref, hhi_ref),
                (1, qb_ref, pb_ref, nb_ref, vb_ref, gb_ref, hlo_ref)):
            j = (s - 1) if d == 0 else (n_lat - s)
            has_prev = (j > 0).astype(F32)
            has_next = (j < n_lat - 1).astype(F32)
            x = q_ref[...].reshape(lc, 2 * M_QK)
            xs_ref[SUBLANES:SUBLANES + lc, :] = x
            x_last = x[lc - SUBLANES:lc, :]
            xs_ref[0:SUBLANES, :] = jnp.where(sub == 0, pltpu.roll(p_ref[...], 1, axis=0) * has_prev,
                                              pltpu.roll(x_last, 1, axis=0))
            for k in range(2):
                xs_ref[SUBLANES + lc + k * SUBLANES:2 * SUBLANES + lc + k * SUBLANES, :] = jnp.where(
                    sub == SUBLANES - 1, pltpu.roll(n_ref[k], SUBLANES - 1, axis=0) * has_next,
                    pltpu.roll(x[k * SUBLANES:(k + 1) * SUBLANES, :], SUBLANES - 1, axis=0))
            qk = _silu_t(_conv_rows(xs_ref, lc, cw_ref, cb_ref, SUBLANES))
            v = v_ref[...].reshape(lc, M_WIDTH)
            g = g_ref[...].reshape(lc, GATE_W)

            def write_h(h, val, out_ref=out_ref, j=j):
                lanes = slice(h * M_DV, (h + 1) * M_DV)
                other = park_ref[j, :, lanes].astype(F32)
                park_ref[j, :, lanes] = val.astype(BF16)
                out_ref[:, :, lanes] = (val + other).reshape(GRID_W, COL_GROUP, M_DV)
            _mlstm_dir(d, qk, v, g, bg_ref, ct_ref, m_ref, _scan_colmajor, pd_ref[...], write_h)


def _mlstm(pc, gc, px, gx, conv_w, conv_b, bg2, layer):
    _, bsz, t, _ = px.shape
    rows = t // GRID_W
    n_lat = GRID_W // COL_GROUP
    pxv = px.reshape(N_SEC32, bsz, rows, GRID_W, SEC)
    gxv = gx.reshape(bsz, rows, GRID_W, GATE_W)

    def jf(s):
        return jnp.maximum(s - 1, 0)

    def jb(s):
        return jnp.minimum(n_lat - s, n_lat - 1)

    def lat_specs(jfun):
        return [
            pl.BlockSpec((None, None, rows, COL_GROUP, SEC), lambda b, s: (S32_QK, b, 0, jfun(s), 0)),
            pl.BlockSpec((None, None, None, COL_GROUP, SEC),
                         lambda b, s: (S32_QK, b, rows - 1, jnp.maximum(jfun(s) - 1, 0), 0)),
            pl.BlockSpec((None, None, 2, COL_GROUP, SEC),
                         lambda b, s: (S32_QK, b, 0, jnp.minimum(jfun(s) + 1, n_lat - 1), 0)),
            pl.BlockSpec((None, None, rows, COL_GROUP, SEC), lambda b, s: (S32_V, b, 0, jfun(s), 0)),
            pl.BlockSpec((None, rows, COL_GROUP, GATE_W), lambda b, s: (b, 0, jfun(s), 0)),
        ]

    in_specs = [
        pl.BlockSpec((None, None, CTX_LEN, SEC), lambda b, s: (S32_QK, b, 0, 0)),
        pl.BlockSpec((None, None, CTX_LEN, SEC), lambda b, s: (S32_V, b, 0, 0)),
        pl.BlockSpec((None, CTX_LEN, GATE_W), lambda b, s: (b, 0, 0)),
    ] + lat_specs(jf) + lat_specs(jb) + [
        pl.BlockSpec((None, CONV_W, SEC), lambda b, s: (layer, 0, 0)),
        pl.BlockSpec((None, 1, SEC), lambda b, s: (layer, 0, 0)),
        pl.BlockSpec((None, SUBLANES, LANES), lambda b, s: (layer, 0, 0)),
    ]
    half = n_lat // 2
    out_specs = [
        pl.BlockSpec((None, CTX_LEN, M_WIDTH), lambda b, s: (b, 0, 0)),
        pl.BlockSpec((None, rows, COL_GROUP, M_WIDTH), lambda b, s: (b, 0, jnp.clip(s - 1 - half, 0, half - 1), 0)),
        pl.BlockSpec((None, rows, COL_GROUP, M_WIDTH), lambda b, s: (b, 0, jnp.clip(n_lat - s, 0, half - 1), 0)),
    ]
    out_shape = [
        jax.ShapeDtypeStruct((bsz, CTX_LEN, M_WIDTH), F32),
        jax.ShapeDtypeStruct((bsz, rows, GRID_W // 2, M_WIDTH), F32),
        jax.ShapeDtypeStruct((bsz, rows, GRID_W // 2, M_WIDTH), F32),
    ]
    return pl.pallas_call(
        functools.partial(_mlstm_kernel, n_lat=n_lat),
        grid=(bsz, n_lat + 1),
        in_specs=in_specs,
        out_specs=out_specs,
        out_shape=out_shape,
        scratch_shapes=[
            pltpu.VMEM((LAT_CHUNK + 3 * SUBLANES, 2 * M_QK), F32),
            pltpu.VMEM((2 * M_HEADS, M_DQK, M_DV + LANES), F32),
            pltpu.VMEM((2, SUBLANES, LANES), F32),
            pltpu.VMEM((LAT_CHUNK, LAT_CHUNK), jnp.int32),
            pltpu.VMEM((n_lat, LAT_CHUNK, M_WIDTH), BF16),
        ],
        compiler_params=_cparams(("arbitrary", "arbitrary")),
        name="mlstm",
    )(pc, pc, gc, pxv, pxv, pxv, pxv, gxv, pxv, pxv, pxv, pxv, gxv, conv_w, conv_b, bg2)


def _lru_conv(x_bf, prev_row, next_rows, pm_ref, cw_ref, cb_ref):
    n = x_bf.shape[0]
    s8 = SUBLANES
    xp = jnp.dot(pm_ref[0], x_bf, preferred_element_type=F32)
    sub = lax.broadcasted_iota(jnp.int32, (s8, x_bf.shape[1]), 0)

    def from_next_segment(v, fill):
        return jnp.where(sub == s8 - 1, fill, pltpu.roll(v, s8 - 1, axis=0))

    def from_prev_segment(v, fill):
        return jnp.where(sub == 0, fill, pltpu.roll(v, 1, axis=0))

    first, second, last = xp[0:s8, :], xp[s8:2 * s8, :], xp[n - s8:, :]
    xm1 = jnp.concatenate([from_prev_segment(last, prev_row), xp[:n - s8, :]], axis=0)
    xp1 = jnp.concatenate([xp[s8:, :], from_next_segment(first, next_rows[0:1, :])], axis=0)
    xp2 = jnp.concatenate([xp[2 * s8:, :], from_next_segment(first, next_rows[0:1, :]),
                           from_next_segment(second, next_rows[1:2, :])], axis=0)
    return (cb_ref[...] + cw_ref[0:1, :] * xm1 + cw_ref[1:2, :] * xp
            + cw_ref[2:3, :] * xp1 + cw_ref[3:4, :] * xp2)


def _lru_gates(d, xc, wd_ref, br_ref, sp_ref, a_ref, u_ref):
    xb = xc.astype(BF16)
    for j in range(R_WIDTH // LRU_TILE):
        sl = slice(j * LRU_TILE, (j + 1) * LRU_TILE)
        xj = xb[:, sl]
        tr = jnp.tanh(jnp.dot(xj, wd_ref[d, 0, j], preferred_element_type=F32) + br_ref[d, 0:1, sl])
        ti = jnp.tanh(jnp.dot(xj, wd_ref[d, 1, j], preferred_element_type=F32) + br_ref[d, 1:2, sl])
        sp = sp_ref[d:d + 1, sl]
        nla = tr * sp + sp
        a = jnp.exp2(nla * (-LOG2E))
        a_ref[:, sl] = a
        xh = 0.5 * xc[:, sl]
        z = jnp.tanh(nla) * (a * a + 1.0)
        root = jnp.where(z > 0.0, z * lax.rsqrt(z), 0.0)
        u_ref[:, sl] = root * (ti * xh + xh)


def _lru_scan(d, a_ref, u_ref, h_ref, pm_ref, out_ref, n):
    rev = d == 1
    s8 = SUBLANES
    groups = n // s8
    sub = lax.broadcasted_iota(jnp.int32, (s8, R_WIDTH), 0)

    def body(g, carry):
        h, acc = carry
        gg = (groups - 1 - g) if rev else g
        r0 = pl.multiple_of(gg * s8, s8)
        a = a_ref[pl.ds(r0, s8), :]
        h = a * h + u_ref[pl.ds(r0, s8), :]
        acc = a * acc
        u_ref[pl.ds(r0, s8), :] = h
        a_ref[pl.ds(r0, s8), :] = acc
        return h, acc

    u, a = lax.fori_loop(0, groups, body, (jnp.zeros((s8, R_WIDTH), F32), jnp.ones((s8, R_WIDTH), F32)))
    k = 1
    while k < s8:
        if rev:
            ok = sub < s8 - k
            a_s = pltpu.roll(a, s8 - k, axis=0)
            u_s = pltpu.roll(u, s8 - k, axis=0)
        else:
            ok = sub >= k
            a_s = pltpu.roll(a, k, axis=0)
            u_s = pltpu.roll(u, k, axis=0)
        u = u + a * jnp.where(ok, u_s, 0.0)
        a = a * jnp.where(ok, a_s, 1.0)
        k *= 2
    c0 = h_ref[d, 0:1, :]
    after = u + a * c0
    if rev:
        entry = jnp.where(sub == s8 - 1, c0, pltpu.roll(after, s8 - 1, axis=0))
        h_ref[d, 0:1, :] = after[0:1, :]
    else:
        entry = jnp.where(sub == 0, c0, pltpu.roll(after, 1, axis=0))
        h_ref[d, 0:1, :] = after[s8 - 1:s8, :]
    hs = u_ref[...] + a_ref[...] * jnp.concatenate([entry] * groups, axis=0)
    out_ref[...] = jnp.dot(pm_ref[1], hs.astype(BF16), preferred_element_type=F32).astype(out_ref.dtype)


def _rglru_kernel(cx_ref, xf_ref, pf_ref, nf_ref, xb_ref, pb_ref, nb_ref,
                  cw_ref, cb_ref, wr_ref, br_ref, lam_ref,
                  ycf_ref, ycb_ref, yf_ref, yb_ref,
                  pm_ref, a_ref, u_ref, h_ref, sp_ref, wd_ref, xc_ref, *, n_lat):
    s = pl.program_id(1)
    seg = LRU_BLK // SUBLANES

    @pl.when(s == 0)
    def _():
        h_ref[...] = jnp.zeros_like(h_ref)
        sp_ref[...] = (0.5 * LRU_C) * jax.nn.softplus(-lam_ref[...])
        wd_ref[...] = jnp.zeros_like(wd_ref)
        per = LRU_TILE // R_BLOCK
        for dd in range(2):
            for g in range(2):
                for blk in range(R_BLOCKS):
                    j, p = divmod(blk, per)
                    rows = slice(p * R_BLOCK, (p + 1) * R_BLOCK)
                    wd_ref[dd, g, j, rows, rows] = (0.5 * wr_ref[dd, g, blk]).astype(BF16)
        row = lax.broadcasted_iota(jnp.int32, (LRU_BLK, LRU_BLK), 0)
        col = lax.broadcasted_iota(jnp.int32, (LRU_BLK, LRU_BLK), 1)
        pm_ref[0] = jnp.where(col == (row % SUBLANES) * seg + row // SUBLANES, 1.0, 0.0).astype(BF16)
        pm_ref[1] = jnp.where(row == (col % SUBLANES) * seg + col // SUBLANES, 1.0, 0.0).astype(BF16)
        zero_rows = jnp.zeros((2, R_WIDTH), F32)
        xc = _lru_conv(cx_ref[...], zero_rows[0:1, :], zero_rows, pm_ref, cw_ref, cb_ref)
        for d, out_ref in ((0, ycf_ref), (1, ycb_ref)):
            _lru_gates(d, xc, wd_ref, br_ref, sp_ref, a_ref, u_ref)
            _lru_scan(d, a_ref, u_ref, h_ref, pm_ref, out_ref, LRU_BLK)

    @pl.when(jnp.logical_and(s > 0, s <= n_lat // 2))
    def _():
        for d, x_ref, p_ref, n_ref in ((0, xf_ref, pf_ref, nf_ref), (1, xb_ref, pb_ref, nb_ref)):
            j = (s - 1) if d == 0 else (n_lat - s)
            has_prev = (j > 0).astype(F32)
            has_next = (j < n_lat - 1).astype(F32)
            for k in range(LRU_PER_STEP):
                lo, hi = k * LRU_BLK, (k + 1) * LRU_BLK
                if k == 0:
                    prev_row = p_ref[...].astype(F32)[BF16_ROWS - 1:BF16_ROWS, :] * has_prev
                else:
                    prev_row = x_ref[lo - BF16_ROWS:lo, :].astype(F32)[BF16_ROWS - 1:BF16_ROWS, :]
                if k == LRU_PER_STEP - 1:
                    next_rows = n_ref[...].astype(F32)[0:2, :] * has_next
                else:
                    next_rows = x_ref[hi:hi + BF16_ROWS, :].astype(F32)[0:2, :]
                xc_ref[j * LRU_PER_STEP + k] = _lru_conv(x_ref[lo:hi, :], prev_row, next_rows, pm_ref, cw_ref, cb_ref)

    @pl.when(s > 0)
    def _():
        for d, out_ref in ((0, yf_ref), (1, yb_ref)):
            j = (s - 1) if d == 0 else (n_lat - s)
            for k in (range(LRU_PER_STEP) if d == 0 else reversed(range(LRU_PER_STEP))):
                _lru_gates(d, xc_ref[j * LRU_PER_STEP + k], wd_ref, br_ref, sp_ref, a_ref, u_ref)
                _lru_scan(d, a_ref, u_ref, h_ref, pm_ref, out_ref.at[pl.ds(k * LRU_BLK, LRU_BLK), :], LRU_BLK)


def _rglru(pc, px, conv_w, conv_b, w_rg, b_rg, lam, layer):
    _, bsz, t, _ = px.shape
    rows_step = LRU_BLK * LRU_PER_STEP
    n_lat = t // rows_step
    per_blk = rows_step // BF16_ROWS
    n_halo = t // BF16_ROWS

    def jf(s):
        return jnp.maximum(s - 1, 0)

    def jb(s):
        return jnp.minimum(n_lat - s, n_lat - 1)

    def jfx(s):
        return jnp.clip(s - 1, 0, n_lat // 2 - 1)

    def jbx(s):
        return jnp.clip(n_lat - s, n_lat // 2, n_lat - 1)

    def lat_specs(jfun):
        return [
            pl.BlockSpec((None, None, rows_step, SEC), lambda b, s: (S16_XL, b, jfun(s), 0)),
            pl.BlockSpec((None, None, BF16_ROWS, SEC),
                         lambda b, s: (S16_XL, b, jnp.maximum(jfun(s) * per_blk - 1, 0), 0)),
            pl.BlockSpec((None, None, BF16_ROWS, SEC),
                         lambda b, s: (S16_XL, b, jnp.minimum((jfun(s) + 1) * per_blk, n_halo - 1), 0)),
        ]

    in_specs = [pl.BlockSpec((None, None, CTX_LEN, SEC), lambda b, s: (S16_XL, b, 0, 0))]
    in_specs += lat_specs(jfx) + lat_specs(jbx) + [
        pl.BlockSpec((None, CONV_W, SEC), lambda b, s: (layer, 0, 0)),
        pl.BlockSpec((None, 1, SEC), lambda b, s: (layer, 0, 0)),
        pl.BlockSpec((None,) + w_rg.shape[1:], lambda b, s: (layer, 0, 0, 0, 0, 0)),
        pl.BlockSpec((None,) + b_rg.shape[1:], lambda b, s: (layer, 0, 0, 0)),
        pl.BlockSpec((None,) + lam.shape[1:], lambda b, s: (layer, 0, 0)),
    ]
    out_specs = [
        pl.BlockSpec((None, CTX_LEN, R_WIDTH), lambda b, s: (b, 0, 0)),
        pl.BlockSpec((None, CTX_LEN, R_WIDTH), lambda b, s: (b, 0, 0)),
        pl.BlockSpec((None, rows_step, R_WIDTH), lambda b, s: (b, jf(s), 0)),
        pl.BlockSpec((None, rows_step, R_WIDTH), lambda b, s: (b, jb(s), 0)),
    ]
    out_shape = [
        jax.ShapeDtypeStruct((bsz, CTX_LEN, R_WIDTH), BF16),
        jax.ShapeDtypeStruct((bsz, CTX_LEN, R_WIDTH), BF16),
        jax.ShapeDtypeStruct((bsz, t, R_WIDTH), BF16),
        jax.ShapeDtypeStruct((bsz, t, R_WIDTH), BF16),
    ]
    return pl.pallas_call(
        functools.partial(_rglru_kernel, n_lat=n_lat),
        grid=(bsz, n_lat + 1),
        in_specs=in_specs,
        out_specs=out_specs,
        out_shape=out_shape,
        scratch_shapes=[
            pltpu.VMEM((2, LRU_BLK, LRU_BLK), BF16),
            pltpu.VMEM((LRU_BLK, R_WIDTH), F32),
            pltpu.VMEM((LRU_BLK, R_WIDTH), F32),
            pltpu.VMEM((2, SUBLANES, R_WIDTH), F32),
            pltpu.VMEM((2, R_WIDTH), F32),
            pltpu.VMEM((2, 2, R_WIDTH // LRU_TILE, LRU_TILE, LRU_TILE), BF16),
            pltpu.VMEM((t // LRU_BLK, LRU_BLK, R_WIDTH), F32),
        ],
        compiler_params=_cparams(("arbitrary", "arbitrary")),
        name="rglru",
    )(pc, px, px, px, px, px, px, conv_w, conv_b, w_rg, b_rg, lam)


def _out_proj_kernel(*refs, final, split_cols):
    if split_cols:
        hlo_ref, hhi_ref = refs[:2]
        hm = jnp.concatenate([hlo_ref[...], hhi_ref[...]], axis=1)
        hm = hm.reshape(hm.shape[0] * hm.shape[1], M_WIDTH)
        refs = refs[2:]
    else:
        hm = refs[0][...]
        refs = refs[1:]
    yf_ref, yb_ref, o_ref, zm_ref, zl_ref, x_ref, gt_ref, mg_ref, w_ref, fg_ref, out_ref = refs
    parts = []
    for h in range(M_HEADS):
        hh = hm[:, h * M_DV:(h + 1) * M_DV]
        parts.append(hh * lax.rsqrt(jnp.mean(hh * hh, axis=-1, keepdims=True) + EPS))
    hn = jnp.concatenate(parts, axis=1) * mg_ref[...]
    ym = hn * _sigmoid_t(o_ref[...].astype(F32)) * _silu_t(zm_ref[...].astype(F32))
    yr = (yf_ref[...].astype(F32) + yb_ref[...].astype(F32)) * _silu_t(zl_ref[...].astype(F32))
    y = jnp.concatenate([ym, yr], axis=1).astype(BF16)
    xn = x_ref[...] + gt_ref[...] * jnp.dot(y, w_ref[...], preferred_element_type=F32)
    if final:
        xn = xn * lax.rsqrt(jnp.mean(xn * xn, axis=-1, keepdims=True) + EPS) * fg_ref[...]
    out_ref[...] = xn


def _out_proj(h_parts, yf, yb, p16, x2d, mod_l, row_of_tile, m_norm_g, w_out, final_g, layer, tm, final):
    m, d = x2d.shape

    def tok(width):
        return pl.BlockSpec((tm, width), lambda i: (i, 0))

    def sec(k):
        return pl.BlockSpec((None, tm, SEC), lambda i: (k, i, 0))

    split_cols = len(h_parts) == 2
    if split_cols:
        h_specs = [pl.BlockSpec((tm // GRID_W, GRID_W // 2, M_WIDTH), lambda i: (i, 0, 0))] * 2
    else:
        h_specs = [tok(M_WIDTH)]
    return pl.pallas_call(
        functools.partial(_out_proj_kernel, final=final, split_cols=split_cols),
        grid=(m // tm,),
        in_specs=h_specs + [
            tok(R_WIDTH), tok(R_WIDTH),
            sec(S16_O), sec(S16_ZM), sec(S16_ZL),
            tok(d),
            pl.BlockSpec((None, None, 1, d), lambda i: (row_of_tile(i), 2, 0, 0)),
            pl.BlockSpec((None, 1, M_WIDTH), lambda i: (layer, 0, 0)),
            pl.BlockSpec((None, d, d), lambda i: (layer, 0, 0), pipeline_mode=pl.Buffered(1)),
            pl.BlockSpec((1, d), lambda i: (0, 0)),
        ],
        out_specs=tok(d),
        out_shape=jax.ShapeDtypeStruct((m, d), F32),
        compiler_params=_cparams(("arbitrary",)),
        name="out_proj",
    )(*h_parts, yf, yb, p16, p16, p16, x2d, mod_l, m_norm_g, w_out, final_g)


def kernel(x, c, ctx, c_ctx, w_mod, b_mod, norm_g, w_in, b_gate, conv_qk_w, conv_qk_b, m_norm_g, conv_r_w, conv_r_b,
           w_rg, b_rg, lru_lambda, w_out, final_g):
    bsz, t, d = x.shape
    depth = w_mod.shape[0]
    nh = M_HEADS

    w_t = jnp.swapaxes(w_in, 1, 2)
    w_main, w_gate = _win_cast(w_t, 0, WA_SECS * SEC, 4 * nh)
    b_mod3 = b_mod[:, None, :]
    bg2 = jnp.zeros((depth, SUBLANES, GATE_W), F32).at[:, 0, :4 * nh].set(b_gate)
    w_out_b = _cast_bf16(w_out)
    norm_g3 = norm_g[:, None, :]
    m_norm_g3 = m_norm_g[:, None, :]
    conv_qk_b3 = conv_qk_b[:, None, :]
    conv_r_b3 = conv_r_b[:, None, :]
    b_rg4 = 0.5 * b_rg.reshape(depth, 2, 2, R_WIDTH)
    fg = final_g[None, :]

    cvec = jnp.concatenate([c, c_ctx[None, :], jnp.zeros((SUBLANES - bsz - 1, d), F32)], axis=0)
    mod = _modulation(cvec, w_mod, b_mod3, 0)

    tm_in = 1024
    tm_ctx = bsz * CTX_LEN
    tm_out = 512
    x2d = x.reshape(bsz * t, d)
    c2d = ctx.reshape(bsz * CTX_LEN, d)
    lat_row_in = lambda i: i // (t // tm_in)
    lat_row_out = lambda i: i // (t // tm_out)
    ctx_row = lambda i: bsz

    for l in range(depth):
        last = l == depth - 1
        mod_l = mod.reshape(SUBLANES, 3, 1, d)
        nxt = None if last else (cvec, w_mod, b_mod3, w_t, 4 * nh)
        lat = _in_proj(x2d, mod_l, lat_row_in, norm_g3, w_main, w_gate, l, tm_in, nxt)
        px32, px16, gx = lat[:3]
        pc32, pc16, gc = _in_proj(c2d, mod_l, ctx_row, norm_g3, w_main, w_gate, l, tm_ctx)
        if not last:
            mod, w_main, w_gate = lat[3:]
        hc, hhi, hlo = _mlstm(pc32.reshape(N_SEC32, bsz, CTX_LEN, SEC), gc.reshape(bsz, CTX_LEN, GATE_W),
                              px32.reshape(N_SEC32, bsz, t, SEC), gx.reshape(bsz, t, GATE_W),
                              conv_qk_w, conv_qk_b3, bg2, l)
        grid_rows = bsz * t // GRID_W
        ycf, ycb, yf, yb = _rglru(pc16.reshape(N_SEC - N_SEC32, bsz, CTX_LEN, SEC),
                                  px16.reshape(N_SEC - N_SEC32, bsz, t, SEC),
                                  conv_r_w, conv_r_b3, w_rg, b_rg4, lru_lambda, l)
        x2d = _out_proj((hlo.reshape(grid_rows, GRID_W // 2, M_WIDTH), hhi.reshape(grid_rows, GRID_W // 2, M_WIDTH)),
                        yf.reshape(bsz * t, R_WIDTH), yb.reshape(bsz * t, R_WIDTH),
                        px16, x2d, mod_l, lat_row_out, m_norm_g3, w_out_b, fg, l, tm_out, last)
        if not last:
            c2d = _out_proj((hc.reshape(bsz * CTX_LEN, M_WIDTH),),
                            ycf.reshape(bsz * CTX_LEN, R_WIDTH), ycb.reshape(bsz * CTX_LEN, R_WIDTH),
                            pc16, c2d, mod_l, ctx_row, m_norm_g3, w_out_b, fg, l, tm_out, False)
    return x2d.reshape(bsz, t, d)
```

```python
import functools

import jax
import jax.numpy as jnp
from jax import lax
from jax.experimental import pallas as pl
from jax.experimental.pallas import tpu as pltpu

D_MODEL = 2048
DEPTH = 4
CTX_LEN = 256
GRID_W = 64
M_WIDTH = 1024
R_WIDTH = 1024
M_HEADS = 4
M_DV = 256
M_DQK = 128
M_QK = 512
R_BLOCKS = 16
R_BLOCK = 64
CONV_W = 4
LRU_C = 8.0
EPS = 1e-6

LANES = 128
SUBLANES = 8
BF16_ROWS = 16
SEC = 1024
N_SEC = 6
N_SEC32 = 2
S32_QK, S32_V = 0, 1
S16_O, S16_ZM, S16_XL, S16_ZL = 0, 1, 2, 3
WA_SECS = 4
GATE_W = LANES
CTX_CHUNK = CTX_LEN
COL_GROUP = SUBLANES
LAT_CHUNK = COL_GROUP * GRID_W
LRU_BLK = 256
LRU_PER_STEP = 4
LRU_TILE = 256
VMEM_LIMIT = 56 * 1024 * 1024
LOG2E = 1.4426950408889634

F32 = jnp.float32
BF16 = jnp.bfloat16


def _cparams(sem):
    return pltpu.CompilerParams(dimension_semantics=sem, vmem_limit_bytes=VMEM_LIMIT)


def _silu_t(x):
    h = 0.5 * x
    return h + h * jnp.tanh(h)


def _dot_nt(a, b):
    return lax.dot_general(a, b, (((1,), (1,)), ((), ())), preferred_element_type=F32)


def _sigmoid_t(x):
    return 0.5 * jnp.tanh(0.5 * x) + 0.5


def _mod_kernel(c_ref, w_ref, b_ref, o_ref):
    c = c_ref[...]
    s = (c * jax.nn.sigmoid(c)).astype(BF16)
    o_ref[...] = jnp.dot(s, w_ref[...].astype(BF16), preferred_element_type=F32) + b_ref[...]


def _modulation(cvec, w_mod, b_mod3, layer):
    _, d, n = w_mod.shape
    tn = 1024
    return pl.pallas_call(
        _mod_kernel,
        grid=(n // tn,),
        in_specs=[
            pl.BlockSpec((SUBLANES, d), lambda j: (0, 0)),
            pl.BlockSpec((None, d, tn), lambda j: (layer, 0, j)),
            pl.BlockSpec((None, 1, tn), lambda j: (layer, 0, j)),
        ],
        out_specs=pl.BlockSpec((SUBLANES, tn), lambda j: (0, j)),
        out_shape=jax.ShapeDtypeStruct((SUBLANES, n), F32),
        compiler_params=_cparams(("arbitrary",)),
        name="mod",
    )(cvec, w_mod, b_mod3)


def _cast_rows(r, a_ref, b_ref, o_ref, g_ref, n_direct, skip):
    tr = o_ref.shape[0]

    @pl.when(r < n_direct)
    def _():
        o_ref[...] = a_ref[...].astype(BF16)

    @pl.when(r >= n_direct)
    def _():
        o_ref[:tr - skip, :] = a_ref[skip:, :].astype(BF16)
        o_ref[tr - skip:, :] = b_ref[...].astype(BF16)

    @pl.when(r == n_direct)
    def _():
        g_ref[0:skip, :] = a_ref[0:skip, :].astype(BF16)
        g_ref[skip:, :] = jnp.zeros((g_ref.shape[0] - skip, g_ref.shape[1]), BF16)


def _in_proj_kernel(x_ref, sh_ref, sc_ref, g_ref, w_ref, wg_ref, *refs, prep):
    n = pl.program_id(1)
    if prep is None:
        p32_ref, p16_ref, gate_ref, h_scr = refs

        def prep_step():
            pass
    else:
        c_ref, wm_ref, bm_ref, wa_ref, wb_ref, p32_ref, p16_ref, gate_ref, modn_ref, wn_ref, wgn_ref, h_scr = refs
        n_direct, skip = prep
        k = pl.program_id(0) * N_SEC + n

        @pl.when(k == n_direct)
        def _():
            wgn_ref[0:skip, :] = wa_ref[0:skip, :].astype(BF16)
            wgn_ref[skip:, :] = jnp.zeros((wgn_ref.shape[0] - skip, wgn_ref.shape[1]), BF16)

        def prep_step():
            c = c_ref[...]
            modn_ref[...] = jnp.dot((c * jax.nn.sigmoid(c)).astype(BF16), wm_ref[...].astype(BF16),
                                    preferred_element_type=F32) + bm_ref[...]
            a = wa_ref[...]
            below = jnp.concatenate([a[skip:, :], wb_ref[...]], axis=0)
            wn_ref[...] = jnp.where(k < n_direct, a, below).astype(BF16)

    @pl.when(n == 0)
    def _():
        x = x_ref[...]
        ms = jnp.mean(x * x, axis=-1, keepdims=True)
        y = x * lax.rsqrt(ms + EPS) * g_ref[...]
        h = (y * (1.0 + sc_ref[...]) + sh_ref[...]).astype(BF16)
        h_scr[...] = h
        gate_ref[...] = _dot_nt(h, wg_ref[...])

    @pl.when(n < N_SEC32)
    def _():
        prep_step()
        p32_ref[...] = _dot_nt(h_scr[...], w_ref[...])

    @pl.when(n >= N_SEC32)
    def _():
        prep_step()
        p16_ref[...] = _dot_nt(h_scr[...], w_ref[...]).astype(BF16)


def _in_proj(x2d, mod_l, row_of_tile, norm_g, w_main, w_gate, layer, tm, nxt=None):
    m, d = x2d.shape
    n_tiles = m // tm
    in_specs = [
        pl.BlockSpec((tm, d), lambda i, n: (i, 0)),
        pl.BlockSpec((None, None, 1, d), lambda i, n: (row_of_tile(i), 0, 0, 0)),
        pl.BlockSpec((None, None, 1, d), lambda i, n: (row_of_tile(i), 1, 0, 0)),
        pl.BlockSpec((None, 1, d), lambda i, n: (layer, 0, 0)),
        pl.BlockSpec((SEC, d), lambda i, n: (n, 0)),
        pl.BlockSpec((GATE_W, d), lambda i, n: (0, 0)),
    ]
    out_specs = [
        pl.BlockSpec((None, tm, SEC), lambda i, n: (jnp.minimum(n, N_SEC32 - 1), i, 0)),
        pl.BlockSpec((None, tm, SEC), lambda i, n: (jnp.maximum(n - N_SEC32, 0), i, 0)),
        pl.BlockSpec((tm, GATE_W), lambda i, n: (i, 0)),
    ]
    out_shape = [
        jax.ShapeDtypeStruct((N_SEC32, m, SEC), F32),
        jax.ShapeDtypeStruct((N_SEC - N_SEC32, m, SEC), BF16),
        jax.ShapeDtypeStruct((m, GATE_W), F32),
    ]
    operands = [x2d, mod_l, mod_l, norm_g, w_main, w_gate]
    prep = None
    if nxt is not None:
        cvec, w_mod, b_mod3, w_t, skip = nxt
        steps = n_tiles * N_SEC
        n_mod = w_mod.shape[2]
        tr, tn = (N_SEC * SEC) // steps, n_mod // steps
        assert tr * steps == N_SEC * SEC and tr % skip == 0 and tn * steps == n_mod and tn % LANES == 0
        prep = (WA_SECS * SEC // tr, skip)

        def step(i, n):
            return i * N_SEC + n

        in_specs += [
            pl.BlockSpec((SUBLANES, d), lambda i, n: (0, 0)),
            pl.BlockSpec((None, d, tn), lambda i, n: (layer + 1, 0, step(i, n))),
            pl.BlockSpec((None, 1, tn), lambda i, n: (layer + 1, 0, step(i, n))),
            pl.BlockSpec((None, tr, d), lambda i, n: (layer + 1, step(i, n), 0)),
            pl.BlockSpec((None, skip, d), lambda i, n: (layer + 1, (step(i, n) + 1) * (tr // skip), 0)),
        ]
        out_specs += [
            pl.BlockSpec((SUBLANES, tn), lambda i, n: (0, step(i, n))),
            pl.BlockSpec((tr, d), lambda i, n: (step(i, n), 0)),
            pl.BlockSpec((GATE_W, d), lambda i, n: (0, 0)),
        ]
        out_shape += [
            jax.ShapeDtypeStruct((SUBLANES, n_mod), F32),
            jax.ShapeDtypeStruct((N_SEC * SEC, d), BF16),
            jax.ShapeDtypeStruct((GATE_W, d), BF16),
        ]
        operands += [cvec, w_mod, b_mod3, w_t, w_t]
    return pl.pallas_call(
        functools.partial(_in_proj_kernel, prep=prep),
        grid=(n_tiles, N_SEC),
        in_specs=in_specs,
        out_specs=out_specs,
        out_shape=out_shape,
        scratch_shapes=[pltpu.VMEM((tm, d), BF16)],
        compiler_params=_cparams(("arbitrary", "arbitrary")),
        name="in_proj",
    )(*operands)


def _win_cast_kernel(a_ref, b_ref, o_ref, g_ref, *, n_direct, skip):
    _cast_rows(pl.program_id(0), a_ref, b_ref, o_ref, g_ref, n_direct, skip)


def _win_cast(w_t, layer, split, skip):
    _, n_in, d = w_t.shape
    tr = 256
    return pl.pallas_call(
        functools.partial(_win_cast_kernel, n_direct=split // tr, skip=skip),
        grid=((n_in - skip) // tr,),
        in_specs=[pl.BlockSpec((None, tr, d), lambda r: (layer, r, 0)),
                  pl.BlockSpec((None, skip, d), lambda r: (layer, (r + 1) * (tr // skip), 0))],
        out_specs=[pl.BlockSpec((tr, d), lambda r: (r, 0)),
                   pl.BlockSpec((GATE_W, d), lambda r: (0, 0))],
        out_shape=[jax.ShapeDtypeStruct((n_in - skip, d), BF16),
                   jax.ShapeDtypeStruct((GATE_W, d), BF16)],
        compiler_params=_cparams(("arbitrary",)),
        name="win_cast",
    )(w_t, w_t)


def _cast_kernel(w_ref, o_ref):
    o_ref[...] = w_ref[...].astype(BF16)


def _cast_bf16(w):
    depth, r, c = w.shape
    tr = 512
    return pl.pallas_call(
        _cast_kernel,
        grid=(depth, r // tr),
        in_specs=[pl.BlockSpec((None, tr, c), lambda l, j: (l, j, 0))],
        out_specs=pl.BlockSpec((None, tr, c), lambda l, j: (l, j, 0)),
        out_shape=jax.ShapeDtypeStruct((depth, r, c), BF16),
        compiler_params=_cparams(("arbitrary", "arbitrary")),
        name="cast_bf16",
    )(w)


def _shifted(x, k, reverse, ident):
    n = x.shape[0]
    row = lax.broadcasted_iota(jnp.int32, x.shape, 0)
    if reverse:
        return jnp.where(row < n - k, pltpu.roll(x, n - k, axis=0), ident)
    return jnp.where(row >= k, pltpu.roll(x, k, axis=0), ident)


def _scan_rows(x, op, ident, reverse):
    n = x.shape[0]
    k = 1
    while k < n:
        x = op(x, _shifted(x, k, reverse, ident))
        k *= 2
    return x


def _scan_colmajor(x, op, ident, reverse):
    n = x.shape[0]
    k = SUBLANES
    while k < n:
        x = op(x, _shifted(x, k, reverse, ident))
        k *= 2
    tot = x[0:SUBLANES, :] if reverse else x[n - SUBLANES:n, :]
    tot = _scan_rows(tot, op, ident, reverse)
    tot = _shifted(tot, 1, reverse, ident)
    return op(x, jnp.concatenate([tot] * (n // SUBLANES), axis=0))


def _conv_rows(xs_ref, n, w_ref, b_ref, step):
    base = SUBLANES - step
    acc = b_ref[...] + w_ref[0:1, :] * xs_ref[base:base + n, :]
    for j in range(1, CONV_W):
        acc = acc + w_ref[j:j + 1, :] * xs_ref[base + j * step:base + j * step + n, :]
    return acc


def _mlstm_dir(d, qk, v, gates, bg_ref, ct_ref, m_ref, scan, posdiff, write_h):
    n = qk.shape[0]
    rev = d == 1
    gi = gates + bg_ref[0:1, :]
    lf = jax.nn.log_sigmoid(pltpu.roll(gi, LANES - M_HEADS, axis=1))
    bc = scan(lf, jnp.add, 0.0, rev)
    a = gi - bc
    m_prev = m_ref[d, 0:1, :]
    mm = jnp.maximum(scan(a, jnp.maximum, -jnp.inf, rev), m_prev)
    inter = jnp.exp(m_prev - mm)
    em = jnp.exp(-(bc + mm))
    last = 0 if rev else n - 1
    mm_last = mm[last:last + 1, :]
    m_new = bc[last:last + 1, :] + mm_last
    decay = jnp.exp(m_prev - mm_last)
    wcol = jnp.exp(a - mm_last)
    a_t = (a * LOG2E).T
    mm2 = mm * LOG2E
    mask = (posdiff <= 0) if rev else (posdiff >= 0)
    ones = jnp.ones((n, LANES), BF16)
    for h in range(M_HEADS):
        e = 2 * M_HEADS * d + h
        st = d * M_HEADS + h
        qf = qk[:, h * M_DQK:(h + 1) * M_DQK]
        q = qf.astype(BF16)
        kf = qk[:, M_QK + h * M_DQK:M_QK + (h + 1) * M_DQK] * (M_DQK ** -0.5)
        vaug = jnp.concatenate([v[:, h * M_DV:(h + 1) * M_DV].astype(BF16), ones], axis=1)
        dmat = jnp.where(mask, jnp.exp2(a_t[e:e + 1, :] - mm2[:, e:e + 1]), 0.0)
        s = lax.dot_general(q, kf.astype(BF16), (((1,), (1,)), ((), ())), preferred_element_type=F32)
        sw = (s * dmat).astype(BF16)
        ct = ct_ref[st]
        qi = (qf * inter[:, e:e + 1]).astype(BF16)
        num = jnp.dot(jnp.concatenate([sw, qi], axis=1), jnp.concatenate([vaug, ct.astype(BF16)], axis=0),
                      preferred_element_type=F32)
        den = jnp.maximum(jnp.abs(num[:, M_DV:]), em[:, e:e + 1])
        write_h(h, num[:, :M_DV] / jnp.concatenate([den, den], axis=1))
        kw = (kf * wcol[:, e:e + 1]).astype(BF16)
        upd = lax.dot_general(kw, vaug, (((0,), (0,)), ((), ())), preferred_element_type=F32)
        ct_ref[st] = decay[:, e:e + 1] * ct + upd
    m_ref[d, 0:1, :] = m_new


def _mlstm_kernel(cqk_ref, cv_ref, cg_ref,
                  qf_ref, pf_ref, nf_ref, vf_ref, gf_ref,
                  qb_ref, pb_ref, nb_ref, vb_ref, gb_ref,
                  cw_ref, cb_ref, bg_ref,
                  hc_ref, hhi_ref, hlo_ref,
                  xs_ref, ct_ref, m_ref, pd_ref, park_ref, *, n_lat):
    s = pl.program_id(1)
    lc = LAT_CHUNK
    zero_rows = jnp.zeros((SUBLANES, 2 * M_QK), F32)

    @pl.when(s == 0)
    def _():
        ct_ref[...] = jnp.zeros_like(ct_ref)
        m_ref[...] = jnp.zeros_like(m_ref)
        park_ref[...] = jnp.zeros_like(park_ref)
        row = lax.broadcasted_iota(jnp.int32, (lc, lc), 0)
        col = lax.broadcasted_iota(jnp.int32, (lc, lc), 1)
        pos_r = (row % COL_GROUP) * GRID_W + row // COL_GROUP
        pos_c = (col % COL_GROUP) * GRID_W + col // COL_GROUP
        pd_ref[...] = pos_r - pos_c
        xs_ref[0:SUBLANES, :] = zero_rows
        xs_ref[SUBLANES + CTX_CHUNK:2 * SUBLANES + CTX_CHUNK, :] = zero_rows
        xs_ref[SUBLANES:SUBLANES + CTX_CHUNK, :] = cqk_ref[...]
        qk = _silu_t(_conv_rows(xs_ref, CTX_CHUNK, cw_ref, cb_ref, 1))
        v = cv_ref[...]
        g = cg_ref[...]
        crow = lax.broadcasted_iota(jnp.int32, (CTX_CHUNK, CTX_CHUNK), 0)
        ccol = lax.broadcasted_iota(jnp.int32, (CTX_CHUNK, CTX_CHUNK), 1)
        for d in (0, 1):
            def write_h(h, val, d=d):
                lanes = slice(h * M_DV, (h + 1) * M_DV)
                hc_ref[:, lanes] = val if d == 0 else hc_ref[:, lanes] + val
            _mlstm_dir(d, qk, v, g, bg_ref, ct_ref, m_ref, _scan_rows, crow - ccol, write_h)

    @pl.when(s > 0)
    def _():
        sub = lax.broadcasted_iota(jnp.int32, (SUBLANES, 2 * M_QK), 0)
        for d, q_ref, p_ref, n_ref, v_ref, g_ref, out_ref in (
                (0, qf_ref, pf_ref, nf_ref, vf_ref, gf_ref, hhi_ref),
                (1, qb_ref, pb_ref, nb_ref, vb_ref, gb_ref, hlo_ref)):
            j = (s - 1) if d == 0 else (n_lat - s)
            has_prev = (j > 0).astype(F32)
            has_next = (j < n_lat - 1).astype(F32)
            x = q_ref[...].reshape(lc, 2 * M_QK)
            xs_ref[SUBLANES:SUBLANES + lc, :] = x
            x_last = x[lc - SUBLANES:lc, :]
            xs_ref[0:SUBLANES, :] = jnp.where(sub == 0, pltpu.roll(p_ref[...], 1, axis=0) * has_prev,
                                              pltpu.roll(x_last, 1, axis=0))
            for k in range(2):
                xs_ref[SUBLANES + lc + k * SUBLANES:2 * SUBLANES + lc + k * SUBLANES, :] = jnp.where(
                    sub == SUBLANES - 1, pltpu.roll(n_ref[k], SUBLANES - 1, axis=0) * has_next,
                    pltpu.roll(x[k * SUBLANES:(k + 1) * SUBLANES, :], SUBLANES - 1, axis=0))
            qk = _silu_t(_conv_rows(xs_ref, lc, cw_ref, cb_ref, SUBLANES))
            v = v_ref[...].reshape(lc, M_WIDTH)
            g = g_ref[...].reshape(lc, GATE_W)

            def write_h(h, val, out_ref=out_ref, j=j):
                lanes = slice(h * M_DV, (h + 1) * M_DV)
                other = park_ref[j, :, lanes].astype(F32)
                park_ref[j, :, lanes] = val.astype(BF16)
                out_ref[:, :, lanes] = (val + other).reshape(GRID_W, COL_GROUP, M_DV)
            _mlstm_dir(d, qk, v, g, bg_ref, ct_ref, m_ref, _scan_colmajor, pd_ref[...], write_h)


def _mlstm(pc, gc, px, gx, conv_w, conv_b, bg2, layer):
    _, bsz, t, _ = px.shape
    rows = t // GRID_W
    n_lat = GRID_W // COL_GROUP
    pxv = px.reshape(N_SEC32, bsz, rows, GRID_W, SEC)
    gxv = gx.reshape(bsz, rows, GRID_W, GATE_W)

    def jf(s):
        return jnp.maximum(s - 1, 0)

    def jb(s):
        return jnp.minimum(n_lat - s, n_lat - 1)

    def lat_specs(jfun):
        return [
            pl.BlockSpec((None, None, rows, COL_GROUP, SEC), lambda b, s: (S32_QK, b, 0, jfun(s), 0)),
            pl.BlockSpec((None, None, None, COL_GROUP, SEC),
                         lambda b, s: (S32_QK, b, rows - 1, jnp.maximum(jfun(s) - 1, 0), 0)),
            pl.BlockSpec((None, None, 2, COL_GROUP, SEC),
                         lambda b, s: (S32_QK, b, 0, jnp.minimum(jfun(s) + 1, n_lat - 1), 0)),
            pl.BlockSpec((None, None, rows, COL_GROUP, SEC), lambda b, s: (S32_V, b, 0, jfun(s), 0)),
            pl.BlockSpec((None, rows, COL_GROUP, GATE_W), lambda b, s: (b, 0, jfun(s), 0)),
        ]

    in_specs = [
        pl.BlockSpec((None, None, CTX_LEN, SEC), lambda b, s: (S32_QK, b, 0, 0)),
        pl.BlockSpec((None, None, CTX_LEN, SEC), lambda b, s: (S32_V, b, 0, 0)),
        pl.BlockSpec((None, CTX_LEN, GATE_W), lambda b, s: (b, 0, 0)),
    ] + lat_specs(jf) + lat_specs(jb) + [
        pl.BlockSpec((None, CONV_W, SEC), lambda b, s: (layer, 0, 0)),
        pl.BlockSpec((None, 1, SEC), lambda b, s: (layer, 0, 0)),
        pl.BlockSpec((None, SUBLANES, LANES), lambda b, s: (layer, 0, 0)),
    ]
    half = n_lat // 2
    out_specs = [
        pl.BlockSpec((None, CTX_LEN, M_WIDTH), lambda b, s: (b, 0, 0)),
        pl.BlockSpec((None, rows, COL_GROUP, M_WIDTH), lambda b, s: (b, 0, jnp.clip(s - 1 - half, 0, half - 1), 0)),
        pl.BlockSpec((None, rows, COL_GROUP, M_WIDTH), lambda b, s: (b, 0, jnp.clip(n_lat - s, 0, half - 1), 0)),
    ]
    out_shape = [
        jax.ShapeDtypeStruct((bsz, CTX_LEN, M_WIDTH), F32),
        jax.ShapeDtypeStruct((bsz, rows, GRID_W // 2, M_WIDTH), F32),
        jax.ShapeDtypeStruct((bsz, rows, GRID_W // 2, M_WIDTH), F32),
    ]
    return pl.pallas_call(
        functools.partial(_mlstm_kernel, n_lat=n_lat),
        grid=(bsz, n_lat + 1),
        in_specs=in_specs,
        out_specs=out_specs,
        out_shape=out_shape,
        scratch_shapes=[
            pltpu.VMEM((LAT_CHUNK + 3 * SUBLANES, 2 * M_QK), F32),
            pltpu.VMEM((2 * M_HEADS, M_DQK, M_DV + LANES), F32),
            pltpu.VMEM((2, SUBLANES, LANES), F32),
            pltpu.VMEM((LAT_CHUNK, LAT_CHUNK), jnp.int32),
            pltpu.VMEM((n_lat, LAT_CHUNK, M_WIDTH), BF16),
        ],
        compiler_params=_cparams(("arbitrary", "arbitrary")),
        name="mlstm",
    )(pc, pc, gc, pxv, pxv, pxv, pxv, gxv, pxv, pxv, pxv, pxv, gxv, conv_w, conv_b, bg2)


def _lru_conv(x_bf, prev_row, next_rows, pm_ref, cw_ref, cb_ref):
    n = x_bf.shape[0]
    s8 = SUBLANES
    xp = jnp.dot(pm_ref[0], x_bf, preferred_element_type=F32)
    sub = lax.broadcasted_iota(jnp.int32, (s8, x_bf.shape[1]), 0)

    def from_next_segment(v, fill):
        return jnp.where(sub == s8 - 1, fill, pltpu.roll(v, s8 - 1, axis=0))

    def from_prev_segment(v, fill):
        return jnp.where(sub == 0, fill, pltpu.roll(v, 1, axis=0))

    first, second, last = xp[0:s8, :], xp[s8:2 * s8, :], xp[n - s8:, :]
    xm1 = jnp.concatenate([from_prev_segment(last, prev_row), xp[:n - s8, :]], axis=0)
    xp1 = jnp.concatenate([xp[s8:, :], from_next_segment(first, next_rows[0:1, :])], axis=0)
    xp2 = jnp.concatenate([xp[2 * s8:, :], from_next_segment(first, next_rows[0:1, :]),
                           from_next_segment(second, next_rows[1:2, :])], axis=0)
    return (cb_ref[...] + cw_ref[0:1, :] * xm1 + cw_ref[1:2, :] * xp
            + cw_ref[2:3, :] * xp1 + cw_ref[3:4, :] * xp2)


def _lru_gates(d, xc, wd_ref, br_ref, sp_ref, a_ref, u_ref):
    xb = xc.astype(BF16)
    for j in range(R_WIDTH // LRU_TILE):
        sl = slice(j * LRU_TILE, (j + 1) * LRU_TILE)
        xj = xb[:, sl]
        tr = jnp.tanh(jnp.dot(xj, wd_ref[d, 0, j], preferred_element_type=F32) + br_ref[d, 0:1, sl])
        ti = jnp.tanh(jnp.dot(xj, wd_ref[d, 1, j], preferred_element_type=F32) + br_ref[d, 1:2, sl])
        sp = sp_ref[d:d + 1, sl]
        nla = tr * sp + sp
        a = jnp.exp2(nla * (-LOG2E))
        a_ref[:, sl] = a
        xh = 0.5 * xc[:, sl]
        z = jnp.tanh(nla) * (a * a + 1.0)
        root = jnp.where(z > 0.0, z * lax.rsqrt(z), 0.0)
        u_ref[:, sl] = root * (ti * xh + xh)


def _lru_scan(d, a_ref, u_ref, h_ref, pm_ref, out_ref, n):
    rev = d == 1
    s8 = SUBLANES
    groups = n // s8
    sub = lax.broadcasted_iota(jnp.int32, (s8, R_WIDTH), 0)

    def body(g, carry):
        h, acc = carry
        gg = (groups - 1 - g) if rev else g
        r0 = pl.multiple_of(gg * s8, s8)
        a = a_ref[pl.ds(r0, s8), :]
        h = a * h + u_ref[pl.ds(r0, s8), :]
        acc = a * acc
        u_ref[pl.ds(r0, s8), :] = h
        a_ref[pl.ds(r0, s8), :] = acc
        return h, acc

    u, a = lax.fori_loop(0, groups, body, (jnp.zeros((s8, R_WIDTH), F32), jnp.ones((s8, R_WIDTH), F32)))
    k = 1
    while k < s8:
        if rev:
            ok = sub < s8 - k
            a_s = pltpu.roll(a, s8 - k, axis=0)
            u_s = pltpu.roll(u, s8 - k, axis=0)
        else:
            ok = sub >= k
            a_s = pltpu.roll(a, k, axis=0)
            u_s = pltpu.roll(u, k, axis=0)
        u = u + a * jnp.where(ok, u_s, 0.0)
        a = a * jnp.where(ok, a_s, 1.0)
        k *= 2
    c0 = h_ref[d, 0:1, :]
    after = u + a * c0
    if rev:
        entry = jnp.where(sub == s8 - 1, c0, pltpu.roll(after, s8 - 1, axis=0))
        h_ref[d, 0:1, :] = after[0:1, :]
    else:
        entry = jnp.where(sub == 0, c0, pltpu.roll(after, 1, axis=0))
        h_ref[d, 0:1, :] = after[s8 - 1:s8, :]
    hs = u_ref[...] + a_ref[...] * jnp.concatenate([entry] * groups, axis=0)
    out_ref[...] = jnp.dot(pm_ref[1], hs.astype(BF16), preferred_element_type=F32).astype(out_ref.dtype)


def _rglru_kernel(cx_ref, xf_ref, pf_ref, nf_ref, xb_ref, pb_ref, nb_ref,
                  cw_ref, cb_ref, wr_ref, br_ref, lam_ref,
                  ycf_ref, ycb_ref, yf_ref, yb_ref,
                  pm_ref, a_ref, u_ref, h_ref, sp_ref, wd_ref, xc_ref, *, n_lat):
    s = pl.program_id(1)
    seg = LRU_BLK // SUBLANES

    @pl.when(s == 0)
    def _():
        h_ref[...] = jnp.zeros_like(h_ref)
        sp_ref[...] = (0.5 * LRU_C) * jax.nn.softplus(-lam_ref[...])
        wd_ref[...] = jnp.zeros_like(wd_ref)
        per = LRU_TILE // R_BLOCK
        for dd in range(2):
            for g in range(2):
                for blk in range(R_BLOCKS):
                    j, p = divmod(blk, per)
                    rows = slice(p * R_BLOCK, (p + 1) * R_BLOCK)
                    wd_ref[dd, g, j, rows, rows] = (0.5 * wr_ref[dd, g, blk]).astype(BF16)
        row = lax.broadcasted_iota(jnp.int32, (LRU_BLK, LRU_BLK), 0)
        col = lax.broadcasted_iota(jnp.int32, (LRU_BLK, LRU_BLK), 1)
        pm_ref[0] = jnp.where(col == (row % SUBLANES) * seg + row // SUBLANES, 1.0, 0.0).astype(BF16)
        pm_ref[1] = jnp.where(row == (col % SUBLANES) * seg + col // SUBLANES, 1.0, 0.0).astype(BF16)
        zero_rows = jnp.zeros((2, R_WIDTH), F32)
        xc = _lru_conv(cx_ref[...], zero_rows[0:1, :], zero_rows, pm_ref, cw_ref, cb_ref)
        for d, out_ref in ((0, ycf_ref), (1, ycb_ref)):
            _lru_gates(d, xc, wd_ref, br_ref, sp_ref, a_ref, u_ref)
            _lru_scan(d, a_ref, u_ref, h_ref, pm_ref, out_ref, LRU_BLK)

    @pl.when(jnp.logical_and(s > 0, s <= n_lat // 2))
    def _():
        for d, x_ref, p_ref, n_ref in ((0, xf_ref, pf_ref, nf_ref), (1, xb_ref, pb_ref, nb_ref)):
            j = (s - 1) if d == 0 else (n_lat - s)
            has_prev = (j > 0).astype(F32)
            has_next = (j < n_lat - 1).astype(F32)
            for k in range(LRU_PER_STEP):
                lo, hi = k * LRU_BLK, (k + 1) * LRU_BLK
                if k == 0:
                    prev_row = p_ref[...].astype(F32)[BF16_ROWS - 1:BF16_ROWS, :] * has_prev
                else:
                    prev_row = x_ref[lo - BF16_ROWS:lo, :].astype(F32)[BF16_ROWS - 1:BF16_ROWS, :]
                if k == LRU_PER_STEP - 1:
                    next_rows = n_ref[...].astype(F32)[0:2, :] * has_next
                else:
                    next_rows = x_ref[hi:hi + BF16_ROWS, :].astype(F32)[0:2, :]
                xc_ref[j * LRU_PER_STEP + k] = _lru_conv(x_ref[lo:hi, :], prev_row, next_rows, pm_ref, cw_ref, cb_ref)

    @pl.when(s > 0)
    def _():
        for d, out_ref in ((0, yf_ref), (1, yb_ref)):
            j = (s - 1) if d == 0 else (n_lat - s)
            for k in (range(LRU_PER_STEP) if d == 0 else reversed(range(LRU_PER_STEP))):
                _lru_gates(d, xc_ref[j * LRU_PER_STEP + k], wd_ref, br_ref, sp_ref, a_ref, u_ref)
                _lru_scan(d, a_ref, u_ref, h_ref, pm_ref, out_ref.at[pl.ds(k * LRU_BLK, LRU_BLK), :], LRU_BLK)


def _rglru(pc, px, conv_w, conv_b, w_rg, b_rg, lam, layer):
    _, bsz, t, _ = px.shape
    rows_step = LRU_BLK * LRU_PER_STEP
    n_lat = t // rows_step
    per_blk = rows_step // BF16_ROWS
    n_halo = t // BF16_ROWS

    def jf(s):
        return jnp.maximum(s - 1, 0)

    def jb(s):
        return jnp.minimum(n_lat - s, n_lat - 1)

    def jfx(s):
        return jnp.clip(s - 1, 0, n_lat // 2 - 1)

    def jbx(s):
        return jnp.clip(n_lat - s, n_lat // 2, n_lat - 1)

    def lat_specs(jfun):
        return [
            pl.BlockSpec((None, None, rows_step, SEC), lambda b, s: (S16_XL, b, jfun(s), 0)),
            pl.BlockSpec((None, None, BF16_ROWS, SEC),
                         lambda b, s: (S16_XL, b, jnp.maximum(jfun(s) * per_blk - 1, 0), 0)),
            pl.BlockSpec((None, None, BF16_ROWS, SEC),
                         lambda b, s: (S16_XL, b, jnp.minimum((jfun(s) + 1) * per_blk, n_halo - 1), 0)),
        ]

    in_specs = [pl.BlockSpec((None, None, CTX_LEN, SEC), lambda b, s: (S16_XL, b, 0, 0))]
    in_specs += lat_specs(jfx) + lat_specs(jbx) + [
        pl.BlockSpec((None, CONV_W, SEC), lambda b, s: (layer, 0, 0)),
        pl.BlockSpec((None, 1, SEC), lambda b, s: (layer, 0, 0)),
        pl.BlockSpec((None,) + w_rg.shape[1:], lambda b, s: (layer, 0, 0, 0, 0, 0)),
        pl.BlockSpec((None,) + b_rg.shape[1:], lambda b, s: (layer, 0, 0, 0)),
        pl.BlockSpec((None,) + lam.shape[1:], lambda b, s: (layer, 0, 0)),
    ]
    out_specs = [
        pl.BlockSpec((None, CTX_LEN, R_WIDTH), lambda b, s: (b, 0, 0)),
        pl.BlockSpec((None, CTX_LEN, R_WIDTH), lambda b, s: (b, 0, 0)),
        pl.BlockSpec((None, rows_step, R_WIDTH), lambda b, s: (b, jf(s), 0)),
        pl.BlockSpec((None, rows_step, R_WIDTH), lambda b, s: (b, jb(s), 0)),
    ]
    out_shape = [
        jax.ShapeDtypeStruct((bsz, CTX_LEN, R_WIDTH), BF16),
        jax.ShapeDtypeStruct((bsz, CTX_LEN, R_WIDTH), BF16),
        jax.ShapeDtypeStruct((bsz, t, R_WIDTH), BF16),
        jax.ShapeDtypeStruct((bsz, t, R_WIDTH), BF16),
    ]
    return pl.pallas_call(
        functools.partial(_rglru_kernel, n_lat=n_lat),
        grid=(bsz, n_lat + 1),
        in_specs=in_specs,
        out_specs=out_specs,
        out_shape=out_shape,
        scratch_shapes=[
            pltpu.VMEM((2, LRU_BLK, LRU_BLK), BF16),
            pltpu.VMEM((LRU_BLK, R_WIDTH), F32),
            pltpu.VMEM((LRU_BLK, R_WIDTH), F32),
            pltpu.VMEM((2, SUBLANES, R_WIDTH), F32),
            pltpu.VMEM((2, R_WIDTH), F32),
            pltpu.VMEM((2, 2, R_WIDTH // LRU_TILE, LRU_TILE, LRU_TILE), BF16),
            pltpu.VMEM((t // LRU_BLK, LRU_BLK, R_WIDTH), F32),
        ],
        compiler_params=_cparams(("arbitrary", "arbitrary")),
        name="rglru",
    )(pc, px, px, px, px, px, px, conv_w, conv_b, w_rg, b_rg, lam)


def _out_proj_kernel(*refs, final, split_cols):
    if split_cols:
        hlo_ref, hhi_ref = refs[:2]
        hm = jnp.concatenate([hlo_ref[...], hhi_ref[...]], axis=1)
        hm = hm.reshape(hm.shape[0] * hm.shape[1], M_WIDTH)
        refs = refs[2:]
    else:
        hm = refs[0][...]
        refs = refs[1:]
    yf_ref, yb_ref, o_ref, zm_ref, zl_ref, x_ref, gt_ref, mg_ref, w_ref, fg_ref, out_ref = refs
    parts = []
    for h in range(M_HEADS):
        hh = hm[:, h * M_DV:(h + 1) * M_DV]
        parts.append(hh * lax.rsqrt(jnp.mean(hh * hh, axis=-1, keepdims=True) + EPS))
    hn = jnp.concatenate(parts, axis=1) * mg_ref[...]
    ym = hn * _sigmoid_t(o_ref[...].astype(F32)) * _silu_t(zm_ref[...].astype(F32))
    yr = (yf_ref[...].astype(F32) + yb_ref[...].astype(F32)) * _silu_t(zl_ref[...].astype(F32))
    y = jnp.concatenate([ym, yr], axis=1).astype(BF16)
    xn = x_ref[...] + gt_ref[...] * jnp.dot(y, w_ref[...], preferred_element_type=F32)
    if final:
        xn = xn * lax.rsqrt(jnp.mean(xn * xn, axis=-1, keepdims=True) + EPS) * fg_ref[...]
    out_ref[...] = xn


def _out_proj(h_parts, yf, yb, p16, x2d, mod_l, row_of_tile, m_norm_g, w_out, final_g, layer, tm, final):
    m, d = x2d.shape

    def tok(width):
        return pl.BlockSpec((tm, width), lambda i: (i, 0))

    def sec(k):
        return pl.BlockSpec((None, tm, SEC), lambda i: (k, i, 0))

    split_cols = len(h_parts) == 2
    if split_cols:
        h_specs = [pl.BlockSpec((tm // GRID_W, GRID_W // 2, M_WIDTH), lambda i: (i, 0, 0))] * 2
    else:
        h_specs = [tok(M_WIDTH)]
    return pl.pallas_call(
        functools.partial(_out_proj_kernel, final=final, split_cols=split_cols),
        grid=(m // tm,),
        in_specs=h_specs + [
            tok(R_WIDTH), tok(R_WIDTH),
            sec(S16_O), sec(S16_ZM), sec(S16_ZL),
            tok(d),
            pl.BlockSpec((None, None, 1, d), lambda i: (row_of_tile(i), 2, 0, 0)),
            pl.BlockSpec((None, 1, M_WIDTH), lambda i: (layer, 0, 0)),
            pl.BlockSpec((None, d, d), lambda i: (layer, 0, 0), pipeline_mode=pl.Buffered(1)),
            pl.BlockSpec((1, d), lambda i: (0, 0)),
        ],
        out_specs=tok(d),
        out_shape=jax.ShapeDtypeStruct((m, d), F32),
        compiler_params=_cparams(("arbitrary",)),
        name="out_proj",
    )(*h_parts, yf, yb, p16, p16, p16, x2d, mod_l, m_norm_g, w_out, final_g)


def kernel(x, c, ctx, c_ctx, w_mod, b_mod, norm_g, w_in, b_gate, conv_qk_w, conv_qk_b, m_norm_g, conv_r_w, conv_r_b,
           w_rg, b_rg, lru_lambda, w_out, final_g):
    bsz, t, d = x.shape
    depth = w_mod.shape[0]
    nh = M_HEADS

    w_t = jnp.swapaxes(w_in, 1, 2)
    w_main, w_gate = _win_cast(w_t, 0, WA_SECS * SEC, 4 * nh)
    b_mod3 = b_mod[:, None, :]
    bg2 = jnp.zeros((depth, SUBLANES, GATE_W), F32).at[:, 0, :4 * nh].set(b_gate)
    w_out_b = _cast_bf16(w_out)
    norm_g3 = norm_g[:, None, :]
    m_norm_g3 = m_norm_g[:, None, :]
    conv_qk_b3 = conv_qk_b[:, None, :]
    conv_r_b3 = conv_r_b[:, None, :]
    b_rg4 = 0.5 * b_rg.reshape(depth, 2, 2, R_WIDTH)
    fg = final_g[None, :]

    cvec = jnp.concatenate([c, c_ctx[None, :], jnp.zeros((SUBLANES - bsz - 1, d), F32)], axis=0)
    mod = _modulation(cvec, w_mod, b_mod3, 0)

    tm_in = 1024
    tm_ctx = bsz * CTX_LEN
    tm_out = 512
    x2d = x.reshape(bsz * t, d)
    c2d = ctx.reshape(bsz * CTX_LEN, d)
    lat_row_in = lambda i: i // (t // tm_in)
    lat_row_out = lambda i: i // (t // tm_out)
    ctx_row = lambda i: bsz

    for l in range(depth):
        last = l == depth - 1
        mod_l = mod.reshape(SUBLANES, 3, 1, d)
        nxt = None if last else (cvec, w_mod, b_mod3, w_t, 4 * nh)
        lat = _in_proj(x2d, mod_l, lat_row_in, norm_g3, w_main, w_gate, l, tm_in, nxt)
        px32, px16, gx = lat[:3]
        pc32, pc16, gc = _in_proj(c2d, mod_l, ctx_row, norm_g3, w_main, w_gate, l, tm_ctx)
        if not last:
            mod, w_main, w_gate = lat[3:]
        hc, hhi, hlo = _mlstm(pc32.reshape(N_SEC32, bsz, CTX_LEN, SEC), gc.reshape(bsz, CTX_LEN, GATE_W),
                              px32.reshape(N_SEC32, bsz, t, SEC), gx.reshape(bsz, t, GATE_W),
                              conv_qk_w, conv_qk_b3, bg2, l)
        grid_rows = bsz * t // GRID_W
        ycf, ycb, yf, yb = _rglru(pc16.reshape(N_SEC - N_SEC32, bsz, CTX_LEN, SEC),
                                  px16.reshape(N_SEC - N_SEC32, bsz, t, SEC),
                                  conv_r_w, conv_r_b3, w_rg, b_rg4, lru_lambda, l)
        x2d = _out_proj((hlo.reshape(grid_rows, GRID_W // 2, M_WIDTH), hhi.reshape(grid_rows, GRID_W // 2, M_WIDTH)),
                        yf.reshape(bsz * t, R_WIDTH), yb.reshape(bsz * t, R_WIDTH),
                        px16, x2d, mod_l, lat_row_out, m_norm_g3, w_out_b, fg, l, tm_out, last)
        if not last:
            c2d = _out_proj((hc.reshape(bsz * CTX_LEN, M_WIDTH),),
                            ycf.reshape(bsz * CTX_LEN, R_WIDTH), ycb.reshape(bsz * CTX_LEN, R_WIDTH),
                            pc16, c2d, mod_l, ctx_row, m_norm_g3, w_out_b, fg, l, tm_out, False)
    return x2d.reshape(bsz, t, d)
```

```python
import functools

import jax
import jax.numpy as jnp
from jax import lax
from jax.experimental import pallas as pl
from jax.experimental.pallas import tpu as pltpu

D_MODEL = 2048
DEPTH = 4
CTX_LEN = 256
GRID_W = 64
M_WIDTH = 1024
R_WIDTH = 1024
M_HEADS = 4
M_DV = 256
M_DQK = 128
M_QK = 512
R_BLOCKS = 16
R_BLOCK = 64
CONV_W = 4
LRU_C = 8.0
EPS = 1e-6

LANES = 128
SUBLANES = 8
BF16_ROWS = 16
SEC = 1024
N_SEC = 6
N_SEC32 = 2
S32_QK, S32_V = 0, 1
S16_O, S16_ZM, S16_XL, S16_ZL = 0, 1, 2, 3
WA_SECS = 4
GATE_W = LANES
CTX_CHUNK = CTX_LEN
COL_GROUP = SUBLANES
LAT_CHUNK = COL_GROUP * GRID_W
LRU_BLK = 256
LRU_PER_STEP = 4
LRU_TILE = 256
VMEM_LIMIT = 56 * 1024 * 1024
LOG2E = 1.4426950408889634

F32 = jnp.float32
BF16 = jnp.bfloat16


def _cparams(sem):
    return pltpu.CompilerParams(dimension_semantics=sem, vmem_limit_bytes=VMEM_LIMIT)


def _silu_half(h):
    return h + h * jnp.tanh(h)


def _silu_t(x):
    return _silu_half(0.5 * x)


def _dot_nt(a, b):
    return lax.dot_general(a, b, (((1,), (1,)), ((), ())), preferred_element_type=F32)


def _sigmoid_t(x):
    return 0.5 * jnp.tanh(0.5 * x) + 0.5


def _mod_kernel(c_ref, w_ref, b_ref, o_ref):
    c = c_ref[...]
    s = (c * jax.nn.sigmoid(c)).astype(BF16)
    o_ref[...] = jnp.dot(s, w_ref[...].astype(BF16), preferred_element_type=F32) + b_ref[...]


def _modulation(cvec, w_mod, b_mod3, layer):
    _, d, n = w_mod.shape
    tn = 1024
    return pl.pallas_call(
        _mod_kernel,
        grid=(n // tn,),
        in_specs=[
            pl.BlockSpec((SUBLANES, d), lambda j: (0, 0)),
            pl.BlockSpec((None, d, tn), lambda j: (layer, 0, j)),
            pl.BlockSpec((None, 1, tn), lambda j: (layer, 0, j)),
        ],
        out_specs=pl.BlockSpec((SUBLANES, tn), lambda j: (0, j)),
        out_shape=jax.ShapeDtypeStruct((SUBLANES, n), F32),
        compiler_params=_cparams(("arbitrary",)),
        name="mod",
    )(cvec, w_mod, b_mod3)


def _cast_rows(r, a_ref, b_ref, o_ref, g_ref, n_direct, skip):
    tr = o_ref.shape[0]

    @pl.when(r < n_direct)
    def _():
        o_ref[...] = a_ref[...].astype(BF16)

    @pl.when(r >= n_direct)
    def _():
        o_ref[:tr - skip, :] = a_ref[skip:, :].astype(BF16)
        o_ref[tr - skip:, :] = b_ref[...].astype(BF16)

    @pl.when(r == n_direct)
    def _():
        g_ref[0:skip, :] = a_ref[0:skip, :].astype(BF16)
        g_ref[skip:, :] = jnp.zeros((g_ref.shape[0] - skip, g_ref.shape[1]), BF16)


def _in_proj_kernel(x_ref, sh_ref, sc_ref, g_ref, w_ref, wg_ref, *refs, prep):
    n = pl.program_id(1)
    if prep is None:
        p32_ref, p16_ref, gate_ref, h_scr = refs

        def prep_step():
            pass
    else:
        c_ref, wm_ref, bm_ref, wa_ref, wb_ref, p32_ref, p16_ref, gate_ref, modn_ref, wn_ref, wgn_ref, h_scr = refs
        n_direct, skip = prep
        k = pl.program_id(0) * N_SEC + n

        @pl.when(k == n_direct)
        def _():
            wgn_ref[0:skip, :] = wa_ref[0:skip, :].astype(BF16)
            wgn_ref[skip:, :] = jnp.zeros((wgn_ref.shape[0] - skip, wgn_ref.shape[1]), BF16)

        def prep_step():
            c = c_ref[...]
            modn_ref[...] = jnp.dot((c * jax.nn.sigmoid(c)).astype(BF16), wm_ref[...].astype(BF16),
                                    preferred_element_type=F32) + bm_ref[...]
            a = wa_ref[...]
            below = jnp.concatenate([a[skip:, :], wb_ref[...]], axis=0)
            wn_ref[...] = jnp.where(k < n_direct, a, below).astype(BF16)

    @pl.when(n == 0)
    def _():
        x = x_ref[...]
        ms = jnp.mean(x * x, axis=-1, keepdims=True)
        y = x * lax.rsqrt(ms + EPS) * g_ref[...]
        h = (y * (1.0 + sc_ref[...]) + sh_ref[...]).astype(BF16)
        h_scr[...] = h
        gate_ref[...] = _dot_nt(h, wg_ref[...])

    @pl.when(n < N_SEC32)
    def _():
        prep_step()
        p32_ref[...] = _dot_nt(h_scr[...], w_ref[...])

    @pl.when(n >= N_SEC32)
    def _():
        prep_step()
        p16_ref[...] = _dot_nt(h_scr[...], w_ref[...]).astype(BF16)


def _in_proj(x2d, mod_l, row_of_tile, norm_g, w_main, w_gate, layer, tm, nxt=None):
    m, d = x2d.shape
    n_tiles = m // tm
    in_specs = [
        pl.BlockSpec((tm, d), lambda i, n: (i, 0)),
        pl.BlockSpec((None, None, 1, d), lambda i, n: (row_of_tile(i), 0, 0, 0)),
        pl.BlockSpec((None, None, 1, d), lambda i, n: (row_of_tile(i), 1, 0, 0)),
        pl.BlockSpec((None, 1, d), lambda i, n: (layer, 0, 0)),
        pl.BlockSpec((SEC, d), lambda i, n: (n, 0)),
        pl.BlockSpec((GATE_W, d), lambda i, n: (0, 0)),
    ]
    out_specs = [
        pl.BlockSpec((None, tm, SEC), lambda i, n: (jnp.minimum(n, N_SEC32 - 1), i, 0)),
        pl.BlockSpec((None, tm, SEC), lambda i, n: (jnp.maximum(n - N_SEC32, 0), i, 0)),
        pl.BlockSpec((tm, GATE_W), lambda i, n: (i, 0)),
    ]
    out_shape = [
        jax.ShapeDtypeStruct((N_SEC32, m, SEC), F32),
        jax.ShapeDtypeStruct((N_SEC - N_SEC32, m, SEC), BF16),
        jax.ShapeDtypeStruct((m, GATE_W), F32),
    ]
    operands = [x2d, mod_l, mod_l, norm_g, w_main, w_gate]
    prep = None
    if nxt is not None:
        cvec, w_mod, b_mod3, w_t, skip = nxt
        steps = n_tiles * N_SEC
        n_mod = w_mod.shape[2]
        tr, tn = (N_SEC * SEC) // steps, n_mod // steps
        assert tr * steps == N_SEC * SEC and tr % skip == 0 and tn * steps == n_mod and tn % LANES == 0
        prep = (WA_SECS * SEC // tr, skip)

        def step(i, n):
            return i * N_SEC + n

        in_specs += [
            pl.BlockSpec((SUBLANES, d), lambda i, n: (0, 0)),
            pl.BlockSpec((None, d, tn), lambda i, n: (layer + 1, 0, step(i, n))),
            pl.BlockSpec((None, 1, tn), lambda i, n: (layer + 1, 0, step(i, n))),
            pl.BlockSpec((None, tr, d), lambda i, n: (layer + 1, step(i, n), 0)),
            pl.BlockSpec((None, skip, d), lambda i, n: (layer + 1, (step(i, n) + 1) * (tr // skip), 0)),
        ]
        out_specs += [
            pl.BlockSpec((SUBLANES, tn), lambda i, n: (0, step(i, n))),
            pl.BlockSpec((tr, d), lambda i, n: (step(i, n), 0)),
            pl.BlockSpec((GATE_W, d), lambda i, n: (0, 0)),
        ]
        out_shape += [
            jax.ShapeDtypeStruct((SUBLANES, n_mod), F32),
            jax.ShapeDtypeStruct((N_SEC * SEC, d), BF16),
            jax.ShapeDtypeStruct((GATE_W, d), BF16),
        ]
        operands += [cvec, w_mod, b_mod3, w_t, w_t]
    return pl.pallas_call(
        functools.partial(_in_proj_kernel, prep=prep),
        grid=(n_tiles, N_SEC),
        in_specs=in_specs,
        out_specs=out_specs,
        out_shape=out_shape,
        scratch_shapes=[pltpu.VMEM((tm, d), BF16)],
        compiler_params=_cparams(("arbitrary", "arbitrary")),
        name="in_proj",
    )(*operands)


def _win_cast_kernel(a_ref, b_ref, o_ref, g_ref, *, n_direct, skip):
    _cast_rows(pl.program_id(0), a_ref, b_ref, o_ref, g_ref, n_direct, skip)


def _win_cast(w_t, layer, split, skip):
    _, n_in, d = w_t.shape
    tr = 256
    return pl.pallas_call(
        functools.partial(_win_cast_kernel, n_direct=split // tr, skip=skip),
        grid=((n_in - skip) // tr,),
        in_specs=[pl.BlockSpec((None, tr, d), lambda r: (layer, r, 0)),
                  pl.BlockSpec((None, skip, d), lambda r: (layer, (r + 1) * (tr // skip), 0))],
        out_specs=[pl.BlockSpec((tr, d), lambda r: (r, 0)),
                   pl.BlockSpec((GATE_W, d), lambda r: (0, 0))],
        out_shape=[jax.ShapeDtypeStruct((n_in - skip, d), BF16),
                   jax.ShapeDtypeStruct((GATE_W, d), BF16)],
        compiler_params=_cparams(("arbitrary",)),
        name="win_cast",
    )(w_t, w_t)


def _cast_kernel(w_ref, o_ref):
    o_ref[...] = w_ref[...].astype(BF16)


def _cast_bf16(w):
    depth, r, c = w.shape
    tr = 512
    return pl.pallas_call(
        _cast_kernel,
        grid=(depth, r // tr),
        in_specs=[pl.BlockSpec((None, tr, c), lambda l, j: (l, j, 0))],
        out_specs=pl.BlockSpec((None, tr, c), lambda l, j: (l, j, 0)),
        out_shape=jax.ShapeDtypeStruct((depth, r, c), BF16),
        compiler_params=_cparams(("arbitrary", "arbitrary")),
        name="cast_bf16",
    )(w)


def _shifted(x, k, reverse, ident):
    n = x.shape[0]
    row = lax.broadcasted_iota(jnp.int32, x.shape, 0)
    if reverse:
        return jnp.where(row < n - k, pltpu.roll(x, n - k, axis=0), ident)
    return jnp.where(row >= k, pltpu.roll(x, k, axis=0), ident)


def _scan_rows(x, op, ident, reverse):
    n = x.shape[0]
    k = 1
    while k < n:
        x = op(x, _shifted(x, k, reverse, ident))
        k *= 2
    return x


def _scan_colmajor(x, op, ident, reverse):
    n = x.shape[0]
    k = SUBLANES
    while k < n:
        x = op(x, _shifted(x, k, reverse, ident))
        k *= 2
    tot = x[0:SUBLANES, :] if reverse else x[n - SUBLANES:n, :]
    tot = _scan_rows(tot, op, ident, reverse)
    tot = _shifted(tot, 1, reverse, ident)
    return op(x, jnp.concatenate([tot] * (n // SUBLANES), axis=0))


def _conv_rows(xs_ref, n, w_ref, b_ref, step):
    base = SUBLANES - step
    acc = b_ref[...] + w_ref[0:1, :] * xs_ref[base:base + n, :]
    for j in range(1, CONV_W):
        acc = acc + w_ref[j:j + 1, :] * xs_ref[base + j * step:base + j * step + n, :]
    return acc


def _mlstm_dir(d, qk, v, gates, bg_ref, ct_ref, m_ref, scan, posdiff, write_h):
    n = qk.shape[0]
    rev = d == 1
    gi = gates + bg_ref[0:1, :]
    lf = jax.nn.log_sigmoid(pltpu.roll(gi, LANES - M_HEADS, axis=1))
    bc = scan(lf, jnp.add, 0.0, rev)
    a = gi - bc
    m_prev = m_ref[d, 0:1, :]
    mm = jnp.maximum(scan(a, jnp.maximum, -jnp.inf, rev), m_prev)
    inter = jnp.exp(m_prev - mm)
    em = jnp.exp(-(bc + mm))
    last = 0 if rev else n - 1
    mm_last = mm[last:last + 1, :]
    m_new = bc[last:last + 1, :] + mm_last
    decay = jnp.exp(m_prev - mm_last)
    wcol = jnp.exp(a - mm_last)
    a_t = (a * LOG2E).T
    mm2 = mm * LOG2E
    mask = (posdiff <= 0) if rev else (posdiff >= 0)
    ones = jnp.ones((n, LANES), BF16)
    for h in range(M_HEADS):
        e = 2 * M_HEADS * d + h
        st = d * M_HEADS + h
        qf = qk[:, h * M_DQK:(h + 1) * M_DQK]
        q = qf.astype(BF16)
        kf = qk[:, M_QK + h * M_DQK:M_QK + (h + 1) * M_DQK] * (M_DQK ** -0.5)
        vaug = jnp.concatenate([v[:, h * M_DV:(h + 1) * M_DV].astype(BF16), ones], axis=1)
        dmat = jnp.where(mask, jnp.exp2(a_t[e:e + 1, :] - mm2[:, e:e + 1]), 0.0)
        s = lax.dot_general(q, kf.astype(BF16), (((1,), (1,)), ((), ())), preferred_element_type=F32)
        sw = (s * dmat).astype(BF16)
        ct = ct_ref[st]
        qi = (qf * inter[:, e:e + 1]).astype(BF16)
        num = jnp.dot(jnp.concatenate([sw, qi], axis=1), jnp.concatenate([vaug, ct.astype(BF16)], axis=0),
                      preferred_element_type=F32)
        den = jnp.maximum(jnp.abs(num[:, M_DV:]), em[:, e:e + 1])
        write_h(h, num[:, :M_DV] / jnp.concatenate([den, den], axis=1))
        kw = (kf * wcol[:, e:e + 1]).astype(BF16)
        upd = lax.dot_general(kw, vaug, (((0,), (0,)), ((), ())), preferred_element_type=F32)
        ct_ref[st] = decay[:, e:e + 1] * ct + upd
    m_ref[d, 0:1, :] = m_new


def _mlstm_kernel(cqk_ref, cv_ref, cg_ref,
                  qf_ref, pf_ref, nf_ref, vf_ref, gf_ref,
                  qb_ref, pb_ref, nb_ref, vb_ref, gb_ref,
                  cw_ref, cb_ref, bg_ref,
                  hc_ref, hhi_ref, hlo_ref,
                  xs_ref, ct_ref, m_ref, pd_ref, park_ref, *, n_lat):
    s = pl.program_id(1)
    lc = LAT_CHUNK
    zero_rows = jnp.zeros((SUBLANES, 2 * M_QK), F32)

    @pl.when(s == 0)
    def _():
        ct_ref[...] = jnp.zeros_like(ct_ref)
        m_ref[...] = jnp.zeros_like(m_ref)
        park_ref[...] = jnp.zeros_like(park_ref)
        row = lax.broadcasted_iota(jnp.int32, (lc, lc), 0)
        col = lax.broadcasted_iota(jnp.int32, (lc, lc), 1)
        pos_r = (row % COL_GROUP) * GRID_W + row // COL_GROUP
        pos_c = (col % COL_GROUP) * GRID_W + col // COL_GROUP
        pd_ref[...] = pos_r - pos_c
        xs_ref[0:SUBLANES, :] = zero_rows
        xs_ref[SUBLANES + CTX_CHUNK:2 * SUBLANES + CTX_CHUNK, :] = zero_rows
        xs_ref[SUBLANES:SUBLANES + CTX_CHUNK, :] = cqk_ref[...]
        qk = _silu_half(_conv_rows(xs_ref, CTX_CHUNK, cw_ref, cb_ref, 1))
        v = cv_ref[...]
        g = cg_ref[...]
        crow = lax.broadcasted_iota(jnp.int32, (CTX_CHUNK, CTX_CHUNK), 0)
        ccol = lax.broadcasted_iota(jnp.int32, (CTX_CHUNK, CTX_CHUNK), 1)
        for d in (0, 1):
            def write_h(h, val, d=d):
                lanes = slice(h * M_DV, (h + 1) * M_DV)
                hc_ref[:, lanes] = val if d == 0 else hc_ref[:, lanes] + val
            _mlstm_dir(d, qk, v, g, bg_ref, ct_ref, m_ref, _scan_rows, crow - ccol, write_h)

    @pl.when(s > 0)
    def _():
        sub = lax.broadcasted_iota(jnp.int32, (SUBLANES, 2 * M_QK), 0)
        for d, q_ref, p_ref, n_ref, v_ref, g_ref, out_ref in (
                (0, qf_ref, pf_ref, nf_ref, vf_ref, gf_ref, hhi_ref),
                (1, qb_ref, pb_ref, nb_ref, vb_ref, gb_ref, hlo_ref)):
            j = (s - 1) if d == 0 else (n_lat - s)
            has_prev = (j > 0).astype(F32)
            has_next = (j < n_lat - 1).astype(F32)
            x = q_ref[...].reshape(lc, 2 * M_QK)
            xs_ref[SUBLANES:SUBLANES + lc, :] = x
            x_last = x[lc - SUBLANES:lc, :]
            xs_ref[0:SUBLANES, :] = jnp.where(sub == 0, pltpu.roll(p_ref[...], 1, axis=0) * has_prev,
                                              pltpu.roll(x_last, 1, axis=0))
            for k in range(2):
                xs_ref[SUBLANES + lc + k * SUBLANES:2 * SUBLANES + lc + k * SUBLANES, :] = jnp.where(
                    sub == SUBLANES - 1, pltpu.roll(n_ref[k], SUBLANES - 1, axis=0) * has_next,
                    pltpu.roll(x[k * SUBLANES:(k + 1) * SUBLANES, :], SUBLANES - 1, axis=0))
            qk = _silu_half(_conv_rows(xs_ref, lc, cw_ref, cb_ref, SUBLANES))
            v = v_ref[...].reshape(lc, M_WIDTH)
            g = g_ref[...].reshape(lc, GATE_W)

            def write_h(h, val, out_ref=out_ref, j=j):
                lanes = slice(h * M_DV, (h + 1) * M_DV)
                other = park_ref[j, :, lanes].astype(F32)
                park_ref[j, :, lanes] = val.astype(BF16)
                out_ref[:, :, lanes] = (val + other).reshape(GRID_W, COL_GROUP, M_DV)
            _mlstm_dir(d, qk, v, g, bg_ref, ct_ref, m_ref, _scan_colmajor, pd_ref[...], write_h)


def _mlstm(pc, gc, px, gx, conv_w, conv_b, bg2, layer):
    _, bsz, t, _ = px.shape
    rows = t // GRID_W
    n_lat = GRID_W // COL_GROUP
    pxv = px.reshape(N_SEC32, bsz, rows, GRID_W, SEC)
    gxv = gx.reshape(bsz, rows, GRID_W, GATE_W)

    def jf(s):
        return jnp.maximum(s - 1, 0)

    def jb(s):
        return jnp.minimum(n_lat - s, n_lat - 1)

    def lat_specs(jfun):
        return [
            pl.BlockSpec((None, None, rows, COL_GROUP, SEC), lambda b, s: (S32_QK, b, 0, jfun(s), 0)),
            pl.BlockSpec((None, None, None, COL_GROUP, SEC),
                         lambda b, s: (S32_QK, b, rows - 1, jnp.maximum(jfun(s) - 1, 0), 0)),
            pl.BlockSpec((None, None, 2, COL_GROUP, SEC),
                         lambda b, s: (S32_QK, b, 0, jnp.minimum(jfun(s) + 1, n_lat - 1), 0)),
            pl.BlockSpec((None, None, rows, COL_GROUP, SEC), lambda b, s: (S32_V, b, 0, jfun(s), 0)),
            pl.BlockSpec((None, rows, COL_GROUP, GATE_W), lambda b, s: (b, 0, jfun(s), 0)),
        ]

    in_specs = [
        pl.BlockSpec((None, None, CTX_LEN, SEC), lambda b, s: (S32_QK, b, 0, 0)),
        pl.BlockSpec((None, None, CTX_LEN, SEC), lambda b, s: (S32_V, b, 0, 0)),
        pl.BlockSpec((None, CTX_LEN, GATE_W), lambda b, s: (b, 0, 0)),
    ] + lat_specs(jf) + lat_specs(jb) + [
        pl.BlockSpec((None, CONV_W, SEC), lambda b, s: (layer, 0, 0)),
        pl.BlockSpec((None, 1, SEC), lambda b, s: (layer, 0, 0)),
        pl.BlockSpec((None, SUBLANES, LANES), lambda b, s: (layer, 0, 0)),
    ]
    half = n_lat // 2
    out_specs = [
        pl.BlockSpec((None, CTX_LEN, M_WIDTH), lambda b, s: (b, 0, 0)),
        pl.BlockSpec((None, rows, COL_GROUP, M_WIDTH), lambda b, s: (b, 0, jnp.clip(s - 1 - half, 0, half - 1), 0)),
        pl.BlockSpec((None, rows, COL_GROUP, M_WIDTH), lambda b, s: (b, 0, jnp.clip(n_lat - s, 0, half - 1), 0)),
    ]
    out_shape = [
        jax.ShapeDtypeStruct((bsz, CTX_LEN, M_WIDTH), F32),
        jax.ShapeDtypeStruct((bsz, rows, GRID_W // 2, M_WIDTH), F32),
        jax.ShapeDtypeStruct((bsz, rows, GRID_W // 2, M_WIDTH), F32),
    ]
    return pl.pallas_call(
        functools.partial(_mlstm_kernel, n_lat=n_lat),
        grid=(bsz, n_lat + 1),
        in_specs=in_specs,
        out_specs=out_specs,
        out_shape=out_shape,
        scratch_shapes=[
            pltpu.VMEM((LAT_CHUNK + 3 * SUBLANES, 2 * M_QK), F32),
            pltpu.VMEM((2 * M_HEADS, M_DQK, M_DV + LANES), F32),
            pltpu.VMEM((2, SUBLANES, LANES), F32),
            pltpu.VMEM((LAT_CHUNK, LAT_CHUNK), jnp.int32),
            pltpu.VMEM((n_lat, LAT_CHUNK, M_WIDTH), BF16),
        ],
        compiler_params=_cparams(("arbitrary", "arbitrary")),
        name="mlstm",
    )(pc, pc, gc, pxv, pxv, pxv, pxv, gxv, pxv, pxv, pxv, pxv, gxv, conv_w, conv_b, bg2)


def _lru_conv(x_bf, prev_row, next_rows, pm_ref, cw_ref, cb_ref):
    n = x_bf.shape[0]
    s8 = SUBLANES
    xp = jnp.dot(pm_ref[0], x_bf, preferred_element_type=F32)
    sub = lax.broadcasted_iota(jnp.int32, (s8, x_bf.shape[1]), 0)

    def from_next_segment(v, fill):
        return jnp.where(sub == s8 - 1, fill, pltpu.roll(v, s8 - 1, axis=0))

    def from_prev_segment(v, fill):
        return jnp.where(sub == 0, fill, pltpu.roll(v, 1, axis=0))

    first, second, last = xp[0:s8, :], xp[s8:2 * s8, :], xp[n - s8:, :]
    xm1 = jnp.concatenate([from_prev_segment(last, prev_row), xp[:n - s8, :]], axis=0)
    xp1 = jnp.concatenate([xp[s8:, :], from_next_segment(first, next_rows[0:1, :])], axis=0)
    xp2 = jnp.concatenate([xp[2 * s8:, :], from_next_segment(first, next_rows[0:1, :]),
                           from_next_segment(second, next_rows[1:2, :])], axis=0)
    return (cb_ref[...] + cw_ref[0:1, :] * xm1 + cw_ref[1:2, :] * xp
            + cw_ref[2:3, :] * xp1 + cw_ref[3:4, :] * xp2)


def _lru_gates(d, xc, wd_ref, br_ref, sp_ref, a_ref, u_ref):
    xb = xc.astype(BF16)
    for j in range(R_WIDTH // LRU_TILE):
        sl = slice(j * LRU_TILE, (j + 1) * LRU_TILE)
        xj = xb[:, sl]
        tr = jnp.tanh(jnp.dot(xj, wd_ref[d, 0, j], preferred_element_type=F32) + br_ref[d, 0:1, sl])
        ti = jnp.tanh(jnp.dot(xj, wd_ref[d, 1, j], preferred_element_type=F32) + br_ref[d, 1:2, sl])
        sp = sp_ref[d:d + 1, sl]
        nla = tr * sp + sp
        a = jnp.exp2(nla * (-LOG2E))
        a_ref[:, sl] = a
        xh = xc[:, sl]
        z = jnp.tanh(nla) * (a * a + 1.0)
        root = jnp.where(z > 0.0, z * lax.rsqrt(z), 0.0)
        u_ref[:, sl] = root * (ti * xh + xh)


def _lru_scan(d, a_ref, u_ref, h_ref, pm_ref, out_ref, n):
    rev = d == 1
    s8 = SUBLANES
    groups = n // s8
    sub = lax.broadcasted_iota(jnp.int32, (s8, R_WIDTH), 0)

    def body(g, carry):
        h, acc = carry
        gg = (groups - 1 - g) if rev else g
        r0 = pl.multiple_of(gg * s8, s8)
        a = a_ref[pl.ds(r0, s8), :]
        h = a * h + u_ref[pl.ds(r0, s8), :]
        acc = a * acc
        u_ref[pl.ds(r0, s8), :] = h
        a_ref[pl.ds(r0, s8), :] = acc
        return h, acc

    u, a = lax.fori_loop(0, groups, body, (jnp.zeros((s8, R_WIDTH), F32), jnp.ones((s8, R_WIDTH), F32)))
    k = 1
    while k < s8:
        if rev:
            ok = sub < s8 - k
            a_s = pltpu.roll(a, s8 - k, axis=0)
            u_s = pltpu.roll(u, s8 - k, axis=0)
        else:
            ok = sub >= k
            a_s = pltpu.roll(a, k, axis=0)
            u_s = pltpu.roll(u, k, axis=0)
        u = u + a * jnp.where(ok, u_s, 0.0)
        a = a * jnp.where(ok, a_s, 1.0)
        k *= 2
    c0 = h_ref[d, 0:1, :]
    after = u + a * c0
    if rev:
        entry = jnp.where(sub == s8 - 1, c0, pltpu.roll(after, s8 - 1, axis=0))
        h_ref[d, 0:1, :] = after[0:1, :]
    else:
        entry = jnp.where(sub == 0, c0, pltpu.roll(after, 1, axis=0))
        h_ref[d, 0:1, :] = after[s8 - 1:s8, :]
    hs = u_ref[...] + a_ref[...] * jnp.concatenate([entry] * groups, axis=0)
    out_ref[...] = jnp.dot(pm_ref[1], hs.astype(BF16), preferred_element_type=F32).astype(out_ref.dtype)


def _rglru_kernel(cx_ref, xf_ref, pf_ref, nf_ref, xb_ref, pb_ref, nb_ref,
                  cw_ref, cb_ref, wr_ref, br_ref, lam_ref,
                  ycf_ref, ycb_ref, yf_ref, yb_ref,
                  pm_ref, a_ref, u_ref, h_ref, sp_ref, wd_ref, xc_ref, *, n_lat):
    s = pl.program_id(1)
    seg = LRU_BLK // SUBLANES

    @pl.when(s == 0)
    def _():
        h_ref[...] = jnp.zeros_like(h_ref)
        sp_ref[...] = (0.5 * LRU_C) * jax.nn.softplus(-lam_ref[...])
        wd_ref[...] = jnp.zeros_like(wd_ref)
        per = LRU_TILE // R_BLOCK
        for dd in range(2):
            for g in range(2):
                for blk in range(R_BLOCKS):
                    j, p = divmod(blk, per)
                    rows = slice(p * R_BLOCK, (p + 1) * R_BLOCK)
                    wd_ref[dd, g, j, rows, rows] = wr_ref[dd, g, blk].astype(BF16)
        row = lax.broadcasted_iota(jnp.int32, (LRU_BLK, LRU_BLK), 0)
        col = lax.broadcasted_iota(jnp.int32, (LRU_BLK, LRU_BLK), 1)
        pm_ref[0] = jnp.where(col == (row % SUBLANES) * seg + row // SUBLANES, 1.0, 0.0).astype(BF16)
        pm_ref[1] = jnp.where(row == (col % SUBLANES) * seg + col // SUBLANES, 1.0, 0.0).astype(BF16)
        zero_rows = jnp.zeros((2, R_WIDTH), F32)
        xc = _lru_conv(cx_ref[...], zero_rows[0:1, :], zero_rows, pm_ref, cw_ref, cb_ref)
        for d, out_ref in ((0, ycf_ref), (1, ycb_ref)):
            _lru_gates(d, xc, wd_ref, br_ref, sp_ref, a_ref, u_ref)
            _lru_scan(d, a_ref, u_ref, h_ref, pm_ref, out_ref, LRU_BLK)

    @pl.when(jnp.logical_and(s > 0, s <= n_lat // 2))
    def _():
        for d, x_ref, p_ref, n_ref in ((0, xf_ref, pf_ref, nf_ref), (1, xb_ref, pb_ref, nb_ref)):
            j = (s - 1) if d == 0 else (n_lat - s)
            has_prev = (j > 0).astype(F32)
            has_next = (j < n_lat - 1).astype(F32)
            for k in range(LRU_PER_STEP):
                lo, hi = k * LRU_BLK, (k + 1) * LRU_BLK
                if k == 0:
                    prev_row = p_ref[...].astype(F32)[BF16_ROWS - 1:BF16_ROWS, :] * has_prev
                else:
                    prev_row = x_ref[lo - BF16_ROWS:lo, :].astype(F32)[BF16_ROWS - 1:BF16_ROWS, :]
                if k == LRU_PER_STEP - 1:
                    next_rows = n_ref[...].astype(F32)[0:2, :] * has_next
                else:
                    next_rows = x_ref[hi:hi + BF16_ROWS, :].astype(F32)[0:2, :]
                xc_ref[j * LRU_PER_STEP + k] = _lru_conv(x_ref[lo:hi, :], prev_row, next_rows, pm_ref, cw_ref, cb_ref)

    @pl.when(s > 0)
    def _():
        for d, out_ref in ((0, yf_ref), (1, yb_ref)):
            j = (s - 1) if d == 0 else (n_lat - s)
            for k in (range(LRU_PER_STEP) if d == 0 else reversed(range(LRU_PER_STEP))):
                _lru_gates(d, xc_ref[j * LRU_PER_STEP + k], wd_ref, br_ref, sp_ref, a_ref, u_ref)
                _lru_scan(d, a_ref, u_ref, h_ref, pm_ref, out_ref.at[pl.ds(k * LRU_BLK, LRU_BLK), :], LRU_BLK)


def _rglru(pc, px, conv_w, conv_b, w_rg, b_rg, lam, layer):
    _, bsz, t, _ = px.shape
    rows_step = LRU_BLK * LRU_PER_STEP
    n_lat = t // rows_step
    per_blk = rows_step // BF16_ROWS
    n_halo = t // BF16_ROWS

    def jf(s):
        return jnp.maximum(s - 1, 0)

    def jb(s):
        return jnp.minimum(n_lat - s, n_lat - 1)

    def jfx(s):
        return jnp.clip(s - 1, 0, n_lat // 2 - 1)

    def jbx(s):
        return jnp.clip(n_lat - s, n_lat // 2, n_lat - 1)

    def lat_specs(jfun):
        return [
            pl.BlockSpec((None, None, rows_step, SEC), lambda b, s: (S16_XL, b, jfun(s), 0)),
            pl.BlockSpec((None, None, BF16_ROWS, SEC),
                         lambda b, s: (S16_XL, b, jnp.maximum(jfun(s) * per_blk - 1, 0), 0)),
            pl.BlockSpec((None, None, BF16_ROWS, SEC),
                         lambda b, s: (S16_XL, b, jnp.minimum((jfun(s) + 1) * per_blk, n_halo - 1), 0)),
        ]

    in_specs = [pl.BlockSpec((None, None, CTX_LEN, SEC), lambda b, s: (S16_XL, b, 0, 0))]
    in_specs += lat_specs(jfx) + lat_specs(jbx) + [
        pl.BlockSpec((None, CONV_W, SEC), lambda b, s: (layer, 0, 0)),
        pl.BlockSpec((None, 1, SEC), lambda b, s: (layer, 0, 0)),
        pl.BlockSpec((None,) + w_rg.shape[1:], lambda b, s: (layer, 0, 0, 0, 0, 0)),
        pl.BlockSpec((None,) + b_rg.shape[1:], lambda b, s: (layer, 0, 0, 0)),
        pl.BlockSpec((None,) + lam.shape[1:], lambda b, s: (layer, 0, 0)),
    ]
    out_specs = [
        pl.BlockSpec((None, CTX_LEN, R_WIDTH), lambda b, s: (b, 0, 0)),
        pl.BlockSpec((None, CTX_LEN, R_WIDTH), lambda b, s: (b, 0, 0)),
        pl.BlockSpec((None, rows_step, R_WIDTH), lambda b, s: (b, jf(s), 0)),
        pl.BlockSpec((None, rows_step, R_WIDTH), lambda b, s: (b, jb(s), 0)),
    ]
    out_shape = [
        jax.ShapeDtypeStruct((bsz, CTX_LEN, R_WIDTH), BF16),
        jax.ShapeDtypeStruct((bsz, CTX_LEN, R_WIDTH), BF16),
        jax.ShapeDtypeStruct((bsz, t, R_WIDTH), BF16),
        jax.ShapeDtypeStruct((bsz, t, R_WIDTH), BF16),
    ]
    return pl.pallas_call(
        functools.partial(_rglru_kernel, n_lat=n_lat),
        grid=(bsz, n_lat + 1),
        in_specs=in_specs,
        out_specs=out_specs,
        out_shape=out_shape,
        scratch_shapes=[
            pltpu.VMEM((2, LRU_BLK, LRU_BLK), BF16),
            pltpu.VMEM((LRU_BLK, R_WIDTH), F32),
            pltpu.VMEM((LRU_BLK, R_WIDTH), F32),
            pltpu.VMEM((2, SUBLANES, R_WIDTH), F32),
            pltpu.VMEM((2, R_WIDTH), F32),
            pltpu.VMEM((2, 2, R_WIDTH // LRU_TILE, LRU_TILE, LRU_TILE), BF16),
            pltpu.VMEM((t // LRU_BLK, LRU_BLK, R_WIDTH), F32),
        ],
        compiler_params=_cparams(("arbitrary", "arbitrary")),
        name="rglru",
    )(pc, px, px, px, px, px, px, conv_w, conv_b, w_rg, b_rg, lam)


def _out_proj_kernel(*refs, final, split_cols):
    if split_cols:
        hlo_ref, hhi_ref = refs[:2]
        hm = jnp.concatenate([hlo_ref[...], hhi_ref[...]], axis=1)
        hm = hm.reshape(hm.shape[0] * hm.shape[1], M_WIDTH)
        refs = refs[2:]
    else:
        hm = refs[0][...]
        refs = refs[1:]
    yf_ref, yb_ref, o_ref, zm_ref, zl_ref, x_ref, gt_ref, mg_ref, w_ref, fg_ref, out_ref = refs
    parts = []
    for h in range(M_HEADS):
        hh = hm[:, h * M_DV:(h + 1) * M_DV]
        parts.append(hh * lax.rsqrt(jnp.mean(hh * hh, axis=-1, keepdims=True) + EPS))
    hn = jnp.concatenate(parts, axis=1) * mg_ref[...]
    ym = hn * _sigmoid_t(o_ref[...].astype(F32)) * _silu_t(zm_ref[...].astype(F32))
    yr = (yf_ref[...].astype(F32) + yb_ref[...].astype(F32)) * _silu_t(zl_ref[...].astype(F32))
    y = jnp.concatenate([ym, yr], axis=1).astype(BF16)
    xn = x_ref[...] + gt_ref[...] * jnp.dot(y, w_ref[...], preferred_element_type=F32)
    if final:
        xn = xn * lax.rsqrt(jnp.mean(xn * xn, axis=-1, keepdims=True) + EPS) * fg_ref[...]
    out_ref[...] = xn


def _out_proj(h_parts, yf, yb, p16, x2d, mod_l, row_of_tile, m_norm_g, w_out, final_g, layer, tm, final):
    m, d = x2d.shape

    def tok(width):
        return pl.BlockSpec((tm, width), lambda i: (i, 0))

    def sec(k):
        return pl.BlockSpec((None, tm, SEC), lambda i: (k, i, 0))

    split_cols = len(h_parts) == 2
    if split_cols:
        h_specs = [pl.BlockSpec((tm // GRID_W, GRID_W // 2, M_WIDTH), lambda i: (i, 0, 0))] * 2
    else:
        h_specs = [tok(M_WIDTH)]
    return pl.pallas_call(
        functools.partial(_out_proj_kernel, final=final, split_cols=split_cols),
        grid=(m // tm,),
        in_specs=h_specs + [
            tok(R_WIDTH), tok(R_WIDTH),
            sec(S16_O), sec(S16_ZM), sec(S16_ZL),
            tok(d),
            pl.BlockSpec((None, None, 1, d), lambda i: (row_of_tile(i), 2, 0, 0)),
            pl.BlockSpec((None, 1, M_WIDTH), lambda i: (layer, 0, 0)),
            pl.BlockSpec((None, d, d), lambda i: (layer, 0, 0), pipeline_mode=pl.Buffered(1)),
            pl.BlockSpec((1, d), lambda i: (0, 0)),
        ],
        out_specs=tok(d),
        out_shape=jax.ShapeDtypeStruct((m, d), F32),
        compiler_params=_cparams(("arbitrary",)),
        name="out_proj",
    )(*h_parts, yf, yb, p16, p16, p16, x2d, mod_l, m_norm_g, w_out, final_g)


def kernel(x, c, ctx, c_ctx, w_mod, b_mod, norm_g, w_in, b_gate, conv_qk_w, conv_qk_b, m_norm_g, conv_r_w, conv_r_b,
           w_rg, b_rg, lru_lambda, w_out, final_g):
    bsz, t, d = x.shape
    depth = w_mod.shape[0]
    nh = M_HEADS

    w_t = jnp.swapaxes(w_in, 1, 2)
    w_main, w_gate = _win_cast(w_t, 0, WA_SECS * SEC, 4 * nh)
    b_mod3 = b_mod[:, None, :]
    bg2 = jnp.zeros((depth, SUBLANES, GATE_W), F32).at[:, 0, :4 * nh].set(b_gate)
    w_out_b = _cast_bf16(w_out)
    norm_g3 = norm_g[:, None, :]
    m_norm_g3 = m_norm_g[:, None, :]
    conv_qk_wh, conv_qk_bh = 0.5 * conv_qk_w, 0.5 * conv_qk_b[:, None, :]
    conv_r_wh, conv_r_bh = 0.5 * conv_r_w, 0.5 * conv_r_b[:, None, :]
    b_rg4 = 0.5 * b_rg.reshape(depth, 2, 2, R_WIDTH)
    fg = final_g[None, :]

    cvec = jnp.concatenate([c, c_ctx[None, :], jnp.zeros((SUBLANES - bsz - 1, d), F32)], axis=0)
    mod = _modulation(cvec, w_mod, b_mod3, 0)

    tm_in = 1024
    tm_ctx = bsz * CTX_LEN
    tm_out = 512
    x2d = x.reshape(bsz * t, d)
    c2d = ctx.reshape(bsz * CTX_LEN, d)
    lat_row_in = lambda i: i // (t // tm_in)
    lat_row_out = lambda i: i // (t // tm_out)
    ctx_row = lambda i: bsz

    for l in range(depth):
        last = l == depth - 1
        mod_l = mod.reshape(SUBLANES, 3, 1, d)
        nxt = None if last else (cvec, w_mod, b_mod3, w_t, 4 * nh)
        lat = _in_proj(x2d, mod_l, lat_row_in, norm_g3, w_main, w_gate, l, tm_in, nxt)
        px32, px16, gx = lat[:3]
        pc32, pc16, gc = _in_proj(c2d, mod_l, ctx_row, norm_g3, w_main, w_gate, l, tm_ctx)
        if not last:
            mod, w_main, w_gate = lat[3:]
        hc, hhi, hlo = _mlstm(pc32.reshape(N_SEC32, bsz, CTX_LEN, SEC), gc.reshape(bsz, CTX_LEN, GATE_W),
                              px32.reshape(N_SEC32, bsz, t, SEC), gx.reshape(bsz, t, GATE_W),
                              conv_qk_wh, conv_qk_bh, bg2, l)
        grid_rows = bsz * t // GRID_W
        ycf, ycb, yf, yb = _rglru(pc16.reshape(N_SEC - N_SEC32, bsz, CTX_LEN, SEC),
                                  px16.reshape(N_SEC - N_SEC32, bsz, t, SEC),
                                  conv_r_wh, conv_r_bh, w_rg, b_rg4, lru_lambda, l)
        x2d = _out_proj((hlo.reshape(grid_rows, GRID_W // 2, M_WIDTH), hhi.reshape(grid_rows, GRID_W // 2, M_WIDTH)),
                        yf.reshape(bsz * t, R_WIDTH), yb.reshape(bsz * t, R_WIDTH),
                        px16, x2d, mod_l, lat_row_out, m_norm_g3, w_out_b, fg, l, tm_out, last)
        if not last:
            c2d = _out_proj((hc.reshape(bsz * CTX_LEN, M_WIDTH),),
                            ycf.reshape(bsz * CTX_LEN, R_WIDTH), ycb.reshape(bsz * CTX_LEN, R_WIDTH),
                            pc16, c2d, mod_l, ctx_row, m_norm_g3, w_out_b, fg, l, tm_out, False)
    return x2d.reshape(bsz, t, d)
```

```python
import functools

import jax
import jax.numpy as jnp
from jax import lax
from jax.experimental import pallas as pl
from jax.experimental.pallas import tpu as pltpu

D_MODEL = 2048
DEPTH = 4
CTX_LEN = 256
GRID_W = 64
M_WIDTH = 1024
R_WIDTH = 1024
M_HEADS = 4
M_DV = 256
M_DQK = 128
M_QK = 512
R_BLOCKS = 16
R_BLOCK = 64
CONV_W = 4
LRU_C = 8.0
EPS = 1e-6

LANES = 128
SUBLANES = 8
BF16_ROWS = 16
SEC = 1024
N_SEC = 6
N_SEC32 = 2
S32_QK, S32_V = 0, 1
S16_O, S16_ZM, S16_XL, S16_ZL = 0, 1, 2, 3
WA_SECS = 4
GATE_W = LANES
CTX_CHUNK = CTX_LEN
COL_GROUP = SUBLANES
LAT_CHUNK = COL_GROUP * GRID_W
LRU_BLK = 256
LRU_PER_STEP = 4
LRU_TILE = 256
VMEM_LIMIT = 56 * 1024 * 1024
LOG2E = 1.4426950408889634

F32 = jnp.float32
BF16 = jnp.bfloat16


def _cparams(sem):
    return pltpu.CompilerParams(dimension_semantics=sem, vmem_limit_bytes=VMEM_LIMIT)


def _silu_half(h):
    return h + h * jnp.tanh(h)


def _silu_t(x):
    return _silu_half(0.5 * x)


def _dot_nt(a, b):
    return lax.dot_general(a, b, (((1,), (1,)), ((), ())), preferred_element_type=F32)


def _sigmoid_t(x):
    return 0.5 * jnp.tanh(0.5 * x) + 0.5


def _mod_kernel(c_ref, w_ref, b_ref, o_ref):
    c = c_ref[...]
    s = (c * jax.nn.sigmoid(c)).astype(BF16)
    o_ref[...] = jnp.dot(s, w_ref[...].astype(BF16), preferred_element_type=F32) + b_ref[...]


def _modulation(cvec, w_mod, b_mod3, layer):
    _, d, n = w_mod.shape
    tn = 1024
    return pl.pallas_call(
        _mod_kernel,
        grid=(n // tn,),
        in_specs=[
            pl.BlockSpec((SUBLANES, d), lambda j: (0, 0)),
            pl.BlockSpec((None, d, tn), lambda j: (layer, 0, j)),
            pl.BlockSpec((None, 1, tn), lambda j: (layer, 0, j)),
        ],
        out_specs=pl.BlockSpec((SUBLANES, tn), lambda j: (0, j)),
        out_shape=jax.ShapeDtypeStruct((SUBLANES, n), F32),
        compiler_params=_cparams(("arbitrary",)),
        name="mod",
    )(cvec, w_mod, b_mod3)


def _cast_rows(r, a_ref, b_ref, o_ref, g_ref, n_direct, skip):
    tr = o_ref.shape[0]

    @pl.when(r < n_direct)
    def _():
        o_ref[...] = a_ref[...].astype(BF16)

    @pl.when(r >= n_direct)
    def _():
        o_ref[:tr - skip, :] = a_ref[skip:, :].astype(BF16)
        o_ref[tr - skip:, :] = b_ref[...].astype(BF16)

    @pl.when(r == n_direct)
    def _():
        g_ref[0:skip, :] = a_ref[0:skip, :].astype(BF16)
        g_ref[skip:, :] = jnp.zeros((g_ref.shape[0] - skip, g_ref.shape[1]), BF16)


def _in_proj_kernel(x_ref, sh_ref, sc_ref, g_ref, w_ref, wg_ref, *refs, prep):
    n = pl.program_id(1)
    if prep is None:
        p32_ref, p16_ref, gate_ref, h_scr = refs

        def prep_step():
            pass
    else:
        c_ref, wm_ref, bm_ref, wa_ref, wb_ref, p32_ref, p16_ref, gate_ref, modn_ref, wn_ref, wgn_ref, h_scr = refs
        n_direct, skip = prep
        k = pl.program_id(0) * N_SEC + n

        @pl.when(k == n_direct)
        def _():
            wgn_ref[0:skip, :] = wa_ref[0:skip, :].astype(BF16)
            wgn_ref[skip:, :] = jnp.zeros((wgn_ref.shape[0] - skip, wgn_ref.shape[1]), BF16)

        def prep_step():
            c = c_ref[...]
            modn_ref[...] = jnp.dot((c * jax.nn.sigmoid(c)).astype(BF16), wm_ref[...].astype(BF16),
                                    preferred_element_type=F32) + bm_ref[...]
            a = wa_ref[...]
            below = jnp.concatenate([a[skip:, :], wb_ref[...]], axis=0)
            wn_ref[...] = jnp.where(k < n_direct, a, below).astype(BF16)

    @pl.when(n == 0)
    def _():
        x = x_ref[...]
        ms = jnp.mean(x * x, axis=-1, keepdims=True)
        y = x * lax.rsqrt(ms + EPS) * g_ref[...]
        h = (y * (1.0 + sc_ref[...]) + sh_ref[...]).astype(BF16)
        h_scr[...] = h
        gate_ref[...] = _dot_nt(h, wg_ref[...])

    @pl.when(n < N_SEC32)
    def _():
        prep_step()
        p32_ref[...] = _dot_nt(h_scr[...], w_ref[...])

    @pl.when(n >= N_SEC32)
    def _():
        prep_step()
        p16_ref[...] = _dot_nt(h_scr[...], w_ref[...]).astype(BF16)


def _in_proj(x2d, mod_l, row_of_tile, norm_g, w_main, w_gate, layer, tm, nxt=None):
    m, d = x2d.shape
    n_tiles = m // tm
    in_specs = [
        pl.BlockSpec((tm, d), lambda i, n: (i, 0)),
        pl.BlockSpec((None, None, 1, d), lambda i, n: (row_of_tile(i), 0, 0, 0)),
        pl.BlockSpec((None, None, 1, d), lambda i, n: (row_of_tile(i), 1, 0, 0)),
        pl.BlockSpec((None, 1, d), lambda i, n: (layer, 0, 0)),
        pl.BlockSpec((SEC, d), lambda i, n: (n, 0)),
        pl.BlockSpec((GATE_W, d), lambda i, n: (0, 0)),
    ]
    out_specs = [
        pl.BlockSpec((None, tm, SEC), lambda i, n: (jnp.minimum(n, N_SEC32 - 1), i, 0)),
        pl.BlockSpec((None, tm, SEC), lambda i, n: (jnp.maximum(n - N_SEC32, 0), i, 0)),
        pl.BlockSpec((tm, GATE_W), lambda i, n: (i, 0)),
    ]
    out_shape = [
        jax.ShapeDtypeStruct((N_SEC32, m, SEC), F32),
        jax.ShapeDtypeStruct((N_SEC - N_SEC32, m, SEC), BF16),
        jax.ShapeDtypeStruct((m, GATE_W), F32),
    ]
    operands = [x2d, mod_l, mod_l, norm_g, w_main, w_gate]
    prep = None
    if nxt is not None:
        cvec, w_mod, b_mod3, w_t, skip = nxt
        steps = n_tiles * N_SEC
        n_mod = w_mod.shape[2]
        tr, tn = (N_SEC * SEC) // steps, n_mod // steps
        assert tr * steps == N_SEC * SEC and tr % skip == 0 and tn * steps == n_mod and tn % LANES == 0
        prep = (WA_SECS * SEC // tr, skip)

        def step(i, n):
            return i * N_SEC + n

        in_specs += [
            pl.BlockSpec((SUBLANES, d), lambda i, n: (0, 0)),
            pl.BlockSpec((None, d, tn), lambda i, n: (layer + 1, 0, step(i, n))),
            pl.BlockSpec((None, 1, tn), lambda i, n: (layer + 1, 0, step(i, n))),
            pl.BlockSpec((None, tr, d), lambda i, n: (layer + 1, step(i, n), 0)),
            pl.BlockSpec((None, skip, d), lambda i, n: (layer + 1, (step(i, n) + 1) * (tr // skip), 0)),
        ]
        out_specs += [
            pl.BlockSpec((SUBLANES, tn), lambda i, n: (0, step(i, n))),
            pl.BlockSpec((tr, d), lambda i, n: (step(i, n), 0)),
            pl.BlockSpec((GATE_W, d), lambda i, n: (0, 0)),
        ]
        out_shape += [
            jax.ShapeDtypeStruct((SUBLANES, n_mod), F32),
            jax.ShapeDtypeStruct((N_SEC * SEC, d), BF16),
            jax.ShapeDtypeStruct((GATE_W, d), BF16),
        ]
        operands += [cvec, w_mod, b_mod3, w_t, w_t]
    return pl.pallas_call(
        functools.partial(_in_proj_kernel, prep=prep),
        grid=(n_tiles, N_SEC),
        in_specs=in_specs,
        out_specs=out_specs,
        out_shape=out_shape,
        scratch_shapes=[pltpu.VMEM((tm, d), BF16)],
        compiler_params=_cparams(("arbitrary", "arbitrary")),
        name="in_proj",
    )(*operands)


def _win_cast_kernel(a_ref, b_ref, o_ref, g_ref, *, n_direct, skip):
    _cast_rows(pl.program_id(0), a_ref, b_ref, o_ref, g_ref, n_direct, skip)


def _win_cast(w_t, layer, split, skip):
    _, n_in, d = w_t.shape
    tr = 256
    return pl.pallas_call(
        functools.partial(_win_cast_kernel, n_direct=split // tr, skip=skip),
        grid=((n_in - skip) // tr,),
        in_specs=[pl.BlockSpec((None, tr, d), lambda r: (layer, r, 0)),
                  pl.BlockSpec((None, skip, d), lambda r: (layer, (r + 1) * (tr // skip), 0))],
        out_specs=[pl.BlockSpec((tr, d), lambda r: (r, 0)),
                   pl.BlockSpec((GATE_W, d), lambda r: (0, 0))],
        out_shape=[jax.ShapeDtypeStruct((n_in - skip, d), BF16),
                   jax.ShapeDtypeStruct((GATE_W, d), BF16)],
        compiler_params=_cparams(("arbitrary",)),
        name="win_cast",
    )(w_t, w_t)


def _cast_kernel(w_ref, o_ref):
    o_ref[...] = w_ref[...].astype(BF16)


def _cast_bf16(w):
    depth, r, c = w.shape
    tr = 512
    return pl.pallas_call(
        _cast_kernel,
        grid=(depth, r // tr),
        in_specs=[pl.BlockSpec((None, tr, c), lambda l, j: (l, j, 0))],
        out_specs=pl.BlockSpec((None, tr, c), lambda l, j: (l, j, 0)),
        out_shape=jax.ShapeDtypeStruct((depth, r, c), BF16),
        compiler_params=_cparams(("arbitrary", "arbitrary")),
        name="cast_bf16",
    )(w)


def _shifted(x, k, reverse, ident):
    n = x.shape[0]
    row = lax.broadcasted_iota(jnp.int32, x.shape, 0)
    if reverse:
        return jnp.where(row < n - k, pltpu.roll(x, n - k, axis=0), ident)
    return jnp.where(row >= k, pltpu.roll(x, k, axis=0), ident)


def _scan_rows(x, op, ident, reverse):
    n = x.shape[0]
    k = 1
    while k < n:
        x = op(x, _shifted(x, k, reverse, ident))
        k *= 2
    return x


def _scan_colmajor(x, op, ident, reverse):
    n = x.shape[0]
    k = SUBLANES
    while k < n:
        x = op(x, _shifted(x, k, reverse, ident))
        k *= 2
    tot = x[0:SUBLANES, :] if reverse else x[n - SUBLANES:n, :]
    tot = _scan_rows(tot, op, ident, reverse)
    tot = _shifted(tot, 1, reverse, ident)
    return op(x, jnp.concatenate([tot] * (n // SUBLANES), axis=0))


def _conv_rows(xs_ref, n, w_ref, b_ref, step):
    base = SUBLANES - step
    acc = b_ref[...] + w_ref[0:1, :] * xs_ref[base:base + n, :]
    for j in range(1, CONV_W):
        acc = acc + w_ref[j:j + 1, :] * xs_ref[base + j * step:base + j * step + n, :]
    return acc


def _mlstm_dir(d, qk, v, gates, bg_ref, ct_ref, m_ref, scan, posdiff, write_h):
    n = qk.shape[0]
    rev = d == 1
    gi = gates + bg_ref[0:1, :]
    lf = jax.nn.log_sigmoid(pltpu.roll(gi, LANES - M_HEADS, axis=1))
    bc = scan(lf, jnp.add, 0.0, rev)
    a = gi - bc
    m_prev = m_ref[d, 0:1, :]
    mm = jnp.maximum(scan(a, jnp.maximum, -jnp.inf, rev), m_prev)
    inter = jnp.exp(m_prev - mm)
    em = jnp.exp(-(bc + mm))
    last = 0 if rev else n - 1
    mm_last = mm[last:last + 1, :]
    m_new = bc[last:last + 1, :] + mm_last
    decay = jnp.exp(m_prev - mm_last)
    wcol = jnp.exp(a - mm_last)
    a_t = (a * LOG2E).T
    mm2 = mm * LOG2E
    mask = (posdiff <= 0) if rev else (posdiff >= 0)
    ones = jnp.ones((n, LANES), BF16)
    for h in range(M_HEADS):
        e = 2 * M_HEADS * d + h
        st = d * M_HEADS + h
        qf = qk[:, h * M_DQK:(h + 1) * M_DQK]
        q = qf.astype(BF16)
        kf = qk[:, M_QK + h * M_DQK:M_QK + (h + 1) * M_DQK] * (M_DQK ** -0.5)
        vaug = jnp.concatenate([v[:, h * M_DV:(h + 1) * M_DV].astype(BF16), ones], axis=1)
        dmat = jnp.where(mask, jnp.exp2(a_t[e:e + 1, :] - mm2[:, e:e + 1]), 0.0)
        s = lax.dot_general(q, kf.astype(BF16), (((1,), (1,)), ((), ())), preferred_element_type=F32)
        sw = (s * dmat).astype(BF16)
        ct = ct_ref[st]
        qi = (qf * inter[:, e:e + 1]).astype(BF16)
        num = jnp.dot(jnp.concatenate([sw, qi], axis=1), jnp.concatenate([vaug, ct.astype(BF16)], axis=0),
                      preferred_element_type=F32)
        den = jnp.maximum(jnp.abs(num[:, M_DV:]), em[:, e:e + 1])
        write_h(h, num[:, :M_DV] / jnp.concatenate([den, den], axis=1))
        kw = (kf * wcol[:, e:e + 1]).astype(BF16)
        upd = lax.dot_general(kw, vaug, (((0,), (0,)), ((), ())), preferred_element_type=F32)
        ct_ref[st] = decay[:, e:e + 1] * ct + upd
    m_ref[d, 0:1, :] = m_new


def _mlstm_kernel(cqk_ref, cv_ref, cg_ref,
                  qf_ref, pf_ref, nf_ref, vf_ref, gf_ref,
                  qb_ref, pb_ref, nb_ref, vb_ref, gb_ref,
                  cw_ref, cb_ref, bg_ref,
                  hc_ref, hhi_ref, hlo_ref,
                  xs_ref, ct_ref, m_ref, pd_ref, park_ref, *, n_lat):
    s = pl.program_id(1)
    lc = LAT_CHUNK
    zero_rows = jnp.zeros((SUBLANES, 2 * M_QK), F32)

    @pl.when(s == 0)
    def _():
        ct_ref[...] = jnp.zeros_like(ct_ref)
        m_ref[...] = jnp.zeros_like(m_ref)
        park_ref[...] = jnp.zeros_like(park_ref)
        row = lax.broadcasted_iota(jnp.int32, (lc, lc), 0)
        col = lax.broadcasted_iota(jnp.int32, (lc, lc), 1)
        pos_r = (row % COL_GROUP) * GRID_W + row // COL_GROUP
        pos_c = (col % COL_GROUP) * GRID_W + col // COL_GROUP
        pd_ref[...] = pos_r - pos_c
        xs_ref[0:SUBLANES, :] = zero_rows
        xs_ref[SUBLANES + CTX_CHUNK:2 * SUBLANES + CTX_CHUNK, :] = zero_rows
        xs_ref[SUBLANES:SUBLANES + CTX_CHUNK, :] = cqk_ref[...]
        qk = _silu_half(_conv_rows(xs_ref, CTX_CHUNK, cw_ref, cb_ref, 1))
        v = cv_ref[...]
        g = cg_ref[...]
        crow = lax.broadcasted_iota(jnp.int32, (CTX_CHUNK, CTX_CHUNK), 0)
        ccol = lax.broadcasted_iota(jnp.int32, (CTX_CHUNK, CTX_CHUNK), 1)
        for d in (0, 1):
            def write_h(h, val, d=d):
                lanes = slice(h * M_DV, (h + 1) * M_DV)
                hc_ref[:, lanes] = val if d == 0 else hc_ref[:, lanes] + val
            _mlstm_dir(d, qk, v, g, bg_ref, ct_ref, m_ref, _scan_rows, crow - ccol, write_h)

    @pl.when(s > 0)
    def _():
        sub = lax.broadcasted_iota(jnp.int32, (SUBLANES, 2 * M_QK), 0)
        for d, q_ref, p_ref, n_ref, v_ref, g_ref, out_ref in (
                (0, qf_ref, pf_ref, nf_ref, vf_ref, gf_ref, hhi_ref),
                (1, qb_ref, pb_ref, nb_ref, vb_ref, gb_ref, hlo_ref)):
            j = (s - 1) if d == 0 else (n_lat - s)
            has_prev = (j > 0).astype(F32)
            has_next = (j < n_lat - 1).astype(F32)
            x = q_ref[...].reshape(lc, 2 * M_QK)
            xs_ref[SUBLANES:SUBLANES + lc, :] = x
            x_last = x[lc - SUBLANES:lc, :]
            xs_ref[0:SUBLANES, :] = jnp.where(sub == 0, pltpu.roll(p_ref[...], 1, axis=0) * has_prev,
                                              pltpu.roll(x_last, 1, axis=0))
            for k in range(2):
                xs_ref[SUBLANES + lc + k * SUBLANES:2 * SUBLANES + lc + k * SUBLANES, :] = jnp.where(
                    sub == SUBLANES - 1, pltpu.roll(n_ref[k], SUBLANES - 1, axis=0) * has_next,
                    pltpu.roll(x[k * SUBLANES:(k + 1) * SUBLANES, :], SUBLANES - 1, axis=0))
            qk = _silu_half(_conv_rows(xs_ref, lc, cw_ref, cb_ref, SUBLANES))
            v = v_ref[...].reshape(lc, M_WIDTH)
            g = g_ref[...].reshape(lc, GATE_W)

            def write_h(h, val, out_ref=out_ref, j=j):
                lanes = slice(h * M_DV, (h + 1) * M_DV)
                other = park_ref[j, :, lanes].astype(F32)
                park_ref[j, :, lanes] = val.astype(BF16)
                out_ref[:, :, lanes] = (val + other).reshape(GRID_W, COL_GROUP, M_DV)
            _mlstm_dir(d, qk, v, g, bg_ref, ct_ref, m_ref, _scan_colmajor, pd_ref[...], write_h)


def _mlstm(pc, gc, px, gx, conv_w, conv_b, bg2, layer):
    _, bsz, t, _ = px.shape
    rows = t // GRID_W
    n_lat = GRID_W // COL_GROUP
    pxv = px.reshape(N_SEC32, bsz, rows, GRID_W, SEC)
    gxv = gx.reshape(bsz, rows, GRID_W, GATE_W)

    def jf(s):
        return jnp.maximum(s - 1, 0)

    def jb(s):
        return jnp.minimum(n_lat - s, n_lat - 1)

    def lat_specs(jfun):
        return [
            pl.BlockSpec((None, None, rows, COL_GROUP, SEC), lambda b, s: (S32_QK, b, 0, jfun(s), 0)),
            pl.BlockSpec((None, None, None, COL_GROUP, SEC),
                         lambda b, s: (S32_QK, b, rows - 1, jnp.maximum(jfun(s) - 1, 0), 0)),
            pl.BlockSpec((None, None, 2, COL_GROUP, SEC),
                         lambda b, s: (S32_QK, b, 0, jnp.minimum(jfun(s) + 1, n_lat - 1), 0)),
            pl.BlockSpec((None, None, rows, COL_GROUP, SEC), lambda b, s: (S32_V, b, 0, jfun(s), 0)),
            pl.BlockSpec((None, rows, COL_GROUP, GATE_W), lambda b, s: (b, 0, jfun(s), 0)),
        ]

    in_specs = [
        pl.BlockSpec((None, None, CTX_LEN, SEC), lambda b, s: (S32_QK, b, 0, 0)),
        pl.BlockSpec((None, None, CTX_LEN, SEC), lambda b, s: (S32_V, b, 0, 0)),
        pl.BlockSpec((None, CTX_LEN, GATE_W), lambda b, s: (b, 0, 0)),
    ] + lat_specs(jf) + lat_specs(jb) + [
        pl.BlockSpec((None, CONV_W, SEC), lambda b, s: (layer, 0, 0)),
        pl.BlockSpec((None, 1, SEC), lambda b, s: (layer, 0, 0)),
        pl.BlockSpec((None, SUBLANES, LANES), lambda b, s: (layer, 0, 0)),
    ]
    half = n_lat // 2
    out_specs = [
        pl.BlockSpec((None, CTX_LEN, M_WIDTH), lambda b, s: (b, 0, 0)),
        pl.BlockSpec((None, rows, COL_GROUP, M_WIDTH), lambda b, s: (b, 0, jnp.clip(s - 1 - half, 0, half - 1), 0)),
        pl.BlockSpec((None, rows, COL_GROUP, M_WIDTH), lambda b, s: (b, 0, jnp.clip(n_lat - s, 0, half - 1), 0)),
    ]
    out_shape = [
        jax.ShapeDtypeStruct((bsz, CTX_LEN, M_WIDTH), F32),
        jax.ShapeDtypeStruct((bsz, rows, GRID_W // 2, M_WIDTH), F32),
        jax.ShapeDtypeStruct((bsz, rows, GRID_W // 2, M_WIDTH), F32),
    ]
    return pl.pallas_call(
        functools.partial(_mlstm_kernel, n_lat=n_lat),
        grid=(bsz, n_lat + 1),
        in_specs=in_specs,
        out_specs=out_specs,
        out_shape=out_shape,
        scratch_shapes=[
            pltpu.VMEM((LAT_CHUNK + 3 * SUBLANES, 2 * M_QK), F32),
            pltpu.VMEM((2 * M_HEADS, M_DQK, M_DV + LANES), F32),
            pltpu.VMEM((2, SUBLANES, LANES), F32),
            pltpu.VMEM((LAT_CHUNK, LAT_CHUNK), jnp.int32),
            pltpu.VMEM((n_lat, LAT_CHUNK, M_WIDTH), BF16),
        ],
        compiler_params=_cparams(("arbitrary", "arbitrary")),
        name="mlstm",
    )(pc, pc, gc, pxv, pxv, pxv, pxv, gxv, pxv, pxv, pxv, pxv, gxv, conv_w, conv_b, bg2)


def _lru_conv(x_bf, prev_row, next_rows, pm_ref, cw_ref, cb_ref):
    n = x_bf.shape[0]
    s8 = SUBLANES
    xp = jnp.dot(pm_ref[0], x_bf, preferred_element_type=F32)
    sub = lax.broadcasted_iota(jnp.int32, (s8, x_bf.shape[1]), 0)

    def from_next_segment(v, fill):
        return jnp.where(sub == s8 - 1, fill, pltpu.roll(v, s8 - 1, axis=0))

    def from_prev_segment(v, fill):
        return jnp.where(sub == 0, fill, pltpu.roll(v, 1, axis=0))

    first, second, last = xp[0:s8, :], xp[s8:2 * s8, :], xp[n - s8:, :]
    xm1 = jnp.concatenate([from_prev_segment(last, prev_row), xp[:n - s8, :]], axis=0)
    xp1 = jnp.concatenate([xp[s8:, :], from_next_segment(first, next_rows[0:1, :])], axis=0)
    xp2 = jnp.concatenate([xp[2 * s8:, :], from_next_segment(first, next_rows[0:1, :]),
                           from_next_segment(second, next_rows[1:2, :])], axis=0)
    return (cb_ref[...] + cw_ref[0:1, :] * xm1 + cw_ref[1:2, :] * xp
            + cw_ref[2:3, :] * xp1 + cw_ref[3:4, :] * xp2)


def _lru_gates(d, xc, wd_ref, br_ref, sp_ref, a_ref, u_ref):
    xb = xc.astype(BF16)
    for j in range(R_WIDTH // LRU_TILE):
        sl = slice(j * LRU_TILE, (j + 1) * LRU_TILE)
        xj = xb[:, sl]
        tr = jnp.tanh(jnp.dot(xj, wd_ref[d, 0, j], preferred_element_type=F32) + br_ref[d, 0:1, sl])
        ti = jnp.tanh(jnp.dot(xj, wd_ref[d, 1, j], preferred_element_type=F32) + br_ref[d, 1:2, sl])
        sp = sp_ref[d:d + 1, sl]
        nla = tr * sp + sp
        a = jnp.exp2(nla * (-LOG2E))
        a_ref[:, sl] = a
        xh = xc[:, sl]
        z = jnp.tanh(nla) * (a * a + 1.0)
        root = jnp.where(z > 0.0, z * lax.rsqrt(z), 0.0)
        u_ref[:, sl] = root * (ti * xh + xh)


def _lru_scan(d, a_ref, u_ref, h_ref, pm_ref, out_ref, n):
    rev = d == 1
    s8 = SUBLANES
    groups = n // s8
    sub = lax.broadcasted_iota(jnp.int32, (s8, R_WIDTH), 0)

    def body(g, carry):
        h, acc = carry
        gg = (groups - 1 - g) if rev else g
        r0 = pl.multiple_of(gg * s8, s8)
        a = a_ref[pl.ds(r0, s8), :]
        h = a * h + u_ref[pl.ds(r0, s8), :]
        acc = a * acc
        u_ref[pl.ds(r0, s8), :] = h
        a_ref[pl.ds(r0, s8), :] = acc
        return h, acc

    u, a = lax.fori_loop(0, groups, body, (jnp.zeros((s8, R_WIDTH), F32), jnp.ones((s8, R_WIDTH), F32)),
                         unroll=4)
    k = 1
    while k < s8:
        if rev:
            ok = sub < s8 - k
            a_s = pltpu.roll(a, s8 - k, axis=0)
            u_s = pltpu.roll(u, s8 - k, axis=0)
        else:
            ok = sub >= k
            a_s = pltpu.roll(a, k, axis=0)
            u_s = pltpu.roll(u, k, axis=0)
        u = u + a * jnp.where(ok, u_s, 0.0)
        a = a * jnp.where(ok, a_s, 1.0)
        k *= 2
    c0 = h_ref[d, 0:1, :]
    after = u + a * c0
    if rev:
        entry = jnp.where(sub == s8 - 1, c0, pltpu.roll(after, s8 - 1, axis=0))
        h_ref[d, 0:1, :] = after[0:1, :]
    else:
        entry = jnp.where(sub == 0, c0, pltpu.roll(after, 1, axis=0))
        h_ref[d, 0:1, :] = after[s8 - 1:s8, :]
    hs = u_ref[...] + a_ref[...] * jnp.concatenate([entry] * groups, axis=0)
    out_ref[...] = jnp.dot(pm_ref[1], hs.astype(BF16), preferred_element_type=F32).astype(out_ref.dtype)


def _rglru_kernel(cx_ref, xf_ref, pf_ref, nf_ref, xb_ref, pb_ref, nb_ref,
                  cw_ref, cb_ref, wr_ref, br_ref, lam_ref,
                  ycf_ref, ycb_ref, yf_ref, yb_ref,
                  pm_ref, a_ref, u_ref, h_ref, sp_ref, wd_ref, xc_ref, *, n_lat):
    s = pl.program_id(1)
    seg = LRU_BLK // SUBLANES

    @pl.when(s == 0)
    def _():
        h_ref[...] = jnp.zeros_like(h_ref)
        sp_ref[...] = (0.5 * LRU_C) * jax.nn.softplus(-lam_ref[...])
        wd_ref[...] = jnp.zeros_like(wd_ref)
        per = LRU_TILE // R_BLOCK
        for dd in range(2):
            for g in range(2):
                for blk in range(R_BLOCKS):
                    j, p = divmod(blk, per)
                    rows = slice(p * R_BLOCK, (p + 1) * R_BLOCK)
                    wd_ref[dd, g, j, rows, rows] = wr_ref[dd, g, blk].astype(BF16)
        row = lax.broadcasted_iota(jnp.int32, (LRU_BLK, LRU_BLK), 0)
        col = lax.broadcasted_iota(jnp.int32, (LRU_BLK, LRU_BLK), 1)
        pm_ref[0] = jnp.where(col == (row % SUBLANES) * seg + row // SUBLANES, 1.0, 0.0).astype(BF16)
        pm_ref[1] = jnp.where(row == (col % SUBLANES) * seg + col // SUBLANES, 1.0, 0.0).astype(BF16)
        zero_rows = jnp.zeros((2, R_WIDTH), F32)
        xc = _lru_conv(cx_ref[...], zero_rows[0:1, :], zero_rows, pm_ref, cw_ref, cb_ref)
        for d, out_ref in ((0, ycf_ref), (1, ycb_ref)):
            _lru_gates(d, xc, wd_ref, br_ref, sp_ref, a_ref, u_ref)
            _lru_scan(d, a_ref, u_ref, h_ref, pm_ref, out_ref, LRU_BLK)

    @pl.when(jnp.logical_and(s > 0, s <= n_lat // 2))
    def _():
        for d, x_ref, p_ref, n_ref in ((0, xf_ref, pf_ref, nf_ref), (1, xb_ref, pb_ref, nb_ref)):
            j = (s - 1) if d == 0 else (n_lat - s)
            has_prev = (j > 0).astype(F32)
            has_next = (j < n_lat - 1).astype(F32)
            for k in range(LRU_PER_STEP):
                lo, hi = k * LRU_BLK, (k + 1) * LRU_BLK
                if k == 0:
                    prev_row = p_ref[...].astype(F32)[BF16_ROWS - 1:BF16_ROWS, :] * has_prev
                else:
                    prev_row = x_ref[lo - BF16_ROWS:lo, :].astype(F32)[BF16_ROWS - 1:BF16_ROWS, :]
                if k == LRU_PER_STEP - 1:
                    next_rows = n_ref[...].astype(F32)[0:2, :] * has_next
                else:
                    next_rows = x_ref[hi:hi + BF16_ROWS, :].astype(F32)[0:2, :]
                xc_ref[j * LRU_PER_STEP + k] = _lru_conv(x_ref[lo:hi, :], prev_row, next_rows, pm_ref, cw_ref, cb_ref)

    @pl.when(s > 0)
    def _():
        for d, out_ref in ((0, yf_ref), (1, yb_ref)):
            j = (s - 1) if d == 0 else (n_lat - s)
            for k in (range(LRU_PER_STEP) if d == 0 else reversed(range(LRU_PER_STEP))):
                _lru_gates(d, xc_ref[j * LRU_PER_STEP + k], wd_ref, br_ref, sp_ref, a_ref, u_ref)
                _lru_scan(d, a_ref, u_ref, h_ref, pm_ref, out_ref.at[pl.ds(k * LRU_BLK, LRU_BLK), :], LRU_BLK)


def _rglru(pc, px, conv_w, conv_b, w_rg, b_rg, lam, layer):
    _, bsz, t, _ = px.shape
    rows_step = LRU_BLK * LRU_PER_STEP
    n_lat = t // rows_step
    per_blk = rows_step // BF16_ROWS
    n_halo = t // BF16_ROWS

    def jf(s):
        return jnp.maximum(s - 1, 0)

    def jb(s):
        return jnp.minimum(n_lat - s, n_lat - 1)

    def jfx(s):
        return jnp.clip(s - 1, 0, n_lat // 2 - 1)

    def jbx(s):
        return jnp.clip(n_lat - s, n_lat // 2, n_lat - 1)

    def lat_specs(jfun):
        return [
            pl.BlockSpec((None, None, rows_step, SEC), lambda b, s: (S16_XL, b, jfun(s), 0)),
            pl.BlockSpec((None, None, BF16_ROWS, SEC),
                         lambda b, s: (S16_XL, b, jnp.maximum(jfun(s) * per_blk - 1, 0), 0)),
            pl.BlockSpec((None, None, BF16_ROWS, SEC),
                         lambda b, s: (S16_XL, b, jnp.minimum((jfun(s) + 1) * per_blk, n_halo - 1), 0)),
        ]

    in_specs = [pl.BlockSpec((None, None, CTX_LEN, SEC), lambda b, s: (S16_XL, b, 0, 0))]
    in_specs += lat_specs(jfx) + lat_specs(jbx) + [
        pl.BlockSpec((None, CONV_W, SEC), lambda b, s: (layer, 0, 0)),
        pl.BlockSpec((None, 1, SEC), lambda b, s: (layer, 0, 0)),
        pl.BlockSpec((None,) + w_rg.shape[1:], lambda b, s: (layer, 0, 0, 0, 0, 0)),
        pl.BlockSpec((None,) + b_rg.shape[1:], lambda b, s: (layer, 0, 0, 0)),
        pl.BlockSpec((None,) + lam.shape[1:], lambda b, s: (layer, 0, 0)),
    ]
    out_specs = [
        pl.BlockSpec((None, CTX_LEN, R_WIDTH), lambda b, s: (b, 0, 0)),
        pl.BlockSpec((None, CTX_LEN, R_WIDTH), lambda b, s: (b, 0, 0)),
        pl.BlockSpec((None, rows_step, R_WIDTH), lambda b, s: (b, jf(s), 0)),
        pl.BlockSpec((None, rows_step, R_WIDTH), lambda b, s: (b, jb(s), 0)),
    ]
    out_shape = [
        jax.ShapeDtypeStruct((bsz, CTX_LEN, R_WIDTH), BF16),
        jax.ShapeDtypeStruct((bsz, CTX_LEN, R_WIDTH), BF16),
        jax.ShapeDtypeStruct((bsz, t, R_WIDTH), BF16),
        jax.ShapeDtypeStruct((bsz, t, R_WIDTH), BF16),
    ]
    return pl.pallas_call(
        functools.partial(_rglru_kernel, n_lat=n_lat),
        grid=(bsz, n_lat + 1),
        in_specs=in_specs,
        out_specs=out_specs,
        out_shape=out_shape,
        scratch_shapes=[
            pltpu.VMEM((2, LRU_BLK, LRU_BLK), BF16),
            pltpu.VMEM((LRU_BLK, R_WIDTH), F32),
            pltpu.VMEM((LRU_BLK, R_WIDTH), F32),
            pltpu.VMEM((2, SUBLANES, R_WIDTH), F32),
            pltpu.VMEM((2, R_WIDTH), F32),
            pltpu.VMEM((2, 2, R_WIDTH // LRU_TILE, LRU_TILE, LRU_TILE), BF16),
            pltpu.VMEM((t // LRU_BLK, LRU_BLK, R_WIDTH), F32),
        ],
        compiler_params=_cparams(("arbitrary", "arbitrary")),
        name="rglru",
    )(pc, px, px, px, px, px, px, conv_w, conv_b, w_rg, b_rg, lam)


def _out_proj_kernel(*refs, final, split_cols):
    if split_cols:
        hlo_ref, hhi_ref = refs[:2]
        hm = jnp.concatenate([hlo_ref[...], hhi_ref[...]], axis=1)
        hm = hm.reshape(hm.shape[0] * hm.shape[1], M_WIDTH)
        refs = refs[2:]
    else:
        hm = refs[0][...]
        refs = refs[1:]
    yf_ref, yb_ref, o_ref, zm_ref, zl_ref, x_ref, gt_ref, mg_ref, w_ref, fg_ref, out_ref = refs
    parts = []
    for h in range(M_HEADS):
        hh = hm[:, h * M_DV:(h + 1) * M_DV]
        parts.append(hh * lax.rsqrt(jnp.mean(hh * hh, axis=-1, keepdims=True) + EPS))
    hn = jnp.concatenate(parts, axis=1) * mg_ref[...]
    ym = hn * _sigmoid_t(o_ref[...].astype(F32)) * _silu_t(zm_ref[...].astype(F32))
    yr = (yf_ref[...].astype(F32) + yb_ref[...].astype(F32)) * _silu_t(zl_ref[...].astype(F32))
    y = jnp.concatenate([ym, yr], axis=1).astype(BF16)
    xn = x_ref[...] + gt_ref[...] * jnp.dot(y, w_ref[...], preferred_element_type=F32)
    if final:
        xn = xn * lax.rsqrt(jnp.mean(xn * xn, axis=-1, keepdims=True) + EPS) * fg_ref[...]
    out_ref[...] = xn


def _out_proj(h_parts, yf, yb, p16, x2d, mod_l, row_of_tile, m_norm_g, w_out, final_g, layer, tm, final):
    m, d = x2d.shape

    def tok(width):
        return pl.BlockSpec((tm, width), lambda i: (i, 0))

    def sec(k):
        return pl.BlockSpec((None, tm, SEC), lambda i: (k, i, 0))

    split_cols = len(h_parts) == 2
    if split_cols:
        h_specs = [pl.BlockSpec((tm // GRID_W, GRID_W // 2, M_WIDTH), lambda i: (i, 0, 0))] * 2
    else:
        h_specs = [tok(M_WIDTH)]
    return pl.pallas_call(
        functools.partial(_out_proj_kernel, final=final, split_cols=split_cols),
        grid=(m // tm,),
        in_specs=h_specs + [
            tok(R_WIDTH), tok(R_WIDTH),
            sec(S16_O), sec(S16_ZM), sec(S16_ZL),
            tok(d),
            pl.BlockSpec((None, None, 1, d), lambda i: (row_of_tile(i), 2, 0, 0)),
            pl.BlockSpec((None, 1, M_WIDTH), lambda i: (layer, 0, 0)),
            pl.BlockSpec((None, d, d), lambda i: (layer, 0, 0), pipeline_mode=pl.Buffered(1)),
            pl.BlockSpec((1, d), lambda i: (0, 0)),
        ],
        out_specs=tok(d),
        out_shape=jax.ShapeDtypeStruct((m, d), F32),
        compiler_params=_cparams(("arbitrary",)),
        name="out_proj",
    )(*h_parts, yf, yb, p16, p16, p16, x2d, mod_l, m_norm_g, w_out, final_g)


def kernel(x, c, ctx, c_ctx, w_mod, b_mod, norm_g, w_in, b_gate, conv_qk_w, conv_qk_b, m_norm_g, conv_r_w, conv_r_b,
           w_rg, b_rg, lru_lambda, w_out, final_g):
    bsz, t, d = x.shape
    depth = w_mod.shape[0]
    nh = M_HEADS

    w_t = jnp.swapaxes(w_in, 1, 2)
    w_main, w_gate = _win_cast(w_t, 0, WA_SECS * SEC, 4 * nh)
    b_mod3 = b_mod[:, None, :]
    bg2 = jnp.zeros((depth, SUBLANES, GATE_W), F32).at[:, 0, :4 * nh].set(b_gate)
    w_out_b = _cast_bf16(w_out)
    norm_g3 = norm_g[:, None, :]
    m_norm_g3 = m_norm_g[:, None, :]
    conv_qk_wh, conv_qk_bh = 0.5 * conv_qk_w, 0.5 * conv_qk_b[:, None, :]
    conv_r_wh, conv_r_bh = 0.5 * conv_r_w, 0.5 * conv_r_b[:, None, :]
    b_rg4 = 0.5 * b_rg.reshape(depth, 2, 2, R_WIDTH)
    fg = final_g[None, :]

    cvec = jnp.concatenate([c, c_ctx[None, :], jnp.zeros((SUBLANES - bsz - 1, d), F32)], axis=0)
    mod = _modulation(cvec, w_mod, b_mod3, 0)

    tm_in = 1024
    tm_ctx = bsz * CTX_LEN
    tm_out = 512
    x2d = x.reshape(bsz * t, d)
    c2d = ctx.reshape(bsz * CTX_LEN, d)
    lat_row_in = lambda i: i // (t // tm_in)
    lat_row_out = lambda i: i // (t // tm_out)
    ctx_row = lambda i: bsz

    for l in range(depth):
        last = l == depth - 1
        mod_l = mod.reshape(SUBLANES, 3, 1, d)
        nxt = None if last else (cvec, w_mod, b_mod3, w_t, 4 * nh)
        lat = _in_proj(x2d, mod_l, lat_row_in, norm_g3, w_main, w_gate, l, tm_in, nxt)
        px32, px16, gx = lat[:3]
        pc32, pc16, gc = _in_proj(c2d, mod_l, ctx_row, norm_g3, w_main, w_gate, l, tm_ctx)
        if not last:
            mod, w_main, w_gate = lat[3:]
        hc, hhi, hlo = _mlstm(pc32.reshape(N_SEC32, bsz, CTX_LEN, SEC), gc.reshape(bsz, CTX_LEN, GATE_W),
                              px32.reshape(N_SEC32, bsz, t, SEC), gx.reshape(bsz, t, GATE_W),
                              conv_qk_wh, conv_qk_bh, bg2, l)
        grid_rows = bsz * t // GRID_W
        ycf, ycb, yf, yb = _rglru(pc16.reshape(N_SEC - N_SEC32, bsz, CTX_LEN, SEC),
                                  px16.reshape(N_SEC - N_SEC32, bsz, t, SEC),
                                  conv_r_wh, conv_r_bh, w_rg, b_rg4, lru_lambda, l)
        x2d = _out_proj((hlo.reshape(grid_rows, GRID_W // 2, M_WIDTH), hhi.reshape(grid_rows, GRID_W // 2, M_WIDTH)),
                        yf.reshape(bsz * t, R_WIDTH), yb.reshape(bsz * t, R_WIDTH),
                        px16, x2d, mod_l, lat_row_out, m_norm_g3, w_out_b, fg, l, tm_out, last)
        if not last:
            c2d = _out_proj((hc.reshape(bsz * CTX_LEN, M_WIDTH),),
                            ycf.reshape(bsz * CTX_LEN, R_WIDTH), ycb.reshape(bsz * CTX_LEN, R_WIDTH),
                            pc16, c2d, mod_l, ctx_row, m_norm_g3, w_out_b, fg, l, tm_out, False)
    return x2d.reshape(bsz, t, d)
```
